```python
import jax, jax.numpy as jnp
from jax import lax
import numpy as np

D_MODEL = 1024
BATCH = 2
SEQ = 8192
DEPTH = 1

CHUNK = 64
EPS = 1e-6

ATT_HEADS = 16
ATT_KV_HEADS = 2
HEAD_DIM = 64
ATT_GROUP = ATT_HEADS // ATT_KV_HEADS
ATT_WIDTH = ATT_HEADS * HEAD_DIM
KV_WIDTH = ATT_KV_HEADS * HEAD_DIM
WINDOW = 128
WINDOW_CHUNKS = WINDOW // CHUNK
KEY_BLOCK = (WINDOW_CHUNKS + 1) * CHUNK
ROPE_DIM = HEAD_DIM // 4
ROPE_THETA = 500000.0

SG_BLOCK = 128
SG_GROUPS = 8
SG_WIDTH = 1024
SG_GROUP_WIDTH = SG_WIDTH // SG_GROUPS

N_BRANCHES = 2
IN_SIZES = [ATT_WIDTH, KV_WIDTH, KV_WIDTH, ATT_WIDTH,
            SG_WIDTH, SG_WIDTH, SG_WIDTH,
            N_BRANCHES * D_MODEL]
IN_WIDTH = int(sum(IN_SIZES))
IN_SPLITS = [int(s) for s in np.cumsum(IN_SIZES)[:-1]]

kernel_name = "hybrid_swa_sink_gmlp_gated_block"


def rmsnorm(x, g):
    xf = x.astype(jnp.float32)
    y = xf * lax.rsqrt(jnp.mean(xf * xf, axis=-1, keepdims=True) + EPS)
    return (y * g.astype(jnp.float32)).astype(x.dtype)


def layernorm(x, g, b):
    xf = x.astype(jnp.float32)
    mu = jnp.mean(xf, axis=-1, keepdims=True)
    xc = xf - mu
    y = xc * lax.rsqrt(jnp.mean(xc * xc, axis=-1, keepdims=True) + EPS)
    return (y * g.astype(jnp.float32) + b.astype(jnp.float32)).astype(x.dtype)


def partial_rope(x, pos):
    half = ROPE_DIM // 2
    inv_freq = ROPE_THETA ** (-(jnp.arange(half, dtype=jnp.float32) * 2.0) / ROPE_DIM)
    ang = pos.astype(jnp.float32)[:, None] * inv_freq[None, :]
    cos = jnp.cos(ang)[None, :, None, :]
    sin = jnp.sin(ang)[None, :, None, :]
    xf = x.astype(jnp.float32)
    x1, x2, rest = xf[..., :half], xf[..., half:ROPE_DIM], xf[..., ROPE_DIM:]
    out = jnp.concatenate([x1 * cos - x2 * sin, x2 * cos + x1 * sin, rest], axis=-1)
    return out.astype(x.dtype)


def window_sink_attention(q, k, v, sinks):
    B, S = q.shape[0], q.shape[1]
    nc = S // CHUNK
    qc = q.reshape(B, nc, CHUNK, ATT_KV_HEADS, ATT_GROUP, HEAD_DIM)
    pad = ((0, 0), (WINDOW_CHUNKS, 0), (0, 0), (0, 0), (0, 0))
    kc = jnp.pad(k.reshape(B, nc, CHUNK, ATT_KV_HEADS, HEAD_DIM), pad)
    vc = jnp.pad(v.reshape(B, nc, CHUNK, ATT_KV_HEADS, HEAD_DIM), pad)
    kb = jnp.concatenate([kc[:, j:j + nc] for j in range(WINDOW_CHUNKS + 1)], axis=2)
    vb = jnp.concatenate([vc[:, j:j + nc] for j in range(WINDOW_CHUNKS + 1)], axis=2)
    s = jnp.einsum('bnqhgd,bnkhd->bhgnqk', qc, kb,
                   preferred_element_type=jnp.float32) * (HEAD_DIM ** -0.5)
    key_chunk = jnp.arange(nc)[:, None] - WINDOW_CHUNKS + jnp.arange(WINDOW_CHUNKS + 1)[None, :]
    valid = jnp.repeat(key_chunk >= 0, CHUNK, axis=1)
    s = jnp.where(valid[None, None, None, :, None, :], s, -1e30)
    sink = sinks.astype(jnp.float32).reshape(ATT_KV_HEADS, ATT_GROUP)[None, :, :, None, None, None]
    m = jnp.maximum(jnp.max(s, axis=-1, keepdims=True), sink)
    p = jnp.exp(s - m)
    denom = jnp.sum(p, axis=-1, keepdims=True) + jnp.exp(sink - m)
    o = jnp.einsum('bhgnqk,bnkhd->bnqhgd', p / denom, vb.astype(jnp.float32))
    return o.reshape(B, S, ATT_WIDTH).astype(q.dtype)


def chunked_spatial_gating(u, v, w_s, b_s, ln_g, ln_b):
    B, S = u.shape[0], u.shape[1]
    nb = S // SG_BLOCK
    v = layernorm(v, ln_g, ln_b)
    vb = v.reshape(B, nb, SG_BLOCK, SG_GROUPS, SG_GROUP_WIDTH)
    cidx = np.arange(SG_BLOCK) // CHUNK
    mask = jnp.asarray(cidx[None, :] <= cidx[:, None])
    w = jnp.where(mask[None], w_s, jnp.zeros_like(w_s))
    mixed = jnp.einsum('gij,bnjgc->bnigc', w, vb) + jnp.transpose(b_s)[None, None, :, :, None]
    return u * mixed.reshape(B, S, SG_WIDTH)


def setup_inputs(seed: int = 0) -> dict:
    key = jax.random.key(seed)
    ks = jax.random.split(key, 16)
    f32 = jnp.float32
    d = D_MODEL
    x = jax.random.normal(ks[0], (BATCH, SEQ, d), f32)
    norm_g = 1.0 + 0.05 * jax.random.normal(ks[1], (DEPTH, d), f32)
    w_in = jax.random.normal(ks[2], (DEPTH, d, IN_WIDTH), f32) * d ** -0.5
    b_merge = 0.01 * jax.random.normal(ks[3], (DEPTH, N_BRANCHES * d), f32)
    att_sinks = 0.5 * jax.random.normal(ks[4], (DEPTH, ATT_HEADS), f32)
    sg_w = jax.random.normal(ks[5], (DEPTH, SG_GROUPS, SG_BLOCK, SG_BLOCK), f32) * (0.5 * SG_BLOCK ** -0.5)
    sg_b = 1.0 + 0.1 * jax.random.normal(ks[6], (DEPTH, SG_GROUPS, SG_BLOCK), f32)
    sg_ln_g = 1.0 + 0.05 * jax.random.normal(ks[7], (DEPTH, SG_WIDTH), f32)
    sg_ln_b = 0.02 * jax.random.normal(ks[8], (DEPTH, SG_WIDTH), f32)
    w_att_out = jax.random.normal(ks[9], (DEPTH, ATT_WIDTH, d), f32) * ATT_WIDTH ** -0.5
    w_sg_out = jax.random.normal(ks[10], (DEPTH, SG_WIDTH, d), f32) * SG_WIDTH ** -0.5
    w_o = jax.random.normal(ks[11], (DEPTH, d, d), f32) * d ** -0.5
    final_g = 1.0 + 0.05 * jax.random.normal(ks[12], (d,), f32)
    return {"x": x, "norm_g": norm_g, "w_in": w_in, "b_merge": b_merge,
            "att_sinks": att_sinks, "sg_w": sg_w, "sg_b": sg_b, "sg_ln_g": sg_ln_g,
            "sg_ln_b": sg_ln_b, "w_att_out": w_att_out, "w_sg_out": w_sg_out,
            "w_o": w_o, "final_g": final_g}


def reference(x, norm_g, w_in, b_merge, att_sinks, sg_w, sg_b, sg_ln_g, sg_ln_b,
              w_att_out, w_sg_out, w_o, final_g):
    B, S = x.shape[0], x.shape[1]
    pos = jnp.arange(S)
    for l in range(DEPTH):
        h = rmsnorm(x, norm_g[l])
        proj = h @ w_in[l]
        q, k, v, za, u, vs, zs, gm = jnp.split(proj, IN_SPLITS, axis=-1)
        q = partial_rope(q.reshape(B, S, ATT_HEADS, HEAD_DIM), pos)
        k = partial_rope(k.reshape(B, S, ATT_KV_HEADS, HEAD_DIM), pos)
        v = v.reshape(B, S, ATT_KV_HEADS, HEAD_DIM)
        a = window_sink_attention(q, k, v, att_sinks[l])
        a = (a * jax.nn.silu(za)) @ w_att_out[l]
        sgo = chunked_spatial_gating(jax.nn.gelu(u, approximate=False),
                                     jax.nn.gelu(vs, approximate=False),
                                     sg_w[l], sg_b[l], sg_ln_g[l], sg_ln_b[l])
        sgo = (sgo * jax.nn.silu(zs)) @ w_sg_out[l]
        ga, gs = jnp.split(jax.nn.sigmoid(gm + b_merge[l]), N_BRANCHES, axis=-1)
        y = ga * a + gs * sgo
        x = x + y @ w_o[l]
    return rmsnorm(x, final_g)
```

```python
import functools

import numpy as np
import jax
import jax.numpy as jnp
from jax.experimental import pallas as pl
from jax.experimental.pallas import tpu as pltpu

D_MODEL = 1024
CHUNK = 64
EPS = 1e-6
ATT_HEADS = 16
ATT_KV_HEADS = 2
HEAD_DIM = 64
ATT_GROUP = ATT_HEADS // ATT_KV_HEADS
WINDOW_CHUNKS = 2
HALO = WINDOW_CHUNKS * CHUNK
KEY_BLOCK = HALO + CHUNK
ROPE_DIM = HEAD_DIM // 4
ROPE_HALF = ROPE_DIM // 2
ROPE_THETA = 500000.0
SG_BLOCK = 128
SG_GROUPS = 8
LANES = 128
PAIRS_PER_KV = ATT_GROUP // 2
MASK_VALUE = -1e30

OFF_Q, OFF_K, OFF_V, OFF_ZA = 0, 1024, 1152, 1280
OFF_U, OFF_VS, OFF_ZS, OFF_GM = 2304, 3328, 4352, 5376
IN_WIDTH = 7424

SEQ_TILE = 256
VMEM_LIMIT_BYTES = 56 * 1024 * 1024


def _dot(a, b):
    return jnp.dot(a, b, preferred_element_type=jnp.float32)


def _dot_nt(a, b):
    return jax.lax.dot_general(a, b, (((1,), (1,)), ((), ())),
                               preferred_element_type=jnp.float32)


def _silu(x):
    return x * jax.nn.sigmoid(x)


def _gelu_exact(x):
    return 0.5 * x * (1.0 + jax.lax.erf(x * np.float32(1.0 / np.sqrt(2.0))))


def _rope(x, cos_t, sin_lo, sin_hi):
    nxt = pltpu.roll(x, LANES - ROPE_HALF, 1)
    prv = pltpu.roll(x, ROPE_HALF, 1)
    return x * cos_t + nxt * sin_lo + prv * sin_hi


def _block_kernel(x_ref, ng_ref, win_ref, bm_ref, sink_ref, sgw_ref, sgb_ref,
                  lng_ref, lnb_ref, wao_ref, wso_ref, wo_ref, fg_ref,
                  cos_ref, sinlo_ref, sinhi_ref,
                  out_ref,
                  h_scr, q_scr, za_scr, kv_scr, ag_scr, sg_scr, vn_scr):
    t = pl.program_id(1)
    T = SEQ_TILE
    bf16 = jnp.bfloat16

    x = x_ref[0]
    ms = jnp.mean(x * x, axis=-1, keepdims=True)
    h = (x * jax.lax.rsqrt(ms + EPS) * ng_ref[...]).astype(bf16)
    h_scr[...] = h

    cos_t = cos_ref[...]
    sin_lo = sinlo_ref[...]
    sin_hi = sinhi_ref[...]
    lane = jax.lax.broadcasted_iota(jnp.int32, (T, LANES), 1)
    low_half = lane < HEAD_DIM

    @pl.when(t == 0)
    def _():
        kv_scr[:, 0:HALO, :] = jnp.zeros((8, HALO, LANES), bf16)

    @pl.when(t != 0)
    def _():
        kv_scr[:, 0:HALO, :] = kv_scr[:, T:T + HALO, :]

    k2 = _rope(_dot(h_scr[...], win_ref[:, OFF_K:OFF_K + LANES]), cos_t, sin_lo, sin_hi)
    v2 = _dot(h_scr[...], win_ref[:, OFF_V:OFF_V + LANES])
    zeros = jnp.zeros_like(k2)
    for i, src in enumerate((k2, v2)):
        h0_lo = jnp.where(low_half, src, zeros)
        h1_hi = jnp.where(low_half, zeros, src)
        h0_hi = pltpu.roll(h0_lo, HEAD_DIM, 1)
        h1_lo = pltpu.roll(h1_hi, HEAD_DIM, 1)
        kv_scr[4 * i + 0, HALO:HALO + T, :] = h0_lo.astype(bf16)
        kv_scr[4 * i + 1, HALO:HALO + T, :] = h0_hi.astype(bf16)
        kv_scr[4 * i + 2, HALO:HALO + T, :] = h1_lo.astype(bf16)
        kv_scr[4 * i + 3, HALO:HALO + T, :] = h1_hi.astype(bf16)

    q = _dot(h_scr[...], win_ref[:, OFF_Q:OFF_Q + 1024])
    scale = np.float32(HEAD_DIM ** -0.5)
    for j in range(1024 // LANES):
        qj = _rope(q[:, j * LANES:(j + 1) * LANES], cos_t, sin_lo, sin_hi) * scale
        q_scr[:, j * LANES:(j + 1) * LANES] = qj.astype(bf16)
    za_scr[...] = _silu(_dot(h_scr[...], win_ref[:, OFF_ZA:OFF_ZA + 1024]))

    key_off = jax.lax.broadcasted_iota(jnp.int32, (1, KEY_BLOCK), 1)
    lane_q = jax.lax.broadcasted_iota(jnp.int32, (PAIRS_PER_KV * CHUNK, LANES), 1)
    for c in range(T // CHUNK):
        r0 = c * CHUNK
        first_key = t * T + r0 - HALO
        valid = (key_off + first_key) >= 0
        for hk in range(ATT_KV_HEADS):
            qs = jnp.concatenate(
                [q_scr[r0:r0 + CHUNK, (hk * PAIRS_PER_KV + j) * LANES:(hk * PAIRS_PER_KV + j + 1) * LANES]
                 for j in range(PAIRS_PER_KV)], axis=0)
            acc = None
            recips = []
            for par in range(2):
                kmat = kv_scr[2 * hk + par, r0:r0 + KEY_BLOCK, :]
                vmat = kv_scr[4 + 2 * hk + par, r0:r0 + KEY_BLOCK, :]
                s = _dot_nt(qs, kmat)
                s = jnp.where(valid, s, MASK_VALUE)
                sink = sink_ref[2 * hk + par]
                m = jnp.maximum(jnp.max(s, axis=-1, keepdims=True), sink)
                p = jnp.exp(s - m)
                denom = jnp.sum(p, axis=-1, keepdims=True) + jnp.exp(sink - m)
                recips.append(1.0 / denom)
                pv = _dot(p.astype(bf16), vmat)
                acc = pv if acc is None else acc + pv
            o = acc * jnp.where(lane_q < HEAD_DIM, recips[0], recips[1])
            for j in range(PAIRS_PER_KV):
                c0 = (hk * PAIRS_PER_KV + j) * LANES
                gated = o[j * CHUNK:(j + 1) * CHUNK] * za_scr[r0:r0 + CHUNK, c0:c0 + LANES]
                ag_scr[r0:r0 + CHUNK, c0:c0 + LANES] = gated.astype(bf16)

    ga = jax.nn.sigmoid(_dot(h_scr[...], win_ref[:, OFF_GM:OFF_GM + 1024]) + bm_ref[:, 0:1024])
    y = ga * _dot(ag_scr[...], wao_ref[...])

    vs = _gelu_exact(_dot(h_scr[...], win_ref[:, OFF_VS:OFF_VS + 1024]))
    mu = jnp.mean(vs, axis=-1, keepdims=True)
    vc = vs - mu
    var = jnp.mean(vc * vc, axis=-1, keepdims=True)
    vn_scr[...] = (vc * jax.lax.rsqrt(var + EPS) * lng_ref[...] + lnb_ref[...]).astype(bf16)

    u = _gelu_exact(_dot(h_scr[...], win_ref[:, OFF_U:OFF_U + 1024]))
    zs = _silu(_dot(h_scr[...], win_ref[:, OFF_ZS:OFF_ZS + 1024]))
    uz = u * zs
    pos_i = jax.lax.broadcasted_iota(jnp.int32, (SG_BLOCK, SG_BLOCK), 0) // CHUNK
    pos_j = jax.lax.broadcasted_iota(jnp.int32, (SG_BLOCK, SG_BLOCK), 1) // CHUNK
    causal = pos_j <= pos_i
    nblk = T // SG_BLOCK
    for g in range(SG_GROUPS):
        wg = jnp.where(causal, sgw_ref[g], 0.0).astype(bf16)
        c0 = g * LANES
        rhs = jnp.concatenate(
            [vn_scr[b * SG_BLOCK:(b + 1) * SG_BLOCK, c0:c0 + LANES] for b in range(nblk)], axis=1)
        mixed = _dot(wg, rhs)
        for b in range(nblk):
            blk = mixed[:, b * LANES:(b + 1) * LANES] + sgb_ref[g]
            gated = uz[b * SG_BLOCK:(b + 1) * SG_BLOCK, c0:c0 + LANES] * blk
            sg_scr[b * SG_BLOCK:(b + 1) * SG_BLOCK, c0:c0 + LANES] = gated.astype(bf16)

    gs = jax.nn.sigmoid(_dot(h_scr[...], win_ref[:, OFF_GM + 1024:OFF_GM + 2048]) + bm_ref[:, 1024:2048])
    y = y + gs * _dot(sg_scr[...], wso_ref[...])
    xo = x_ref[0] + _dot(y.astype(bf16), wo_ref[...])
    ms2 = jnp.mean(xo * xo, axis=-1, keepdims=True)
    out_ref[0] = xo * jax.lax.rsqrt(ms2 + EPS) * fg_ref[...]


def _rope_tables(seq):
    inv_freq = ROPE_THETA ** (-(jnp.arange(ROPE_HALF, dtype=jnp.float32) * 2.0) / ROPE_DIM)
    ang = jnp.arange(seq, dtype=jnp.float32)[:, None] * inv_freq[None, :]
    cos, sin = jnp.cos(ang), jnp.sin(ang)
    ones = jnp.ones((seq, HEAD_DIM - ROPE_DIM), jnp.float32)
    zeros8 = jnp.zeros((seq, ROPE_HALF), jnp.float32)
    zeros48 = jnp.zeros((seq, HEAD_DIM - ROPE_DIM), jnp.float32)
    cos_h = jnp.concatenate([cos, cos, ones], axis=1)
    lo_h = jnp.concatenate([-sin, zeros8, zeros48], axis=1)
    hi_h = jnp.concatenate([zeros8, sin, zeros48], axis=1)
    two = lambda a: jnp.concatenate([a, a], axis=1)
    return two(cos_h), two(lo_h), two(hi_h)


@jax.jit
def kernel(x, norm_g, w_in, b_merge, att_sinks, sg_w, sg_b, sg_ln_g, sg_ln_b,
           w_att_out, w_sg_out, w_o, final_g):
    B, S, D = x.shape
    T = SEQ_TILE
    assert D == D_MODEL and S % T == 0 and w_in.shape == (1, D, IN_WIDTH)
    bf16 = jnp.bfloat16
    f32 = jnp.float32

    cos_t, sin_lo, sin_hi = _rope_tables(S)
    sinks = att_sinks[0].astype(f32).reshape(ATT_KV_HEADS, PAIRS_PER_KV, 2)
    sink_rows = jnp.repeat(jnp.transpose(sinks, (0, 2, 1)).reshape(4, PAIRS_PER_KV), CHUNK, axis=1)
    sink_rows = sink_rows[:, :, None]
    sgb = jnp.broadcast_to(sg_b[0].astype(f32)[:, :, None], (SG_GROUPS, SG_BLOCK, LANES))

    const2 = lambda shape: pl.BlockSpec(shape, lambda b, t: (0, 0), pipeline_mode=pl.Buffered(1))
    const3 = lambda shape: pl.BlockSpec(shape, lambda b, t: (0, 0, 0), pipeline_mode=pl.Buffered(1))
    rope_spec = pl.BlockSpec((T, LANES), lambda b, t: (t, 0))

    grid_spec = pltpu.PrefetchScalarGridSpec(
        num_scalar_prefetch=0,
        grid=(B, S // T),
        in_specs=[
            pl.BlockSpec((1, T, D), lambda b, t: (b, t, 0)),
            const2((1, D)),
            const2((D, IN_WIDTH)),
            const2((1, 2 * D)),
            const3((4, PAIRS_PER_KV * CHUNK, 1)),
            const3((SG_GROUPS, SG_BLOCK, SG_BLOCK)),
            const3((SG_GROUPS, SG_BLOCK, LANES)),
            const2((1, D)),
            const2((1, D)),
            const2((D, D)),
            const2((D, D)),
            const2((D, D)),
            const2((1, D)),
            rope_spec, rope_spec, rope_spec,
        ],
        out_specs=pl.BlockSpec((1, T, D), lambda b, t: (b, t, 0)),
        scratch_shapes=[
            pltpu.VMEM((T, D), bf16),
            pltpu.VMEM((T, D), bf16),
            pltpu.VMEM((T, D), f32),
            pltpu.VMEM((8, HALO + T, LANES), bf16),
            pltpu.VMEM((T, D), bf16),
            pltpu.VMEM((T, D), bf16),
            pltpu.VMEM((T, D), bf16),
        ],
    )
    return pl.pallas_call(
        _block_kernel,
        grid_spec=grid_spec,
        out_shape=jax.ShapeDtypeStruct((B, S, D), x.dtype),
        compiler_params=pltpu.CompilerParams(
            dimension_semantics=("arbitrary", "arbitrary"),
            vmem_limit_bytes=VMEM_LIMIT_BYTES,
        ),
        name="hybrid_block",
    )(
        x, norm_g.astype(f32), w_in[0].astype(bf16), b_merge.astype(f32), sink_rows,
        sg_w[0].astype(f32), sgb, sg_ln_g.astype(f32), sg_ln_b.astype(f32),
        w_att_out[0].astype(bf16), w_sg_out[0].astype(bf16), w_o[0].astype(bf16),
        final_g.reshape(1, D).astype(f32), cos_t, sin_lo, sin_hi,
    )
```

```python
import numpy as np
import jax
import jax.numpy as jnp
from jax.experimental import pallas as pl
from jax.experimental.pallas import tpu as pltpu

D_MODEL = 1024
CHUNK = 64
EPS = 1e-6
ATT_HEADS = 16
ATT_KV_HEADS = 2
HEAD_DIM = 64
ATT_GROUP = ATT_HEADS // ATT_KV_HEADS
WINDOW_CHUNKS = 2
HALO = WINDOW_CHUNKS * CHUNK
KEY_BLOCK = HALO + CHUNK
ROPE_DIM = HEAD_DIM // 4
ROPE_HALF = ROPE_DIM // 2
ROPE_THETA = 500000.0
SG_BLOCK = 128
SG_GROUPS = 8
LANES = 128
PAIRS_PER_KV = ATT_GROUP // 2
STACK_ROWS = PAIRS_PER_KV * CHUNK
MASK_VALUE = -1e30

OFF_Q, OFF_K, OFF_V, OFF_ZA = 0, 1024, 1152, 1280
OFF_U, OFF_VS, OFF_ZS, OFF_GM = 2304, 3328, 4352, 5376
IN_WIDTH = 7424
ATT_IN_WIDTH = OFF_U
N_SIDE = 4
SIDE_U, SIDE_VS, SIDE_ZS, SIDE_GA = range(N_SIDE)

SEQ_TILE = 256
VMEM_LIMIT_BYTES = 56 * 1024 * 1024

assert SEQ_TILE // CHUNK == N_SIDE


def _dot(a, b):
    return jnp.dot(a, b, preferred_element_type=jnp.float32)


def _dot_nt(a, b):
    return jax.lax.dot_general(a, b, (((1,), (1,)), ((), ())),
                               preferred_element_type=jnp.float32)


def _dot_tn(a, b):
    return jax.lax.dot_general(a, b, (((0,), (0,)), ((), ())),
                               preferred_element_type=jnp.float32)


def _silu(x):
    return x * jax.nn.sigmoid(x)


def _gelu_exact(x):
    return 0.5 * x * (1.0 + jax.lax.erf(x * np.float32(1.0 / np.sqrt(2.0))))


def _block_kernel(x_ref, ng_ref, wa_ref, wside_ref, wgs_ref, bm_ref, sink_ref, sgw_ref, sgb_ref,
                  lng_ref, lnb_ref, wao_ref, wso_ref, wo_ref, fg_ref,
                  rrow_ref, rtile_ref,
                  out_ref,
                  h_scr, q_scr, za_scr, kv_scr, ag_scr, sg_scr, vn_scr, side_scr):
    t = pl.program_id(1)
    T = SEQ_TILE
    bf16 = jnp.bfloat16

    x = x_ref[0]
    ms = jnp.mean(x * x, axis=-1, keepdims=True)
    h_scr[...] = (x * jax.lax.rsqrt(ms + EPS) * ng_ref[...]).astype(bf16)

    cb, sb, sb_sgn = rrow_ref[0], rrow_ref[1], rrow_ref[2]
    ca, sa, sa_sgn = rtile_ref[t, 0:1, :], rtile_ref[t, 1:2, :], rtile_ref[t, 2:3, :]
    cos_t = ca * cb - sa * sb
    sin_t = sa_sgn * cb + ca * sb_sgn
    lane = jax.lax.broadcasted_iota(jnp.int32, (T, LANES), 1)
    first_half = (lane % HEAD_DIM) < ROPE_HALF
    low_half = lane < HEAD_DIM

    def rope(v):
        nxt = pltpu.roll(v, LANES - ROPE_HALF, 1)
        prv = pltpu.roll(v, ROPE_HALF, 1)
        return v * cos_t + jnp.where(first_half, nxt, prv) * sin_t

    @pl.when(t == 0)
    def _():
        kv_scr[:, 0:HALO, :] = jnp.zeros((8, HALO, LANES), bf16)

    @pl.when(t != 0)
    def _():
        kv_scr[:, 0:HALO, :] = kv_scr[:, T:T + HALO, :]

    h = h_scr[...]
    k2 = rope(_dot(h, wa_ref[:, OFF_K:OFF_K + LANES]))
    v2 = _dot(h, wa_ref[:, OFF_V:OFF_V + LANES])
    zeros = jnp.zeros_like(k2)
    for i, src in enumerate((k2, v2)):
        h0_lo = jnp.where(low_half, src, zeros)
        h1_hi = jnp.where(low_half, zeros, src)
        h0_hi = pltpu.roll(h0_lo, HEAD_DIM, 1)
        h1_lo = pltpu.roll(h1_hi, HEAD_DIM, 1)
        kv_scr[4 * i + 0, HALO:HALO + T, :] = h0_lo.astype(bf16)
        kv_scr[4 * i + 1, HALO:HALO + T, :] = h0_hi.astype(bf16)
        kv_scr[4 * i + 2, HALO:HALO + T, :] = h1_lo.astype(bf16)
        kv_scr[4 * i + 3, HALO:HALO + T, :] = h1_hi.astype(bf16)

    q = _dot(h, wa_ref[:, OFF_Q:OFF_Q + 1024])
    scale = np.float32(HEAD_DIM ** -0.5)
    for j in range(1024 // LANES):
        qj = rope(q[:, j * LANES:(j + 1) * LANES]) * scale
        q_scr[:, j * LANES:(j + 1) * LANES] = qj.astype(bf16)
    za_scr[...] = _silu(_dot(h, wa_ref[:, OFF_ZA:OFF_ZA + 1024]))

    key_off = jax.lax.broadcasted_iota(jnp.int32, (KEY_BLOCK, 1), 0)

    def chunk_body(c, carry):
        r0 = pl.multiple_of(c * CHUNK, CHUNK)
        rows = pl.ds(r0, CHUNK)
        keys = pl.ds(r0, KEY_BLOCK)
        valid = (key_off + (t * T + r0 - HALO)) >= 0
        units = [(hk, par) for hk in range(ATT_KV_HEADS) for par in range(2)]
        scores = []
        for hk, par in units:
            qs = jnp.concatenate(
                [q_scr[rows, (hk * PAIRS_PER_KV + j) * LANES:(hk * PAIRS_PER_KV + j + 1) * LANES]
                 for j in range(PAIRS_PER_KV)], axis=0)
            scores.append(_dot_nt(kv_scr[2 * hk + par, keys, :], qs))
        side_scr[c] = _dot(h_scr[...], wside_ref[c])
        probs = []
        for (hk, par), s in zip(units, scores):
            s = jnp.where(valid, s, MASK_VALUE)
            sink = sink_ref[2 * hk + par, 0:1, :]
            m = jnp.maximum(jnp.max(s, axis=0, keepdims=True), sink)
            p = jnp.exp(s - m)
            denom = jnp.sum(p, axis=0, keepdims=True) + jnp.exp(sink - m)
            probs.append((p * (1.0 / denom)).astype(bf16))
        for hk in range(ATT_KV_HEADS):
            o = (_dot_tn(probs[2 * hk], kv_scr[4 + 2 * hk, keys, :])
                 + _dot_tn(probs[2 * hk + 1], kv_scr[5 + 2 * hk, keys, :]))
            for j in range(PAIRS_PER_KV):
                c0 = (hk * PAIRS_PER_KV + j) * LANES
                gated = o[j * CHUNK:(j + 1) * CHUNK] * za_scr[rows, c0:c0 + LANES]
                ag_scr[rows, c0:c0 + LANES] = gated.astype(bf16)
        return carry

    jax.lax.fori_loop(0, T // CHUNK, chunk_body, 0)

    ga = jax.nn.sigmoid(side_scr[SIDE_GA] + bm_ref[:, 0:1024])
    y = ga * _dot(ag_scr[...], wao_ref[...])

    vs = _gelu_exact(side_scr[SIDE_VS])
    mu = jnp.mean(vs, axis=-1, keepdims=True)
    vc = vs - mu
    var = jnp.mean(vc * vc, axis=-1, keepdims=True)
    vn_scr[...] = (vc * jax.lax.rsqrt(var + EPS) * lng_ref[...] + lnb_ref[...]).astype(bf16)

    uz = _gelu_exact(side_scr[SIDE_U]) * _silu(side_scr[SIDE_ZS])
    pos_i = jax.lax.broadcasted_iota(jnp.int32, (SG_BLOCK, SG_BLOCK), 0) // CHUNK
    pos_j = jax.lax.broadcasted_iota(jnp.int32, (SG_BLOCK, SG_BLOCK), 1) // CHUNK
    causal = pos_j <= pos_i
    nblk = T // SG_BLOCK
    for g in range(SG_GROUPS):
        wg = jnp.where(causal, sgw_ref[g], 0.0).astype(bf16)
        c0 = g * LANES
        rhs = jnp.concatenate(
            [vn_scr[b * SG_BLOCK:(b + 1) * SG_BLOCK, c0:c0 + LANES] for b in range(nblk)], axis=1)
        mixed = _dot(wg, rhs)
        for b in range(nblk):
            blk = mixed[:, b * LANES:(b + 1) * LANES] + sgb_ref[g]
            gated = uz[b * SG_BLOCK:(b + 1) * SG_BLOCK, c0:c0 + LANES] * blk
            sg_scr[b * SG_BLOCK:(b + 1) * SG_BLOCK, c0:c0 + LANES] = gated.astype(bf16)

    gs = jax.nn.sigmoid(_dot(h_scr[...], wgs_ref[...]) + bm_ref[:, 1024:2048])
    y = y + gs * _dot(sg_scr[...], wso_ref[...])
    xo = x_ref[0] + _dot(y.astype(bf16), wo_ref[...])
    ms2 = jnp.mean(xo * xo, axis=-1, keepdims=True)
    out_ref[0] = xo * jax.lax.rsqrt(ms2 + EPS) * fg_ref[...]


def _rope_tables(seq, tile):
    lane = np.arange(LANES) % HEAD_DIM
    in_rope = lane < ROPE_DIM
    inv_freq = ROPE_THETA ** (-(jnp.arange(ROPE_HALF, dtype=jnp.float32) * 2.0) / ROPE_DIM)
    freq = jnp.where(in_rope, inv_freq[lane % ROPE_HALF], 0.0)
    sign = jnp.where(lane < ROPE_HALF, -1.0, 1.0).astype(jnp.float32)

    def tables(pos, pad_rows):
        ang = pos.astype(jnp.float32)[:, None] * freq[None, :]
        c, s = jnp.cos(ang), jnp.sin(ang)
        tab = jnp.stack([c, s, s * sign], axis=0)
        return tab if pad_rows is None else jnp.pad(jnp.transpose(tab, (1, 0, 2)),
                                                    ((0, 0), (0, pad_rows - 3), (0, 0)))

    row_tab = tables(jnp.arange(tile), None)
    tile_tab = tables(jnp.arange(seq // tile) * tile, 8)
    return row_tab, tile_tab


@jax.jit
def kernel(x, norm_g, w_in, b_merge, att_sinks, sg_w, sg_b, sg_ln_g, sg_ln_b,
           w_att_out, w_sg_out, w_o, final_g):
    B, S, D = x.shape
    T = SEQ_TILE
    assert D == D_MODEL and S % T == 0 and w_in.shape == (1, D, IN_WIDTH)
    bf16 = jnp.bfloat16
    f32 = jnp.float32

    row_tab, tile_tab = _rope_tables(S, T)
    sinks = att_sinks[0].astype(f32).reshape(ATT_KV_HEADS, PAIRS_PER_KV, 2)
    sink_rows = jnp.repeat(jnp.transpose(sinks, (0, 2, 1)).reshape(4, PAIRS_PER_KV), CHUNK, axis=1)
    sink_rows = jnp.broadcast_to(sink_rows[:, None, :], (4, 8, STACK_ROWS))
    sgb = jnp.broadcast_to(sg_b[0].astype(f32)[:, :, None], (SG_GROUPS, SG_BLOCK, LANES))

    w_in_bf = w_in[0].astype(bf16)
    w_att_in = w_in_bf[:, :ATT_IN_WIDTH]
    side_end = ATT_IN_WIDTH + N_SIDE * D
    w_side = jnp.transpose(w_in_bf[:, ATT_IN_WIDTH:side_end].reshape(D, N_SIDE, D), (1, 0, 2))
    w_gs = w_in_bf[:, side_end:]

    def const(shape):
        zeros = (0,) * len(shape)
        return pl.BlockSpec(shape, lambda b, t: zeros, pipeline_mode=pl.Buffered(1))

    grid_spec = pltpu.PrefetchScalarGridSpec(
        num_scalar_prefetch=0,
        grid=(B, S // T),
        in_specs=[
            pl.BlockSpec((1, T, D), lambda b, t: (b, t, 0)),
            const((1, D)),
            const((D, ATT_IN_WIDTH)),
            const((N_SIDE, D, D)),
            const((D, D)),
            const((1, 2 * D)),
            const((4, 8, STACK_ROWS)),
            const((SG_GROUPS, SG_BLOCK, SG_BLOCK)),
            const((SG_GROUPS, SG_BLOCK, LANES)),
            const((1, D)),
            const((1, D)),
            const((D, D)),
            const((D, D)),
            const((D, D)),
            const((1, D)),
            const((3, T, LANES)),
            const((S // T, 8, LANES)),
        ],
        out_specs=pl.BlockSpec((1, T, D), lambda b, t: (b, t, 0)),
        scratch_shapes=[
            pltpu.VMEM((T, D), bf16),
            pltpu.VMEM((T, D), bf16),
            pltpu.VMEM((T, D), f32),
            pltpu.VMEM((8, HALO + T, LANES), bf16),
            pltpu.VMEM((T, D), bf16),
            pltpu.VMEM((T, D), bf16),
            pltpu.VMEM((T, D), bf16),
            pltpu.VMEM((N_SIDE, T, D), f32),
        ],
    )
    return pl.pallas_call(
        _block_kernel,
        grid_spec=grid_spec,
        out_shape=jax.ShapeDtypeStruct((B, S, D), x.dtype),
        compiler_params=pltpu.CompilerParams(
            dimension_semantics=("arbitrary", "arbitrary"),
            vmem_limit_bytes=VMEM_LIMIT_BYTES,
        ),
        name="hybrid_block",
    )(
        x, norm_g.astype(f32), w_att_in, w_side, w_gs, b_merge.astype(f32), sink_rows,
        sg_w[0].astype(f32), sgb, sg_ln_g.astype(f32), sg_ln_b.astype(f32),
        w_att_out[0].astype(bf16), w_sg_out[0].astype(bf16), w_o[0].astype(bf16),
        final_g.reshape(1, D).astype(f32), row_tab, tile_tab,
    )
```

```python
import numpy as np
import jax
import jax.numpy as jnp
from jax.experimental import pallas as pl
from jax.experimental.pallas import tpu as pltpu

D_MODEL = 1024
CHUNK = 64
EPS = 1e-6
ATT_HEADS = 16
ATT_KV_HEADS = 2
HEAD_DIM = 64
ATT_GROUP = ATT_HEADS // ATT_KV_HEADS
WINDOW_CHUNKS = 2
HALO = WINDOW_CHUNKS * CHUNK
KEY_BLOCK = HALO + CHUNK
ROPE_DIM = HEAD_DIM // 4
ROPE_HALF = ROPE_DIM // 2
ROPE_THETA = 500000.0
SG_BLOCK = 128
SG_GROUPS = 8
LANES = 128
PAIRS_PER_KV = ATT_GROUP // 2
STACK_ROWS = PAIRS_PER_KV * CHUNK
MASK_VALUE = -1e30

OFF_Q, OFF_K, OFF_V, OFF_ZA = 0, 1024, 1152, 1280
OFF_U, OFF_VS, OFF_ZS, OFF_GM = 2304, 3328, 4352, 5376
IN_WIDTH = 7424
ATT_IN_WIDTH = OFF_U
N_SIDE = 4
SIDE_U, SIDE_VS, SIDE_ZS, SIDE_GA = range(N_SIDE)

SEQ_TILE = 512
ROW_BLOCK = 256
N_ROW_BLOCKS = SEQ_TILE // ROW_BLOCK
N_CHUNKS = SEQ_TILE // CHUNK
VMEM_LIMIT_BYTES = 58 * 1024 * 1024

assert N_CHUNKS == N_SIDE * N_ROW_BLOCKS


def _dot(a, b):
    return jnp.dot(a, b, preferred_element_type=jnp.float32)


def _dot_nt(a, b):
    return jax.lax.dot_general(a, b, (((1,), (1,)), ((), ())),
                               preferred_element_type=jnp.float32)


def _dot_tn(a, b):
    return jax.lax.dot_general(a, b, (((0,), (0,)), ((), ())),
                               preferred_element_type=jnp.float32)


def _silu(x):
    return x * jax.nn.sigmoid(x)


def _gelu_exact(x):
    return 0.5 * x * (1.0 + jax.lax.erf(x * np.float32(1.0 / np.sqrt(2.0))))


def _block_kernel(x_ref, ng_ref, wa_ref, wside_ref, wgs_ref, bm_ref, sink_ref, sgw_ref, sgb_ref,
                  lng_ref, lnb_ref, wao_ref, wso_ref, wo_ref, fg_ref,
                  rrow_ref, rtile_ref,
                  out_ref,
                  h_scr, q_scr, za_scr, kv_scr, ag_scr, sg_scr, vn_scr, uz_scr, ga_scr):
    t = pl.program_id(1)
    T = SEQ_TILE
    RB = ROW_BLOCK
    bf16 = jnp.bfloat16

    @pl.when(t == 0)
    def _():
        kv_scr[:, 0:HALO, :] = jnp.zeros((8, HALO, LANES), bf16)

    @pl.when(t != 0)
    def _():
        kv_scr[:, 0:HALO, :] = kv_scr[:, T:T + HALO, :]

    lane = jax.lax.broadcasted_iota(jnp.int32, (RB, LANES), 1)
    first_half = (lane % HEAD_DIM) < ROPE_HALF
    low_half = lane < HEAD_DIM
    scale = np.float32(HEAD_DIM ** -0.5)

    for rb in range(N_ROW_BLOCKS):
        rows = slice(rb * RB, (rb + 1) * RB)
        x = x_ref[0, rows, :]
        ms = jnp.mean(x * x, axis=-1, keepdims=True)
        h = (x * jax.lax.rsqrt(ms + EPS) * ng_ref[...]).astype(bf16)
        h_scr[rows, :] = h

        cb, sb, sb_sgn = rrow_ref[0, rows, :], rrow_ref[1, rows, :], rrow_ref[2, rows, :]
        ca, sa, sa_sgn = rtile_ref[t, 0:1, :], rtile_ref[t, 1:2, :], rtile_ref[t, 2:3, :]
        cos_t = ca * cb - sa * sb
        sin_t = sa_sgn * cb + ca * sb_sgn

        def rope(v):
            nxt = pltpu.roll(v, LANES - ROPE_HALF, 1)
            prv = pltpu.roll(v, ROPE_HALF, 1)
            return v * cos_t + jnp.where(first_half, nxt, prv) * sin_t

        k2 = rope(_dot(h, wa_ref[:, OFF_K:OFF_K + LANES]))
        v2 = _dot(h, wa_ref[:, OFF_V:OFF_V + LANES])
        zeros = jnp.zeros_like(k2)
        krows = slice(HALO + rb * RB, HALO + (rb + 1) * RB)
        for i, src in enumerate((k2, v2)):
            h0_lo = jnp.where(low_half, src, zeros)
            h1_hi = jnp.where(low_half, zeros, src)
            h0_hi = pltpu.roll(h0_lo, HEAD_DIM, 1)
            h1_lo = pltpu.roll(h1_hi, HEAD_DIM, 1)
            kv_scr[4 * i + 0, krows, :] = h0_lo.astype(bf16)
            kv_scr[4 * i + 1, krows, :] = h0_hi.astype(bf16)
            kv_scr[4 * i + 2, krows, :] = h1_lo.astype(bf16)
            kv_scr[4 * i + 3, krows, :] = h1_hi.astype(bf16)

        q = _dot(h, wa_ref[:, OFF_Q:OFF_Q + 1024])
        for j in range(1024 // LANES):
            qj = rope(q[:, j * LANES:(j + 1) * LANES]) * scale
            q_scr[rows, j * LANES:(j + 1) * LANES] = qj.astype(bf16)
        za_scr[rows, :] = _silu(_dot(h, wa_ref[:, OFF_ZA:OFF_ZA + 1024]))

    key_off = jax.lax.broadcasted_iota(jnp.int32, (KEY_BLOCK, 1), 0)
    units = [(hk, par) for hk in range(ATT_KV_HEADS) for par in range(2)]
    for c in range(N_CHUNKS):
        r0 = c * CHUNK
        rows = slice(r0, r0 + CHUNK)
        keys = slice(r0, r0 + KEY_BLOCK)
        valid = (key_off + (t * T + r0 - HALO)) >= 0
        scores = []
        for hk, par in units:
            qs = jnp.concatenate(
                [q_scr[rows, (hk * PAIRS_PER_KV + j) * LANES:(hk * PAIRS_PER_KV + j + 1) * LANES]
                 for j in range(PAIRS_PER_KV)], axis=0)
            scores.append(_dot_nt(kv_scr[2 * hk + par, keys, :], qs))

        side, rb = divmod(c, N_ROW_BLOCKS)
        srows = slice(rb * RB, (rb + 1) * RB)
        raw = _dot(h_scr[srows, :], wside_ref[side])
        if side == SIDE_U:
            uz_scr[srows, :] = _gelu_exact(raw)
        elif side == SIDE_VS:
            vs = _gelu_exact(raw)
            mu = jnp.mean(vs, axis=-1, keepdims=True)
            vc = vs - mu
            var = jnp.mean(vc * vc, axis=-1, keepdims=True)
            vn_scr[srows, :] = (vc * jax.lax.rsqrt(var + EPS) * lng_ref[...] + lnb_ref[...]).astype(bf16)
        elif side == SIDE_ZS:
            uz_scr[srows, :] = uz_scr[srows, :] * _silu(raw)
        else:
            ga_scr[srows, :] = jax.nn.sigmoid(raw + bm_ref[:, 0:1024])

        probs = []
        for (hk, par), s in zip(units, scores):
            s = jnp.where(valid, s, MASK_VALUE)
            sink = sink_ref[2 * hk + par, 0:1, :]
            m = jnp.maximum(jnp.max(s, axis=0, keepdims=True), sink)
            p = jnp.exp(s - m)
            denom = jnp.sum(p, axis=0, keepdims=True) + jnp.exp(sink - m)
            probs.append((p * (1.0 / denom)).astype(bf16))
        for hk in range(ATT_KV_HEADS):
            o = (_dot_tn(probs[2 * hk], kv_scr[4 + 2 * hk, keys, :])
                 + _dot_tn(probs[2 * hk + 1], kv_scr[5 + 2 * hk, keys, :]))
            for j in range(PAIRS_PER_KV):
                c0 = (hk * PAIRS_PER_KV + j) * LANES
                gated = o[j * CHUNK:(j + 1) * CHUNK] * za_scr[rows, c0:c0 + LANES]
                ag_scr[rows, c0:c0 + LANES] = gated.astype(bf16)

    pos_i = jax.lax.broadcasted_iota(jnp.int32, (SG_BLOCK, SG_BLOCK), 0) // CHUNK
    pos_j = jax.lax.broadcasted_iota(jnp.int32, (SG_BLOCK, SG_BLOCK), 1) // CHUNK
    causal = pos_j <= pos_i
    nblk = T // SG_BLOCK
    for g in range(SG_GROUPS):
        wg = jnp.where(causal, sgw_ref[g], 0.0).astype(bf16)
        c0 = g * LANES
        rhs = jnp.concatenate(
            [vn_scr[b * SG_BLOCK:(b + 1) * SG_BLOCK, c0:c0 + LANES] for b in range(nblk)], axis=1)
        mixed = _dot(wg, rhs)
        for b in range(nblk):
            brows = slice(b * SG_BLOCK, (b + 1) * SG_BLOCK)
            blk = mixed[:, b * LANES:(b + 1) * LANES] + sgb_ref[g]
            sg_scr[brows, c0:c0 + LANES] = (uz_scr[brows, c0:c0 + LANES] * blk).astype(bf16)

    for rb in range(N_ROW_BLOCKS):
        rows = slice(rb * RB, (rb + 1) * RB)
        y = ga_scr[rows, :] * _dot(ag_scr[rows, :], wao_ref[...])
        gs = jax.nn.sigmoid(_dot(h_scr[rows, :], wgs_ref[...]) + bm_ref[:, 1024:2048])
        y = y + gs * _dot(sg_scr[rows, :], wso_ref[...])
        xo = x_ref[0, rows, :] + _dot(y.astype(bf16), wo_ref[...])
        ms2 = jnp.mean(xo * xo, axis=-1, keepdims=True)
        out_ref[0, rows, :] = xo * jax.lax.rsqrt(ms2 + EPS) * fg_ref[...]


def _rope_tables(seq, tile):
    lane = np.arange(LANES) % HEAD_DIM
    in_rope = lane < ROPE_DIM
    inv_freq = ROPE_THETA ** (-(jnp.arange(ROPE_HALF, dtype=jnp.float32) * 2.0) / ROPE_DIM)
    freq = jnp.where(in_rope, inv_freq[lane % ROPE_HALF], 0.0)
    sign = jnp.where(lane < ROPE_HALF, -1.0, 1.0).astype(jnp.float32)

    def tables(pos, pad_rows):
        ang = pos.astype(jnp.float32)[:, None] * freq[None, :]
        c, s = jnp.cos(ang), jnp.sin(ang)
        tab = jnp.stack([c, s, s * sign], axis=0)
        return tab if pad_rows is None else jnp.pad(jnp.transpose(tab, (1, 0, 2)),
                                                    ((0, 0), (0, pad_rows - 3), (0, 0)))

    row_tab = tables(jnp.arange(tile), None)
    tile_tab = tables(jnp.arange(seq // tile) * tile, 8)
    return row_tab, tile_tab


@jax.jit
def kernel(x, norm_g, w_in, b_merge, att_sinks, sg_w, sg_b, sg_ln_g, sg_ln_b,
           w_att_out, w_sg_out, w_o, final_g):
    B, S, D = x.shape
    T = SEQ_TILE
    assert D == D_MODEL and S % T == 0 and w_in.shape == (1, D, IN_WIDTH)
    bf16 = jnp.bfloat16
    f32 = jnp.float32

    row_tab, tile_tab = _rope_tables(S, T)
    sinks = att_sinks[0].astype(f32).reshape(ATT_KV_HEADS, PAIRS_PER_KV, 2)
    sink_rows = jnp.repeat(jnp.transpose(sinks, (0, 2, 1)).reshape(4, PAIRS_PER_KV), CHUNK, axis=1)
    sink_rows = jnp.broadcast_to(sink_rows[:, None, :], (4, 8, STACK_ROWS))
    sgb = jnp.broadcast_to(sg_b[0].astype(f32)[:, :, None], (SG_GROUPS, SG_BLOCK, LANES))

    w_in_bf = w_in[0].astype(bf16)
    w_att_in = w_in_bf[:, :ATT_IN_WIDTH]
    side_end = ATT_IN_WIDTH + N_SIDE * D
    w_side = jnp.transpose(w_in_bf[:, ATT_IN_WIDTH:side_end].reshape(D, N_SIDE, D), (1, 0, 2))
    w_gs = w_in_bf[:, side_end:]

    def const(shape):
        zeros = (0,) * len(shape)
        return pl.BlockSpec(shape, lambda b, t: zeros, pipeline_mode=pl.Buffered(1))

    grid_spec = pltpu.PrefetchScalarGridSpec(
        num_scalar_prefetch=0,
        grid=(B, S // T),
        in_specs=[
            pl.BlockSpec((1, T, D), lambda b, t: (b, t, 0)),
            const((1, D)),
            const((D, ATT_IN_WIDTH)),
            const((N_SIDE, D, D)),
            const((D, D)),
            const((1, 2 * D)),
            const((4, 8, STACK_ROWS)),
            const((SG_GROUPS, SG_BLOCK, SG_BLOCK)),
            const((SG_GROUPS, SG_BLOCK, LANES)),
            const((1, D)),
            const((1, D)),
            const((D, D)),
            const((D, D)),
            const((D, D)),
            const((1, D)),
            const((3, T, LANES)),
            const((S // T, 8, LANES)),
        ],
        out_specs=pl.BlockSpec((1, T, D), lambda b, t: (b, t, 0)),
        scratch_shapes=[
            pltpu.VMEM((T, D), bf16),
            pltpu.VMEM((T, D), bf16),
            pltpu.VMEM((T, D), f32),
            pltpu.VMEM((8, HALO + T, LANES), bf16),
            pltpu.VMEM((T, D), bf16),
            pltpu.VMEM((T, D), bf16),
            pltpu.VMEM((T, D), bf16),
            pltpu.VMEM((T, D), f32),
            pltpu.VMEM((T, D), f32),
        ],
    )
    return pl.pallas_call(
        _block_kernel,
        grid_spec=grid_spec,
        out_shape=jax.ShapeDtypeStruct((B, S, D), x.dtype),
        compiler_params=pltpu.CompilerParams(
            dimension_semantics=("arbitrary", "arbitrary"),
            vmem_limit_bytes=VMEM_LIMIT_BYTES,
        ),
        name="hybrid_block",
    )(
        x, norm_g.astype(f32), w_att_in, w_side, w_gs, b_merge.astype(f32), sink_rows,
        sg_w[0].astype(f32), sgb, sg_ln_g.astype(f32), sg_ln_b.astype(f32),
        w_att_out[0].astype(bf16), w_sg_out[0].astype(bf16), w_o[0].astype(bf16),
        final_g.reshape(1, D).astype(f32), row_tab, tile_tab,
    )
```

```python
import numpy as np
import jax
import jax.numpy as jnp
from jax.experimental import pallas as pl
from jax.experimental.pallas import tpu as pltpu

D_MODEL = 1024
CHUNK = 64
EPS = 1e-6
ATT_HEADS = 16
ATT_KV_HEADS = 2
HEAD_DIM = 64
ATT_GROUP = ATT_HEADS // ATT_KV_HEADS
WINDOW_CHUNKS = 2
HALO = WINDOW_CHUNKS * CHUNK
KEY_BLOCK = HALO + CHUNK
ROPE_DIM = HEAD_DIM // 4
ROPE_HALF = ROPE_DIM // 2
ROPE_THETA = 500000.0
SG_BLOCK = 128
SG_GROUPS = 8
LANES = 128
PAIRS_PER_KV = ATT_GROUP // 2
STACK_ROWS = PAIRS_PER_KV * CHUNK
MASK_VALUE = -1e30

OFF_Q, OFF_K, OFF_V, OFF_ZA = 0, 1024, 1152, 1280
OFF_U, OFF_VS, OFF_ZS, OFF_GM = 2304, 3328, 4352, 5376
IN_WIDTH = 7424
N_SIDE = 4
SIDE_U, SIDE_VS, SIDE_ZS, SIDE_GA = range(N_SIDE)
SIDE_OFF = (OFF_U, OFF_VS, OFF_ZS, OFF_GM)
OFF_GS = OFF_GM + 1024
OFF_WAO, OFF_WSO, OFF_WO = 0, 1024, 2048
PAD_WIDTH = D_MODEL + LANES
OUT_W_WIDTH = 3 * D_MODEL + LANES
LOG2E = np.float32(1.4426950408889634)

SEQ_TILE = 512
ROW_BLOCK = 256
N_ROW_BLOCKS = SEQ_TILE // ROW_BLOCK
N_CHUNKS = SEQ_TILE // CHUNK
VMEM_LIMIT_BYTES = 58 * 1024 * 1024

assert N_CHUNKS == N_SIDE * N_ROW_BLOCKS


def _dot(a, b):
    return jnp.dot(a, b, preferred_element_type=jnp.float32)


def _dot_nt(a, b):
    return jax.lax.dot_general(a, b, (((1,), (1,)), ((), ())),
                               preferred_element_type=jnp.float32)


def _dot_tn(a, b):
    return jax.lax.dot_general(a, b, (((0,), (0,)), ((), ())),
                               preferred_element_type=jnp.float32)


def _sigmoid(x):
    return 0.5 * jnp.tanh(0.5 * x) + 0.5


def _silu(x):
    hx = 0.5 * x
    return hx * (1.0 + jnp.tanh(hx))


def _gelu_exact(x):
    return 0.5 * x * (1.0 + jax.lax.erf(x * np.float32(1.0 / np.sqrt(2.0))))


def _block_kernel(x_ref, ng_ref, wa_ref, wout_ref, bm_ref, sink_ref, sgw_ref, sgb_ref,
                  lng_ref, lnb_ref, fg_ref,
                  rrow_ref, rtile_ref,
                  out_ref,
                  h_scr, q_scr, za_scr, kv_scr, ag_scr, sg_scr, vn_scr, uz_scr, ga_scr):
    t = pl.program_id(1)
    T = SEQ_TILE
    RB = ROW_BLOCK
    bf16 = jnp.bfloat16

    @pl.when(t == 0)
    def _():
        kv_scr[:, 0:HALO, :] = jnp.zeros((8, HALO, LANES), bf16)

    @pl.when(t != 0)
    def _():
        kv_scr[:, 0:HALO, :] = kv_scr[:, T:T + HALO, :]

    lane = jax.lax.broadcasted_iota(jnp.int32, (RB, LANES), 1)
    first_half = (lane % HEAD_DIM) < ROPE_HALF
    low_half = lane < HEAD_DIM
    scale = np.float32(HEAD_DIM ** -0.5) * LOG2E

    for rb in range(N_ROW_BLOCKS):
        rows = slice(rb * RB, (rb + 1) * RB)
        x = x_ref[0, rows, :]
        ms = jnp.mean(x * x, axis=-1, keepdims=True)
        h = (x * jax.lax.rsqrt(ms + EPS) * ng_ref[...]).astype(bf16)
        h_scr[rows, 0:D_MODEL] = h

        cb, sb, sb_sgn = rrow_ref[0, rows, :], rrow_ref[1, rows, :], rrow_ref[2, rows, :]
        ca, sa, sa_sgn = rtile_ref[t, 0:1, :], rtile_ref[t, 1:2, :], rtile_ref[t, 2:3, :]
        cos_t = ca * cb - sa * sb
        sin_t = sa_sgn * cb + ca * sb_sgn

        def rope(v):
            nxt = pltpu.roll(v, LANES - ROPE_HALF, 1)
            prv = pltpu.roll(v, ROPE_HALF, 1)
            return v * cos_t + jnp.where(first_half, nxt, prv) * sin_t

        k2 = rope(_dot(h, wa_ref[:, OFF_K:OFF_K + LANES]))
        v2 = _dot(h, wa_ref[:, OFF_V:OFF_V + LANES])
        zeros = jnp.zeros_like(k2)
        krows = slice(HALO + rb * RB, HALO + (rb + 1) * RB)
        for i, src in enumerate((k2, v2)):
            h0_lo = jnp.where(low_half, src, zeros)
            h1_hi = jnp.where(low_half, zeros, src)
            h0_hi = pltpu.roll(h0_lo, HEAD_DIM, 1)
            h1_lo = pltpu.roll(h1_hi, HEAD_DIM, 1)
            kv_scr[4 * i + 0, krows, :] = h0_lo.astype(bf16)
            kv_scr[4 * i + 1, krows, :] = h0_hi.astype(bf16)
            kv_scr[4 * i + 2, krows, :] = h1_lo.astype(bf16)
            kv_scr[4 * i + 3, krows, :] = h1_hi.astype(bf16)

        q = _dot(h, wa_ref[:, OFF_Q:OFF_Q + 1024])
        for j in range(1024 // LANES):
            qj = rope(q[:, j * LANES:(j + 1) * LANES]) * scale
            q_scr[rows, j * LANES:(j + 1) * LANES] = qj.astype(bf16)
        za_scr[rows, :] = _silu(_dot(h, wa_ref[:, OFF_ZA:OFF_ZA + 1024]))

    key_off = jax.lax.broadcasted_iota(jnp.int32, (KEY_BLOCK, 1), 0)
    units = [(hk, par) for hk in range(ATT_KV_HEADS) for par in range(2)]
    for c in range(N_CHUNKS):
        r0 = c * CHUNK
        rows = slice(r0, r0 + CHUNK)
        keys = slice(r0, r0 + KEY_BLOCK)
        valid = (key_off + (t * T + r0 - HALO)) >= 0
        scores = []
        for hk, par in units:
            qs = jnp.concatenate(
                [q_scr[rows, (hk * PAIRS_PER_KV + j) * LANES:(hk * PAIRS_PER_KV + j + 1) * LANES]
                 for j in range(PAIRS_PER_KV)], axis=0)
            scores.append(_dot_nt(kv_scr[2 * hk + par, keys, :], qs))

        side, rb = divmod(c, N_ROW_BLOCKS)
        srows = slice(rb * RB, (rb + 1) * RB)
        raw = _dot(h_scr[srows, 0:D_MODEL],
                   wa_ref[:, SIDE_OFF[side]:SIDE_OFF[side] + D_MODEL])
        if side == SIDE_U:
            uz_scr[srows, :] = _gelu_exact(raw)
        elif side == SIDE_VS:
            vs = _gelu_exact(raw)
            mu = jnp.mean(vs, axis=-1, keepdims=True)
            vc = vs - mu
            var = jnp.mean(vc * vc, axis=-1, keepdims=True)
            vn_scr[srows, 0:D_MODEL] = (vc * jax.lax.rsqrt(var + EPS) * lng_ref[...]
                                        + lnb_ref[...]).astype(bf16)
        elif side == SIDE_ZS:
            uz_scr[srows, :] = uz_scr[srows, :] * _silu(raw)
        else:
            ga_scr[srows, :] = _sigmoid(raw + bm_ref[:, 0:1024])

        probs = []
        for (hk, par), s in zip(units, scores):
            if r0 < HALO:
                s = jnp.where(valid, s, MASK_VALUE)
            sink = sink_ref[2 * hk + par, 0:1, :] * LOG2E
            m = jnp.maximum(jnp.max(s, axis=0, keepdims=True), sink)
            p = jnp.exp2(s - m)
            denom = jnp.sum(p, axis=0, keepdims=True) + jnp.exp2(sink - m)
            probs.append((p * (1.0 / denom)).astype(bf16))
        for hk in range(ATT_KV_HEADS):
            o = (_dot_tn(probs[2 * hk], kv_scr[4 + 2 * hk, keys, :])
                 + _dot_tn(probs[2 * hk + 1], kv_scr[5 + 2 * hk, keys, :]))
            for j in range(PAIRS_PER_KV):
                c0 = (hk * PAIRS_PER_KV + j) * LANES
                gated = o[j * CHUNK:(j + 1) * CHUNK] * za_scr[rows, c0:c0 + LANES]
                ag_scr[rows, c0:c0 + LANES] = gated.astype(bf16)

    pos_i = jax.lax.broadcasted_iota(jnp.int32, (SG_BLOCK, SG_BLOCK), 0) // CHUNK
    pos_j = jax.lax.broadcasted_iota(jnp.int32, (SG_BLOCK, SG_BLOCK), 1) // CHUNK
    causal = pos_j <= pos_i
    nblk = T // SG_BLOCK
    for g in range(SG_GROUPS):
        wg = jnp.where(causal, sgw_ref[g], 0.0).astype(bf16)
        c0 = g * LANES
        rhs = jnp.concatenate(
            [vn_scr[b * SG_BLOCK:(b + 1) * SG_BLOCK, c0:c0 + LANES] for b in range(nblk)], axis=1)
        mixed = _dot(wg, rhs)
        for b in range(nblk):
            brows = slice(b * SG_BLOCK, (b + 1) * SG_BLOCK)
            blk = mixed[:, b * LANES:(b + 1) * LANES] + sgb_ref[g]
            sg_scr[brows, c0:c0 + LANES] = (uz_scr[brows, c0:c0 + LANES] * blk).astype(bf16)

    for rb in range(N_ROW_BLOCKS):
        rows = slice(rb * RB, (rb + 1) * RB)
        y = ga_scr[rows, :] * _dot(ag_scr[rows, 0:D_MODEL], wout_ref[:, OFF_WAO:OFF_WAO + D_MODEL])
        gs = _sigmoid(_dot(h_scr[rows, 0:D_MODEL], wa_ref[:, OFF_GS:OFF_GS + D_MODEL])
                      + bm_ref[:, 1024:2048])
        y = y + gs * _dot(sg_scr[rows, 0:D_MODEL], wout_ref[:, OFF_WSO:OFF_WSO + D_MODEL])
        xo = x_ref[0, rows, :] + _dot(y.astype(bf16), wout_ref[:, OFF_WO:OFF_WO + D_MODEL])
        ms2 = jnp.mean(xo * xo, axis=-1, keepdims=True)
        out_ref[0, rows, :] = xo * jax.lax.rsqrt(ms2 + EPS) * fg_ref[...]


def _rope_tables(seq, tile):
    lane = np.arange(LANES) % HEAD_DIM
    inv_freq = ROPE_THETA ** (-(np.arange(ROPE_HALF, dtype=np.float64) * 2.0) / ROPE_DIM)
    freq = np.where(lane < ROPE_DIM, inv_freq[lane % ROPE_HALF], 0.0)
    sign = np.where(lane < ROPE_HALF, -1.0, 1.0)

    def tables(pos):
        ang = pos.astype(np.float64)[:, None] * freq[None, :]
        return np.stack([np.cos(ang), np.sin(ang), np.sin(ang) * sign], axis=0)

    row_tab = tables(np.arange(tile))
    tile_tab = np.zeros((seq // tile, 8, LANES))
    tile_tab[:, 0:3, :] = np.transpose(tables(np.arange(seq // tile) * tile), (1, 0, 2))
    return jnp.asarray(row_tab, jnp.float32), jnp.asarray(tile_tab, jnp.float32)


@jax.jit
def kernel(x, norm_g, w_in, b_merge, att_sinks, sg_w, sg_b, sg_ln_g, sg_ln_b,
           w_att_out, w_sg_out, w_o, final_g):
    B, S, D = x.shape
    T = SEQ_TILE
    assert D == D_MODEL and S % T == 0 and w_in.shape == (1, D, IN_WIDTH)
    bf16 = jnp.bfloat16
    f32 = jnp.float32

    row_tab, tile_tab = _rope_tables(S, T)
    sinks = att_sinks[0].astype(f32).reshape(ATT_KV_HEADS, PAIRS_PER_KV, 2)
    sink_rows = jnp.repeat(jnp.transpose(sinks, (0, 2, 1)).reshape(4, PAIRS_PER_KV), CHUNK, axis=1)
    sink_rows = jnp.broadcast_to(sink_rows[:, None, :], (4, 8, STACK_ROWS))
    sgb = jnp.broadcast_to(sg_b[0].astype(f32)[:, :, None], (SG_GROUPS, SG_BLOCK, LANES))

    w_in_bf = w_in[0].astype(bf16)
    w_out_bf = jnp.concatenate(
        [w_att_out[0], w_sg_out[0], w_o[0], jnp.zeros((D, OUT_W_WIDTH - 3 * D), w_o.dtype)],
        axis=1).astype(bf16)

    def const(shape):
        zeros = (0,) * len(shape)
        return pl.BlockSpec(shape, lambda b, t: zeros, pipeline_mode=pl.Buffered(1))

    grid_spec = pltpu.PrefetchScalarGridSpec(
        num_scalar_prefetch=0,
        grid=(B, S // T),
        in_specs=[
            pl.BlockSpec((1, T, D), lambda b, t: (b, t, 0)),
            const((1, D)),
            const((D, IN_WIDTH)),
            const((D, OUT_W_WIDTH)),
            const((1, 2 * D)),
            const((4, 8, STACK_ROWS)),
            const((SG_GROUPS, SG_BLOCK, SG_BLOCK)),
            const((SG_GROUPS, SG_BLOCK, LANES)),
            const((1, D)),
            const((1, D)),
            const((1, D)),
            const((3, T, LANES)),
            const((S // T, 8, LANES)),
        ],
        out_specs=pl.BlockSpec((1, T, D), lambda b, t: (b, t, 0)),
        scratch_shapes=[
            pltpu.VMEM((T, PAD_WIDTH), bf16),
            pltpu.VMEM((T, PAD_WIDTH), bf16),
            pltpu.VMEM((T, D), f32),
            pltpu.VMEM((8, HALO + T, LANES), bf16),
            pltpu.VMEM((T, PAD_WIDTH), bf16),
            pltpu.VMEM((T, PAD_WIDTH), bf16),
            pltpu.VMEM((T, PAD_WIDTH), bf16),
            pltpu.VMEM((T, D), f32),
            pltpu.VMEM((T, D), f32),
        ],
    )
    return pl.pallas_call(
        _block_kernel,
        grid_spec=grid_spec,
        out_shape=jax.ShapeDtypeStruct((B, S, D), x.dtype),
        compiler_params=pltpu.CompilerParams(
            dimension_semantics=("arbitrary", "arbitrary"),
            vmem_limit_bytes=VMEM_LIMIT_BYTES,
        ),
        name="hybrid_block",
    )(
        x, norm_g.astype(f32), w_in_bf, w_out_bf, b_merge.astype(f32), sink_rows,
        sg_w[0].astype(f32), sgb, sg_ln_g.astype(f32), sg_ln_b.astype(f32),
        final_g.reshape(1, D).astype(f32), row_tab, tile_tab,
    )
```

```python
import numpy as np
import jax
import jax.numpy as jnp
from jax.experimental import pallas as pl
from jax.experimental.pallas import tpu as pltpu

D_MODEL = 1024
CHUNK = 64
EPS = 1e-6
ATT_HEADS = 16
ATT_KV_HEADS = 2
HEAD_DIM = 64
ATT_GROUP = ATT_HEADS // ATT_KV_HEADS
WINDOW_CHUNKS = 2
HALO = WINDOW_CHUNKS * CHUNK
KEY_BLOCK = HALO + CHUNK
ROPE_DIM = HEAD_DIM // 4
ROPE_HALF = ROPE_DIM // 2
ROPE_THETA = 500000.0
SG_BLOCK = 128
SG_GROUPS = 8
LANES = 128
PAIRS_PER_KV = ATT_GROUP // 2
STACK_ROWS = PAIRS_PER_KV * CHUNK
MASK_VALUE = -1e30

OFF_Q, OFF_K, OFF_V, OFF_ZA = 0, 1024, 1152, 1280
OFF_U, OFF_VS, OFF_ZS, OFF_GM = 2304, 3328, 4352, 5376
IN_WIDTH = 7424
N_SIDE = 4
SIDE_U, SIDE_VS, SIDE_ZS, SIDE_GA = range(N_SIDE)
SIDE_OFF = (OFF_U, OFF_VS, OFF_ZS, OFF_GM)
OFF_GS = OFF_GM + 1024
OFF_WAO, OFF_WSO, OFF_WO = 0, 1024, 2048
PAD_WIDTH = D_MODEL + LANES
OUT_W_WIDTH = 3 * D_MODEL + LANES
LOG2E = np.float32(1.4426950408889634)
LOAD_ROWS = 32
N_LOADS = D_MODEL // LOAD_ROWS

SEQ_TILE = 512
ROW_BLOCK = 256
N_ROW_BLOCKS = SEQ_TILE // ROW_BLOCK
N_CHUNKS = SEQ_TILE // CHUNK
VMEM_LIMIT_BYTES = 58 * 1024 * 1024

assert N_CHUNKS == N_SIDE * N_ROW_BLOCKS


def _dot(a, b):
    return jnp.dot(a, b, preferred_element_type=jnp.float32)


def _dot_nt(a, b):
    return jax.lax.dot_general(a, b, (((1,), (1,)), ((), ())),
                               preferred_element_type=jnp.float32)


def _dot_tn(a, b):
    return jax.lax.dot_general(a, b, (((0,), (0,)), ((), ())),
                               preferred_element_type=jnp.float32)


def _sigmoid(x):
    return 0.5 * jnp.tanh(0.5 * x) + 0.5


def _silu(x):
    hx = 0.5 * x
    return hx * (1.0 + jnp.tanh(hx))


def _gelu_exact(x):
    return 0.5 * x * (1.0 + jax.lax.erf(x * np.float32(1.0 / np.sqrt(2.0))))


def _load_weights(w_in_hbm, w_out_hbm, wa_ref, wout_ref, stage_in, stage_out, sems):
    n_out = len(w_out_hbm)

    def copies(i, slot):
        rows = pl.ds(pl.multiple_of(i * LOAD_ROWS, LOAD_ROWS), LOAD_ROWS)
        cps = [pltpu.make_async_copy(w_in_hbm.at[0, rows, :], stage_in.at[slot], sems.at[slot, 0])]
        for k in range(n_out):
            cps.append(pltpu.make_async_copy(w_out_hbm[k].at[0, rows, :], stage_out.at[slot, k],
                                             sems.at[slot, 1 + k]))
        return cps

    for cp in copies(0, 0):
        cp.start()

    def body(i, carry):
        slot = i % 2

        @pl.when(i + 1 < N_LOADS)
        def _():
            for cp in copies(i + 1, 1 - slot):
                cp.start()

        for cp in copies(i, slot):
            cp.wait()
        rows = pl.ds(pl.multiple_of(i * LOAD_ROWS, LOAD_ROWS), LOAD_ROWS)
        wa_ref[rows, :] = stage_in[slot].astype(jnp.bfloat16)
        for k in range(n_out):
            wout_ref[rows, k * D_MODEL:(k + 1) * D_MODEL] = stage_out[slot, k].astype(jnp.bfloat16)
        return carry

    jax.lax.fori_loop(0, N_LOADS, body, 0)


def _block_kernel(x_ref, ng_ref, w_in_hbm, wao_hbm, wso_hbm, wo_hbm, bm_ref, sink_ref, sgw_ref, sgb_ref,
                  lng_ref, lnb_ref, fg_ref,
                  rrow_ref, rtile_ref,
                  out_ref,
                  h_scr, q_scr, za_scr, kv_scr, ag_scr, sg_scr, vn_scr, uz_scr, ga_scr,
                  wa_ref, wout_ref, stage_in, stage_out, load_sems):
    t = pl.program_id(1)
    T = SEQ_TILE
    RB = ROW_BLOCK
    bf16 = jnp.bfloat16

    @pl.when((pl.program_id(0) == 0) & (t == 0))
    def _():
        _load_weights(w_in_hbm, (wao_hbm, wso_hbm, wo_hbm), wa_ref, wout_ref,
                      stage_in, stage_out, load_sems)

    @pl.when(t == 0)
    def _():
        kv_scr[:, 0:HALO, :] = jnp.zeros((8, HALO, LANES), bf16)

    @pl.when(t != 0)
    def _():
        kv_scr[:, 0:HALO, :] = kv_scr[:, T:T + HALO, :]

    lane = jax.lax.broadcasted_iota(jnp.int32, (RB, LANES), 1)
    first_half = (lane % HEAD_DIM) < ROPE_HALF
    low_half = lane < HEAD_DIM
    scale = np.float32(HEAD_DIM ** -0.5) * LOG2E

    for rb in range(N_ROW_BLOCKS):
        rows = slice(rb * RB, (rb + 1) * RB)
        x = x_ref[0, rows, :]
        ms = jnp.mean(x * x, axis=-1, keepdims=True)
        h = (x * jax.lax.rsqrt(ms + EPS) * ng_ref[...]).astype(bf16)
        h_scr[rows, 0:D_MODEL] = h

        cb, sb, sb_sgn = rrow_ref[0, rows, :], rrow_ref[1, rows, :], rrow_ref[2, rows, :]
        ca, sa, sa_sgn = rtile_ref[t, 0:1, :], rtile_ref[t, 1:2, :], rtile_ref[t, 2:3, :]
        cos_t = ca * cb - sa * sb
        sin_t = sa_sgn * cb + ca * sb_sgn

        def rope(v):
            nxt = pltpu.roll(v, LANES - ROPE_HALF, 1)
            prv = pltpu.roll(v, ROPE_HALF, 1)
            return v * cos_t + jnp.where(first_half, nxt, prv) * sin_t

        kv2 = _dot(h, wa_ref[:, OFF_K:OFF_V + LANES])
        k2 = rope(kv2[:, 0:LANES])
        v2 = kv2[:, LANES:2 * LANES]
        zeros = jnp.zeros_like(k2)
        krows = slice(HALO + rb * RB, HALO + (rb + 1) * RB)
        for i, src in enumerate((k2, v2)):
            h0_lo = jnp.where(low_half, src, zeros)
            h1_hi = jnp.where(low_half, zeros, src)
            h0_hi = pltpu.roll(h0_lo, HEAD_DIM, 1)
            h1_lo = pltpu.roll(h1_hi, HEAD_DIM, 1)
            kv_scr[4 * i + 0, krows, :] = h0_lo.astype(bf16)
            kv_scr[4 * i + 1, krows, :] = h0_hi.astype(bf16)
            kv_scr[4 * i + 2, krows, :] = h1_lo.astype(bf16)
            kv_scr[4 * i + 3, krows, :] = h1_hi.astype(bf16)

        q = _dot(h, wa_ref[:, OFF_Q:OFF_Q + 1024])
        for j in range(1024 // LANES):
            qj = rope(q[:, j * LANES:(j + 1) * LANES]) * scale
            q_scr[rows, j * LANES:(j + 1) * LANES] = qj.astype(bf16)
        za_scr[rows, :] = _silu(_dot(h, wa_ref[:, OFF_ZA:OFF_ZA + 1024]))

    key_off = jax.lax.broadcasted_iota(jnp.int32, (KEY_BLOCK, 1), 0)
    units = [(hk, par) for hk in range(ATT_KV_HEADS) for par in range(2)]
    for c in range(N_CHUNKS):
        r0 = c * CHUNK
        rows = slice(r0, r0 + CHUNK)
        keys = slice(r0, r0 + KEY_BLOCK)
        valid = (key_off + (t * T + r0 - HALO)) >= 0
        scores = []
        for hk, par in units:
            qs = jnp.concatenate(
                [q_scr[rows, (hk * PAIRS_PER_KV + j) * LANES:(hk * PAIRS_PER_KV + j + 1) * LANES]
                 for j in range(PAIRS_PER_KV)], axis=0)
            scores.append(_dot_nt(kv_scr[2 * hk + par, keys, :], qs))

        side, rb = divmod(c, N_ROW_BLOCKS)
        srows = slice(rb * RB, (rb + 1) * RB)
        raw = _dot(h_scr[srows, 0:D_MODEL],
                   wa_ref[:, SIDE_OFF[side]:SIDE_OFF[side] + D_MODEL])
        if side == SIDE_U:
            uz_scr[srows, :] = _gelu_exact(raw)
        elif side == SIDE_VS:
            vs = _gelu_exact(raw)
            mu = jnp.mean(vs, axis=-1, keepdims=True)
            vc = vs - mu
            var = jnp.mean(vc * vc, axis=-1, keepdims=True)
            vn_scr[srows, 0:D_MODEL] = (vc * jax.lax.rsqrt(var + EPS) * lng_ref[...]
                                        + lnb_ref[...]).astype(bf16)
        elif side == SIDE_ZS:
            uz_scr[srows, :] = uz_scr[srows, :] * _silu(raw)
        else:
            ga_scr[srows, :] = _sigmoid(raw + bm_ref[:, 0:1024])

        probs = []
        for (hk, par), s in zip(units, scores):
            if r0 < HALO:
                s = jnp.where(valid, s, MASK_VALUE)
            sink = sink_ref[2 * hk + par, 0:1, :] * LOG2E
            m = jnp.maximum(jnp.max(s, axis=0, keepdims=True), sink)
            p = jnp.exp2(s - m)
            denom = jnp.sum(p, axis=0, keepdims=True) + jnp.exp2(sink - m)
            probs.append((p * (1.0 / denom)).astype(bf16))
        for hk in range(ATT_KV_HEADS):
            o = (_dot_tn(probs[2 * hk], kv_scr[4 + 2 * hk, keys, :])
                 + _dot_tn(probs[2 * hk + 1], kv_scr[5 + 2 * hk, keys, :]))
            for j in range(PAIRS_PER_KV):
                c0 = (hk * PAIRS_PER_KV + j) * LANES
                gated = o[j * CHUNK:(j + 1) * CHUNK] * za_scr[rows, c0:c0 + LANES]
                ag_scr[rows, c0:c0 + LANES] = gated.astype(bf16)

    pos_i = jax.lax.broadcasted_iota(jnp.int32, (SG_BLOCK, SG_BLOCK), 0) // CHUNK
    pos_j = jax.lax.broadcasted_iota(jnp.int32, (SG_BLOCK, SG_BLOCK), 1) // CHUNK
    causal = pos_j <= pos_i
    nblk = T // SG_BLOCK
    for g in range(SG_GROUPS):
        wg = jnp.where(causal, sgw_ref[g], 0.0).astype(bf16)
        c0 = g * LANES
        rhs = jnp.concatenate(
            [vn_scr[b * SG_BLOCK:(b + 1) * SG_BLOCK, c0:c0 + LANES] for b in range(nblk)], axis=1)
        mixed = _dot(wg, rhs)
        for b in range(nblk):
            brows = slice(b * SG_BLOCK, (b + 1) * SG_BLOCK)
            blk = mixed[:, b * LANES:(b + 1) * LANES] + sgb_ref[g]
            sg_scr[brows, c0:c0 + LANES] = (uz_scr[brows, c0:c0 + LANES] * blk).astype(bf16)

    for rb in range(N_ROW_BLOCKS):
        rows = slice(rb * RB, (rb + 1) * RB)
        y = ga_scr[rows, :] * _dot(ag_scr[rows, 0:D_MODEL], wout_ref[:, OFF_WAO:OFF_WAO + D_MODEL])
        gs = _sigmoid(_dot(h_scr[rows, 0:D_MODEL], wa_ref[:, OFF_GS:OFF_GS + D_MODEL])
                      + bm_ref[:, 1024:2048])
        y = y + gs * _dot(sg_scr[rows, 0:D_MODEL], wout_ref[:, OFF_WSO:OFF_WSO + D_MODEL])
        xo = x_ref[0, rows, :] + _dot(y.astype(bf16), wout_ref[:, OFF_WO:OFF_WO + D_MODEL])
        ms2 = jnp.mean(xo * xo, axis=-1, keepdims=True)
        out_ref[0, rows, :] = xo * jax.lax.rsqrt(ms2 + EPS) * fg_ref[...]


def _rope_tables(seq, tile):
    lane = np.arange(LANES) % HEAD_DIM
    inv_freq = ROPE_THETA ** (-(np.arange(ROPE_HALF, dtype=np.float64) * 2.0) / ROPE_DIM)
    freq = np.where(lane < ROPE_DIM, inv_freq[lane % ROPE_HALF], 0.0)
    sign = np.where(lane < ROPE_HALF, -1.0, 1.0)

    def tables(pos):
        ang = pos.astype(np.float64)[:, None] * freq[None, :]
        return np.stack([np.cos(ang), np.sin(ang), np.sin(ang) * sign], axis=0)

    row_tab = tables(np.arange(tile))
    tile_tab = np.zeros((seq // tile, 8, LANES))
    tile_tab[:, 0:3, :] = np.transpose(tables(np.arange(seq // tile) * tile), (1, 0, 2))
    return jnp.asarray(row_tab, jnp.float32), jnp.asarray(tile_tab, jnp.float32)


@jax.jit
def kernel(x, norm_g, w_in, b_merge, att_sinks, sg_w, sg_b, sg_ln_g, sg_ln_b,
           w_att_out, w_sg_out, w_o, final_g):
    B, S, D = x.shape
    T = SEQ_TILE
    assert D == D_MODEL and S % T == 0 and w_in.shape == (1, D, IN_WIDTH)
    bf16 = jnp.bfloat16
    f32 = jnp.float32

    row_tab, tile_tab = _rope_tables(S, T)
    sinks = att_sinks[0].astype(f32).reshape(ATT_KV_HEADS, PAIRS_PER_KV, 2)
    sink_rows = jnp.repeat(jnp.transpose(sinks, (0, 2, 1)).reshape(4, PAIRS_PER_KV), CHUNK, axis=1)
    sink_rows = jnp.broadcast_to(sink_rows[:, None, :], (4, 8, STACK_ROWS))
    sgb = jnp.broadcast_to(sg_b[0].astype(f32)[:, :, None], (SG_GROUPS, SG_BLOCK, LANES))

    hbm = pl.BlockSpec(memory_space=pl.ANY)

    def const(shape):
        zeros = (0,) * len(shape)
        return pl.BlockSpec(shape, lambda b, t: zeros, pipeline_mode=pl.Buffered(1))

    grid_spec = pltpu.PrefetchScalarGridSpec(
        num_scalar_prefetch=0,
        grid=(B, S // T),
        in_specs=[
            pl.BlockSpec((1, T, D), lambda b, t: (b, t, 0)),
            const((1, D)),
            hbm, hbm, hbm, hbm,
            const((1, 2 * D)),
            const((4, 8, STACK_ROWS)),
            const((SG_GROUPS, SG_BLOCK, SG_BLOCK)),
            const((SG_GROUPS, SG_BLOCK, LANES)),
            const((1, D)),
            const((1, D)),
            const((1, D)),
            const((3, T, LANES)),
            const((S // T, 8, LANES)),
        ],
        out_specs=pl.BlockSpec((1, T, D), lambda b, t: (b, t, 0)),
        scratch_shapes=[
            pltpu.VMEM((T, PAD_WIDTH), bf16),
            pltpu.VMEM((T, PAD_WIDTH), bf16),
            pltpu.VMEM((T, D), f32),
            pltpu.VMEM((8, HALO + T, LANES), bf16),
            pltpu.VMEM((T, PAD_WIDTH), bf16),
            pltpu.VMEM((T, PAD_WIDTH), bf16),
            pltpu.VMEM((T, PAD_WIDTH), bf16),
            pltpu.VMEM((T, D), f32),
            pltpu.VMEM((T, D), f32),
            pltpu.VMEM((D, IN_WIDTH), bf16),
            pltpu.VMEM((D, OUT_W_WIDTH), bf16),
            pltpu.VMEM((2, LOAD_ROWS, IN_WIDTH), f32),
            pltpu.VMEM((2, 3, LOAD_ROWS, D), f32),
            pltpu.SemaphoreType.DMA((2, 4)),
        ],
    )
    return pl.pallas_call(
        _block_kernel,
        grid_spec=grid_spec,
        out_shape=jax.ShapeDtypeStruct((B, S, D), x.dtype),
        compiler_params=pltpu.CompilerParams(
            dimension_semantics=("arbitrary", "arbitrary"),
            vmem_limit_bytes=VMEM_LIMIT_BYTES,
        ),
        name="hybrid_block",
    )(
        x, norm_g.astype(f32), w_in.astype(f32), w_att_out.astype(f32), w_sg_out.astype(f32),
        w_o.astype(f32), b_merge.astype(f32), sink_rows,
        sg_w[0].astype(f32), sgb, sg_ln_g.astype(f32), sg_ln_b.astype(f32),
        final_g.reshape(1, D).astype(f32), row_tab, tile_tab,
    )
```

```python
import numpy as np
import jax
import jax.numpy as jnp
from jax.experimental import pallas as pl
from jax.experimental.pallas import tpu as pltpu

D_MODEL = 1024
CHUNK = 64
EPS = 1e-6
ATT_HEADS = 16
ATT_KV_HEADS = 2
HEAD_DIM = 64
ATT_GROUP = ATT_HEADS // ATT_KV_HEADS
WINDOW_CHUNKS = 2
HALO = WINDOW_CHUNKS * CHUNK
KEY_BLOCK = HALO + CHUNK
ROPE_DIM = HEAD_DIM // 4
ROPE_HALF = ROPE_DIM // 2
ROPE_THETA = 500000.0
SG_BLOCK = 128
SG_GROUPS = 8
LANES = 128
PAIRS_PER_KV = ATT_GROUP // 2
STACK_ROWS = PAIRS_PER_KV * CHUNK
MASK_VALUE = -1e30

OFF_Q, OFF_K, OFF_V, OFF_ZA = 0, 1024, 1152, 1280
OFF_U, OFF_VS, OFF_ZS, OFF_GM = 2304, 3328, 4352, 5376
IN_WIDTH = 7424
N_SIDE = 4
SIDE_U, SIDE_VS, SIDE_ZS, SIDE_GA = range(N_SIDE)
SIDE_OFF = (OFF_U, OFF_VS, OFF_ZS, OFF_GM)
OFF_GS = OFF_GM + 1024
OFF_WAO, OFF_WSO, OFF_WO = 0, 1024, 2048
PAD_WIDTH = D_MODEL + LANES
OUT_W_WIDTH = 3 * D_MODEL + LANES
LOG2E = np.float32(1.4426950408889634)
LOAD_ROWS = 32
N_LOADS = D_MODEL // LOAD_ROWS
LOAD_SLOTS = 4
assert N_LOADS >= LOAD_SLOTS

SEQ_TILE = 512
ROW_BLOCK = 256
N_ROW_BLOCKS = SEQ_TILE // ROW_BLOCK
N_CHUNKS = SEQ_TILE // CHUNK
VMEM_LIMIT_BYTES = 58 * 1024 * 1024

assert N_CHUNKS == N_SIDE * N_ROW_BLOCKS


def _dot(a, b):
    return jnp.dot(a, b, preferred_element_type=jnp.float32)


def _dot_nt(a, b):
    return jax.lax.dot_general(a, b, (((1,), (1,)), ((), ())),
                               preferred_element_type=jnp.float32)


def _dot_tn(a, b):
    return jax.lax.dot_general(a, b, (((0,), (0,)), ((), ())),
                               preferred_element_type=jnp.float32)


def _sigmoid(x):
    return 0.5 * jnp.tanh(0.5 * x) + 0.5


def _silu(x):
    hx = 0.5 * x
    return hx * (1.0 + jnp.tanh(hx))


def _gelu_exact(x):
    return 0.5 * x * (1.0 + jax.lax.erf(x * np.float32(1.0 / np.sqrt(2.0))))


def _load_weights(w_in_hbm, w_out_hbm, wa_ref, wout_ref, stage_in, stage_out, sems):
    n_out = len(w_out_hbm)
    ahead = LOAD_SLOTS - 1

    def copies(i, slot):
        rows = pl.ds(pl.multiple_of(i * LOAD_ROWS, LOAD_ROWS), LOAD_ROWS)
        cps = [pltpu.make_async_copy(w_in_hbm.at[0, rows, :], stage_in.at[slot], sems.at[slot, 0])]
        for k in range(n_out):
            cps.append(pltpu.make_async_copy(w_out_hbm[k].at[0, rows, :], stage_out.at[slot, k],
                                             sems.at[slot, 1 + k]))
        return cps

    for i in range(ahead):
        for cp in copies(i, i):
            cp.start()

    def body(i, carry):
        slot = i % LOAD_SLOTS

        @pl.when(i + ahead < N_LOADS)
        def _():
            for cp in copies(i + ahead, (i + ahead) % LOAD_SLOTS):
                cp.start()

        for cp in copies(i, slot):
            cp.wait()
        rows = pl.ds(pl.multiple_of(i * LOAD_ROWS, LOAD_ROWS), LOAD_ROWS)
        wa_ref[rows, :] = stage_in[slot].astype(jnp.bfloat16)
        for k in range(n_out):
            wout_ref[rows, k * D_MODEL:(k + 1) * D_MODEL] = stage_out[slot, k].astype(jnp.bfloat16)
        return carry

    jax.lax.fori_loop(0, N_LOADS, body, 0)


def _block_kernel(x_ref, ng_ref, w_in_hbm, wao_hbm, wso_hbm, wo_hbm, bm_ref, sink_ref, sgw_ref, sgb_ref,
                  lng_ref, lnb_ref, fg_ref,
                  rrow_ref, rtile_ref,
                  out_ref,
                  h_scr, q_scr, za_scr, kv_scr, ag_scr, sg_scr, vn_scr, uz_scr, ga_scr,
                  wa_ref, wout_ref, stage_in, stage_out, load_sems):
    t = pl.program_id(1)
    T = SEQ_TILE
    RB = ROW_BLOCK
    bf16 = jnp.bfloat16

    @pl.when((pl.program_id(0) == 0) & (t == 0))
    def _():
        _load_weights(w_in_hbm, (wao_hbm, wso_hbm, wo_hbm), wa_ref, wout_ref,
                      stage_in, stage_out, load_sems)

    @pl.when(t == 0)
    def _():
        kv_scr[:, 0:HALO, :] = jnp.zeros((8, HALO, LANES), bf16)

    @pl.when(t != 0)
    def _():
        kv_scr[:, 0:HALO, :] = kv_scr[:, T:T + HALO, :]

    lane = jax.lax.broadcasted_iota(jnp.int32, (RB, LANES), 1)
    first_half = (lane % HEAD_DIM) < ROPE_HALF
    low_half = lane < HEAD_DIM
    scale = np.float32(HEAD_DIM ** -0.5) * LOG2E

    for rb in range(N_ROW_BLOCKS):
        rows = slice(rb * RB, (rb + 1) * RB)
        x = x_ref[0, rows, :]
        ms = jnp.mean(x * x, axis=-1, keepdims=True)
        h = (x * jax.lax.rsqrt(ms + EPS) * ng_ref[...]).astype(bf16)
        h_scr[rows, 0:D_MODEL] = h

        cb, sb, sb_sgn = rrow_ref[0, rows, :], rrow_ref[1, rows, :], rrow_ref[2, rows, :]
        ca, sa, sa_sgn = rtile_ref[t, 0:1, :], rtile_ref[t, 1:2, :], rtile_ref[t, 2:3, :]
        cos_t = ca * cb - sa * sb
        sin_t = sa_sgn * cb + ca * sb_sgn

        def rope(v):
            nxt = pltpu.roll(v, LANES - ROPE_HALF, 1)
            prv = pltpu.roll(v, ROPE_HALF, 1)
            return v * cos_t + jnp.where(first_half, nxt, prv) * sin_t

        kv2 = _dot(h, wa_ref[:, OFF_K:OFF_V + LANES])
        k2 = rope(kv2[:, 0:LANES])
        v2 = kv2[:, LANES:2 * LANES]
        zeros = jnp.zeros_like(k2)
        krows = slice(HALO + rb * RB, HALO + (rb + 1) * RB)
        for i, src in enumerate((k2, v2)):
            h0_lo = jnp.where(low_half, src, zeros)
            h1_hi = jnp.where(low_half, zeros, src)
            h0_hi = pltpu.roll(h0_lo, HEAD_DIM, 1)
            h1_lo = pltpu.roll(h1_hi, HEAD_DIM, 1)
            kv_scr[4 * i + 0, krows, :] = h0_lo.astype(bf16)
            kv_scr[4 * i + 1, krows, :] = h0_hi.astype(bf16)
            kv_scr[4 * i + 2, krows, :] = h1_lo.astype(bf16)
            kv_scr[4 * i + 3, krows, :] = h1_hi.astype(bf16)

        q = _dot(h, wa_ref[:, OFF_Q:OFF_Q + 1024])
        for j in range(1024 // LANES):
            qj = rope(q[:, j * LANES:(j + 1) * LANES]) * scale
            q_scr[rows, j * LANES:(j + 1) * LANES] = qj.astype(bf16)
        za_scr[rows, :] = _silu(_dot(h, wa_ref[:, OFF_ZA:OFF_ZA + 1024]))

    key_off = jax.lax.broadcasted_iota(jnp.int32, (KEY_BLOCK, 1), 0)
    units = [(hk, par) for hk in range(ATT_KV_HEADS) for par in range(2)]
    for c in range(N_CHUNKS):
        r0 = c * CHUNK
        rows = slice(r0, r0 + CHUNK)
        keys = slice(r0, r0 + KEY_BLOCK)
        valid = (key_off + (t * T + r0 - HALO)) >= 0
        scores = []
        for hk, par in units:
            qs = jnp.concatenate(
                [q_scr[rows, (hk * PAIRS_PER_KV + j) * LANES:(hk * PAIRS_PER_KV + j + 1) * LANES]
                 for j in range(PAIRS_PER_KV)], axis=0)
            scores.append(_dot_nt(kv_scr[2 * hk + par, keys, :], qs))

        side, rb = divmod(c, N_ROW_BLOCKS)
        srows = slice(rb * RB, (rb + 1) * RB)
        raw = _dot(h_scr[srows, 0:D_MODEL],
                   wa_ref[:, SIDE_OFF[side]:SIDE_OFF[side] + D_MODEL])
        if side == SIDE_U:
            uz_scr[srows, :] = _gelu_exact(raw)
        elif side == SIDE_VS:
            vs = _gelu_exact(raw)
            mu = jnp.mean(vs, axis=-1, keepdims=True)
            vc = vs - mu
            var = jnp.mean(vc * vc, axis=-1, keepdims=True)
            vn_scr[srows, 0:D_MODEL] = (vc * jax.lax.rsqrt(var + EPS) * lng_ref[...]
                                        + lnb_ref[...]).astype(bf16)
        elif side == SIDE_ZS:
            uz_scr[srows, :] = uz_scr[srows, :] * _silu(raw)
        else:
            ga_scr[srows, :] = _sigmoid(raw + bm_ref[:, 0:1024])

        probs = []
        for (hk, par), s in zip(units, scores):
            if r0 < HALO:
                s = jnp.where(valid, s, MASK_VALUE)
            sink = sink_ref[2 * hk + par, 0:1, :] * LOG2E
            m = jnp.maximum(jnp.max(s, axis=0, keepdims=True), sink)
            p = jnp.exp2(s - m)
            denom = jnp.sum(p, axis=0, keepdims=True) + jnp.exp2(sink - m)
            probs.append((p * (1.0 / denom)).astype(bf16))
        for hk in range(ATT_KV_HEADS):
            o = (_dot_tn(probs[2 * hk], kv_scr[4 + 2 * hk, keys, :])
                 + _dot_tn(probs[2 * hk + 1], kv_scr[5 + 2 * hk, keys, :]))
            for j in range(PAIRS_PER_KV):
                c0 = (hk * PAIRS_PER_KV + j) * LANES
                gated = o[j * CHUNK:(j + 1) * CHUNK] * za_scr[rows, c0:c0 + LANES]
                ag_scr[rows, c0:c0 + LANES] = gated.astype(bf16)

    pos_i = jax.lax.broadcasted_iota(jnp.int32, (SG_BLOCK, SG_BLOCK), 0) // CHUNK
    pos_j = jax.lax.broadcasted_iota(jnp.int32, (SG_BLOCK, SG_BLOCK), 1) // CHUNK
    causal = pos_j <= pos_i
    nblk = T // SG_BLOCK
    for g in range(SG_GROUPS):
        wg = jnp.where(causal, sgw_ref[g], 0.0).astype(bf16)
        c0 = g * LANES
        rhs = jnp.concatenate(
            [vn_scr[b * SG_BLOCK:(b + 1) * SG_BLOCK, c0:c0 + LANES] for b in range(nblk)], axis=1)
        mixed = _dot(wg, rhs)
        for b in range(nblk):
            brows = slice(b * SG_BLOCK, (b + 1) * SG_BLOCK)
            blk = mixed[:, b * LANES:(b + 1) * LANES] + sgb_ref[g]
            sg_scr[brows, c0:c0 + LANES] = (uz_scr[brows, c0:c0 + LANES] * blk).astype(bf16)

    for rb in range(N_ROW_BLOCKS):
        rows = slice(rb * RB, (rb + 1) * RB)
        y = ga_scr[rows, :] * _dot(ag_scr[rows, 0:D_MODEL], wout_ref[:, OFF_WAO:OFF_WAO + D_MODEL])
        gs = _sigmoid(_dot(h_scr[rows, 0:D_MODEL], wa_ref[:, OFF_GS:OFF_GS + D_MODEL])
                      + bm_ref[:, 1024:2048])
        y = y + gs * _dot(sg_scr[rows, 0:D_MODEL], wout_ref[:, OFF_WSO:OFF_WSO + D_MODEL])
        xo = x_ref[0, rows, :] + _dot(y.astype(bf16), wout_ref[:, OFF_WO:OFF_WO + D_MODEL])
        ms2 = jnp.mean(xo * xo, axis=-1, keepdims=True)
        out_ref[0, rows, :] = xo * jax.lax.rsqrt(ms2 + EPS) * fg_ref[...]


def _rope_tables(seq, tile):
    lane = np.arange(LANES) % HEAD_DIM
    inv_freq = ROPE_THETA ** (-(np.arange(ROPE_HALF, dtype=np.float64) * 2.0) / ROPE_DIM)
    freq = np.where(lane < ROPE_DIM, inv_freq[lane % ROPE_HALF], 0.0)
    sign = np.where(lane < ROPE_HALF, -1.0, 1.0)

    def tables(pos):
        ang = pos.astype(np.float64)[:, None] * freq[None, :]
        return np.stack([np.cos(ang), np.sin(ang), np.sin(ang) * sign], axis=0)

    row_tab = tables(np.arange(tile))
    tile_tab = np.zeros((seq // tile, 8, LANES))
    tile_tab[:, 0:3, :] = np.transpose(tables(np.arange(seq // tile) * tile), (1, 0, 2))
    return jnp.asarray(row_tab, jnp.float32), jnp.asarray(tile_tab, jnp.float32)


@jax.jit
def kernel(x, norm_g, w_in, b_merge, att_sinks, sg_w, sg_b, sg_ln_g, sg_ln_b,
           w_att_out, w_sg_out, w_o, final_g):
    B, S, D = x.shape
    T = SEQ_TILE
    assert D == D_MODEL and S % T == 0 and w_in.shape == (1, D, IN_WIDTH)
    bf16 = jnp.bfloat16
    f32 = jnp.float32

    row_tab, tile_tab = _rope_tables(S, T)
    sinks = att_sinks[0].astype(f32).reshape(ATT_KV_HEADS, PAIRS_PER_KV, 2)
    sink_rows = jnp.repeat(jnp.transpose(sinks, (0, 2, 1)).reshape(4, PAIRS_PER_KV), CHUNK, axis=1)
    sink_rows = jnp.broadcast_to(sink_rows[:, None, :], (4, 8, STACK_ROWS))
    sgb = jnp.broadcast_to(sg_b[0].astype(f32)[:, :, None], (SG_GROUPS, SG_BLOCK, LANES))

    hbm = pl.BlockSpec(memory_space=pl.ANY)

    def const(shape):
        zeros = (0,) * len(shape)
        return pl.BlockSpec(shape, lambda b, t: zeros, pipeline_mode=pl.Buffered(1))

    grid_spec = pltpu.PrefetchScalarGridSpec(
        num_scalar_prefetch=0,
        grid=(B, S // T),
        in_specs=[
            pl.BlockSpec((1, T, D), lambda b, t: (b, t, 0)),
            const((1, D)),
            hbm, hbm, hbm, hbm,
            const((1, 2 * D)),
            const((4, 8, STACK_ROWS)),
            const((SG_GROUPS, SG_BLOCK, SG_BLOCK)),
            const((SG_GROUPS, SG_BLOCK, LANES)),
            const((1, D)),
            const((1, D)),
            const((1, D)),
            const((3, T, LANES)),
            const((S // T, 8, LANES)),
        ],
        out_specs=pl.BlockSpec((1, T, D), lambda b, t: (b, t, 0)),
        scratch_shapes=[
            pltpu.VMEM((T, PAD_WIDTH), bf16),
            pltpu.VMEM((T, PAD_WIDTH), bf16),
            pltpu.VMEM((T, D), f32),
            pltpu.VMEM((8, HALO + T, LANES), bf16),
            pltpu.VMEM((T, PAD_WIDTH), bf16),
            pltpu.VMEM((T, PAD_WIDTH), bf16),
            pltpu.VMEM((T, PAD_WIDTH), bf16),
            pltpu.VMEM((T, D), f32),
            pltpu.VMEM((T, D), f32),
            pltpu.VMEM((D, IN_WIDTH), bf16),
            pltpu.VMEM((D, OUT_W_WIDTH), bf16),
            pltpu.VMEM((LOAD_SLOTS, LOAD_ROWS, IN_WIDTH), f32),
            pltpu.VMEM((LOAD_SLOTS, 3, LOAD_ROWS, D), f32),
            pltpu.SemaphoreType.DMA((LOAD_SLOTS, 4)),
        ],
    )
    return pl.pallas_call(
        _block_kernel,
        grid_spec=grid_spec,
        out_shape=jax.ShapeDtypeStruct((B, S, D), x.dtype),
        compiler_params=pltpu.CompilerParams(
            dimension_semantics=("arbitrary", "arbitrary"),
            vmem_limit_bytes=VMEM_LIMIT_BYTES,
        ),
        name="hybrid_block",
    )(
        x, norm_g.astype(f32), w_in.astype(f32), w_att_out.astype(f32), w_sg_out.astype(f32),
        w_o.astype(f32), b_merge.astype(f32), sink_rows,
        sg_w[0].astype(f32), sgb, sg_ln_g.astype(f32), sg_ln_b.astype(f32),
        final_g.reshape(1, D).astype(f32), row_tab, tile_tab,
    )
```

```python
import numpy as np
import jax
import jax.numpy as jnp
from jax.experimental import pallas as pl
from jax.experimental.pallas import tpu as pltpu

D_MODEL = 1024
CHUNK = 64
EPS = 1e-6
ATT_HEADS = 16
ATT_KV_HEADS = 2
HEAD_DIM = 64
ATT_GROUP = ATT_HEADS // ATT_KV_HEADS
WINDOW_CHUNKS = 2
HALO = WINDOW_CHUNKS * CHUNK
KEY_BLOCK = HALO + CHUNK
ROPE_DIM = HEAD_DIM // 4
ROPE_HALF = ROPE_DIM // 2
ROPE_THETA = 500000.0
SG_BLOCK = 128
SG_GROUPS = 8
LANES = 128
PAIRS_PER_KV = ATT_GROUP // 2
STACK_ROWS = PAIRS_PER_KV * CHUNK
MASK_VALUE = -1e30

OFF_Q, OFF_K, OFF_V, OFF_ZA = 0, 1024, 1152, 1280
OFF_U, OFF_VS, OFF_ZS, OFF_GM = 2304, 3328, 4352, 5376
IN_WIDTH = 7424
N_SIDE = 4
SIDE_U, SIDE_VS, SIDE_ZS, SIDE_GA = range(N_SIDE)
SIDE_OFF = (OFF_U, OFF_VS, OFF_ZS, OFF_GM)
OFF_GS = OFF_GM + 1024
OFF_WAO, OFF_WSO, OFF_WO = 0, 1024, 2048
PAD_WIDTH = D_MODEL + LANES
OUT_W_WIDTH = 3 * D_MODEL + LANES
LOG2E = np.float32(1.4426950408889634)
LOAD_ROWS = 32
N_LOADS = D_MODEL // LOAD_ROWS
LOAD_SLOTS = 4
assert N_LOADS >= LOAD_SLOTS

SEQ_TILE = 512
ROW_BLOCK = 256
N_ROW_BLOCKS = SEQ_TILE // ROW_BLOCK
N_CHUNKS = SEQ_TILE // CHUNK
VMEM_LIMIT_BYTES = 58 * 1024 * 1024

assert N_CHUNKS == N_SIDE * N_ROW_BLOCKS


def _dot(a, b):
    return jnp.dot(a, b, preferred_element_type=jnp.float32)


def _dot_nt(a, b):
    return jax.lax.dot_general(a, b, (((1,), (1,)), ((), ())),
                               preferred_element_type=jnp.float32)


def _sigmoid(x):
    return 0.5 * jnp.tanh(0.5 * x) + 0.5


def _silu(x):
    hx = 0.5 * x
    return hx * (1.0 + jnp.tanh(hx))


def _gelu_exact(x):
    return 0.5 * x * (1.0 + jax.lax.erf(x * np.float32(1.0 / np.sqrt(2.0))))


def _load_weights(w_in_hbm, w_out_hbm, wa_ref, wout_ref, stage_in, stage_out, sems):
    n_out = len(w_out_hbm)
    ahead = LOAD_SLOTS - 1

    def copies(i, slot):
        rows = pl.ds(pl.multiple_of(i * LOAD_ROWS, LOAD_ROWS), LOAD_ROWS)
        cps = [pltpu.make_async_copy(w_in_hbm.at[0, rows, :], stage_in.at[slot], sems.at[slot, 0])]
        for k in range(n_out):
            cps.append(pltpu.make_async_copy(w_out_hbm[k].at[0, rows, :], stage_out.at[slot, k],
                                             sems.at[slot, 1 + k]))
        return cps

    for i in range(ahead):
        for cp in copies(i, i):
            cp.start()

    def body(i, carry):
        slot = i % LOAD_SLOTS

        @pl.when(i + ahead < N_LOADS)
        def _():
            for cp in copies(i + ahead, (i + ahead) % LOAD_SLOTS):
                cp.start()

        for cp in copies(i, slot):
            cp.wait()
        rows = pl.ds(pl.multiple_of(i * LOAD_ROWS, LOAD_ROWS), LOAD_ROWS)
        wa_ref[rows, :] = stage_in[slot].astype(jnp.bfloat16)
        for k in range(n_out):
            wout_ref[rows, k * D_MODEL:(k + 1) * D_MODEL] = stage_out[slot, k].astype(jnp.bfloat16)
        return carry

    jax.lax.fori_loop(0, N_LOADS, body, 0)


def _block_kernel(x_ref, ng_ref, w_in_hbm, wao_hbm, wso_hbm, wo_hbm, bm_ref, sink_ref, sgw_ref, sgb_ref,
                  lng_ref, lnb_ref, fg_ref,
                  rrow_ref, rtile_ref,
                  out_ref,
                  h_scr, q_scr, za_scr, k_scr, v_scr, vt_scr, ag_scr, sg_scr, vn_scr, uz_scr, ga_scr,
                  wa_ref, wout_ref, stage_in, stage_out, load_sems):
    t = pl.program_id(1)
    T = SEQ_TILE
    RB = ROW_BLOCK
    bf16 = jnp.bfloat16

    @pl.when((pl.program_id(0) == 0) & (t == 0))
    def _():
        _load_weights(w_in_hbm, (wao_hbm, wso_hbm, wo_hbm), wa_ref, wout_ref,
                      stage_in, stage_out, load_sems)

    @pl.when(t == 0)
    def _():
        k_scr[:, 0:HALO, :] = jnp.zeros((4, HALO, LANES), bf16)
        v_scr[0:HALO, :] = jnp.zeros((HALO, LANES), jnp.float32)

    @pl.when(t != 0)
    def _():
        k_scr[:, 0:HALO, :] = k_scr[:, T:T + HALO, :]
        v_scr[0:HALO, :] = v_scr[T:T + HALO, :]

    lane = jax.lax.broadcasted_iota(jnp.int32, (RB, LANES), 1)
    first_half = (lane % HEAD_DIM) < ROPE_HALF
    low_half = lane < HEAD_DIM
    scale = np.float32(HEAD_DIM ** -0.5) * LOG2E

    for rb in range(N_ROW_BLOCKS):
        rows = slice(rb * RB, (rb + 1) * RB)
        x = x_ref[0, rows, :]
        ms = jnp.mean(x * x, axis=-1, keepdims=True)
        h = (x * jax.lax.rsqrt(ms + EPS) * ng_ref[...]).astype(bf16)
        h_scr[rows, 0:D_MODEL] = h

        cb, sb, sb_sgn = rrow_ref[0, rows, :], rrow_ref[1, rows, :], rrow_ref[2, rows, :]
        ca, sa, sa_sgn = rtile_ref[t, 0:1, :], rtile_ref[t, 1:2, :], rtile_ref[t, 2:3, :]
        cos_t = ca * cb - sa * sb
        sin_t = sa_sgn * cb + ca * sb_sgn

        def rope(v):
            nxt = pltpu.roll(v, LANES - ROPE_HALF, 1)
            prv = pltpu.roll(v, ROPE_HALF, 1)
            return v * cos_t + jnp.where(first_half, nxt, prv) * sin_t

        kv2 = _dot(h, wa_ref[:, OFF_K:OFF_V + LANES])
        k2 = rope(kv2[:, 0:LANES])
        zeros = jnp.zeros_like(k2)
        krows = slice(HALO + rb * RB, HALO + (rb + 1) * RB)
        h0_lo = jnp.where(low_half, k2, zeros)
        h1_hi = jnp.where(low_half, zeros, k2)
        h0_hi = pltpu.roll(h0_lo, HEAD_DIM, 1)
        h1_lo = pltpu.roll(h1_hi, HEAD_DIM, 1)
        k_scr[0, krows, :] = h0_lo.astype(bf16)
        k_scr[1, krows, :] = h0_hi.astype(bf16)
        k_scr[2, krows, :] = h1_lo.astype(bf16)
        k_scr[3, krows, :] = h1_hi.astype(bf16)
        v_scr[krows, :] = kv2[:, LANES:2 * LANES]

        q = _dot(h, wa_ref[:, OFF_Q:OFF_Q + 1024])
        for j in range(1024 // LANES):
            qj = rope(q[:, j * LANES:(j + 1) * LANES]) * scale
            q_scr[rows, j * LANES:(j + 1) * LANES] = qj.astype(bf16)
        za_scr[rows, :] = _silu(_dot(h, wa_ref[:, OFF_ZA:OFF_ZA + 1024]))

    for a in range(2):
        vt = v_scr[a * CHUNK:HALO + T, :].T
        for hk in range(ATT_KV_HEADS):
            vt_scr[a, hk, :, 0:HALO + T - a * CHUNK] = vt[hk * HEAD_DIM:(hk + 1) * HEAD_DIM].astype(bf16)

    key_off = jax.lax.broadcasted_iota(jnp.int32, (KEY_BLOCK, 1), 0)
    units = [(hk, par) for hk in range(ATT_KV_HEADS) for par in range(2)]
    for c in range(N_CHUNKS):
        r0 = c * CHUNK
        rows = slice(r0, r0 + CHUNK)
        keys = slice(r0, r0 + KEY_BLOCK)
        valid = (key_off + (t * T + r0 - HALO)) >= 0
        scores = []
        for hk, par in units:
            qs = jnp.concatenate(
                [q_scr[rows, (hk * PAIRS_PER_KV + j) * LANES:(hk * PAIRS_PER_KV + j + 1) * LANES]
                 for j in range(PAIRS_PER_KV)], axis=0)
            scores.append(_dot_nt(k_scr[2 * hk + par, keys, :], qs))

        side, rb = divmod(c, N_ROW_BLOCKS)
        srows = slice(rb * RB, (rb + 1) * RB)
        raw = _dot(h_scr[srows, 0:D_MODEL],
                   wa_ref[:, SIDE_OFF[side]:SIDE_OFF[side] + D_MODEL])
        if side == SIDE_U:
            uz_scr[srows, :] = _gelu_exact(raw)
        elif side == SIDE_VS:
            vs = _gelu_exact(raw)
            mu = jnp.mean(vs, axis=-1, keepdims=True)
            vc = vs - mu
            var = jnp.mean(vc * vc, axis=-1, keepdims=True)
            vn_scr[srows, 0:D_MODEL] = (vc * jax.lax.rsqrt(var + EPS) * lng_ref[...]
                                        + lnb_ref[...]).astype(bf16)
        elif side == SIDE_ZS:
            uz_scr[srows, :] = uz_scr[srows, :] * _silu(raw)
        else:
            ga_scr[srows, :] = _sigmoid(raw + bm_ref[:, 0:1024])

        align, lane_tile = c % 2, c // 2
        vkeys = slice(lane_tile * LANES, lane_tile * LANES + KEY_BLOCK)
        outs_t = []
        for (hk, par), s in zip(units, scores):
            if r0 < HALO:
                s = jnp.where(valid, s, MASK_VALUE)
            sink = sink_ref[2 * hk + par, 0:1, :] * LOG2E
            m = jnp.maximum(jnp.max(s, axis=0, keepdims=True), sink)
            p = jnp.exp2(s - m)
            denom = jnp.sum(p, axis=0, keepdims=True) + jnp.exp2(sink - m)
            ot = _dot(vt_scr[align, hk, :, vkeys], p.astype(bf16))
            outs_t.append(ot * (1.0 / denom))
        for hk in range(ATT_KV_HEADS):
            o = jnp.concatenate([outs_t[2 * hk], outs_t[2 * hk + 1]], axis=0).T
            for j in range(PAIRS_PER_KV):
                c0 = (hk * PAIRS_PER_KV + j) * LANES
                gated = o[j * CHUNK:(j + 1) * CHUNK] * za_scr[rows, c0:c0 + LANES]
                ag_scr[rows, c0:c0 + LANES] = gated.astype(bf16)

    pos_i = jax.lax.broadcasted_iota(jnp.int32, (SG_BLOCK, SG_BLOCK), 0) // CHUNK
    pos_j = jax.lax.broadcasted_iota(jnp.int32, (SG_BLOCK, SG_BLOCK), 1) // CHUNK
    causal = pos_j <= pos_i
    nblk = T // SG_BLOCK
    for g in range(SG_GROUPS):
        wg = jnp.where(causal, sgw_ref[g], 0.0).astype(bf16)
        c0 = g * LANES
        rhs = jnp.concatenate(
            [vn_scr[b * SG_BLOCK:(b + 1) * SG_BLOCK, c0:c0 + LANES] for b in range(nblk)], axis=1)
        mixed = _dot(wg, rhs)
        for b in range(nblk):
            brows = slice(b * SG_BLOCK, (b + 1) * SG_BLOCK)
            blk = mixed[:, b * LANES:(b + 1) * LANES] + sgb_ref[g]
            sg_scr[brows, c0:c0 + LANES] = (uz_scr[brows, c0:c0 + LANES] * blk).astype(bf16)

    for rb in range(N_ROW_BLOCKS):
        rows = slice(rb * RB, (rb + 1) * RB)
        y = ga_scr[rows, :] * _dot(ag_scr[rows, 0:D_MODEL], wout_ref[:, OFF_WAO:OFF_WAO + D_MODEL])
        gs = _sigmoid(_dot(h_scr[rows, 0:D_MODEL], wa_ref[:, OFF_GS:OFF_GS + D_MODEL])
                      + bm_ref[:, 1024:2048])
        y = y + gs * _dot(sg_scr[rows, 0:D_MODEL], wout_ref[:, OFF_WSO:OFF_WSO + D_MODEL])
        xo = x_ref[0, rows, :] + _dot(y.astype(bf16), wout_ref[:, OFF_WO:OFF_WO + D_MODEL])
        ms2 = jnp.mean(xo * xo, axis=-1, keepdims=True)
        out_ref[0, rows, :] = xo * jax.lax.rsqrt(ms2 + EPS) * fg_ref[...]


def _rope_tables(seq, tile):
    lane = np.arange(LANES) % HEAD_DIM
    inv_freq = ROPE_THETA ** (-(np.arange(ROPE_HALF, dtype=np.float64) * 2.0) / ROPE_DIM)
    freq = np.where(lane < ROPE_DIM, inv_freq[lane % ROPE_HALF], 0.0)
    sign = np.where(lane < ROPE_HALF, -1.0, 1.0)

    def tables(pos):
        ang = pos.astype(np.float64)[:, None] * freq[None, :]
        return np.stack([np.cos(ang), np.sin(ang), np.sin(ang) * sign], axis=0)

    row_tab = tables(np.arange(tile))
    tile_tab = np.zeros((seq // tile, 8, LANES))
    tile_tab[:, 0:3, :] = np.transpose(tables(np.arange(seq // tile) * tile), (1, 0, 2))
    return jnp.asarray(row_tab, jnp.float32), jnp.asarray(tile_tab, jnp.float32)


@jax.jit
def kernel(x, norm_g, w_in, b_merge, att_sinks, sg_w, sg_b, sg_ln_g, sg_ln_b,
           w_att_out, w_sg_out, w_o, final_g):
    B, S, D = x.shape
    T = SEQ_TILE
    assert D == D_MODEL and S % T == 0 and w_in.shape == (1, D, IN_WIDTH)
    bf16 = jnp.bfloat16
    f32 = jnp.float32

    row_tab, tile_tab = _rope_tables(S, T)
    sinks = att_sinks[0].astype(f32).reshape(ATT_KV_HEADS, PAIRS_PER_KV, 2)
    sink_rows = jnp.repeat(jnp.transpose(sinks, (0, 2, 1)).reshape(4, PAIRS_PER_KV), CHUNK, axis=1)
    sink_rows = jnp.broadcast_to(sink_rows[:, None, :], (4, 8, STACK_ROWS))
    sgb = jnp.broadcast_to(sg_b[0].astype(f32)[:, :, None], (SG_GROUPS, SG_BLOCK, LANES))

    hbm = pl.BlockSpec(memory_space=pl.ANY)

    def const(shape):
        zeros = (0,) * len(shape)
        return pl.BlockSpec(shape, lambda b, t: zeros, pipeline_mode=pl.Buffered(1))

    grid_spec = pltpu.PrefetchScalarGridSpec(
        num_scalar_prefetch=0,
        grid=(B, S // T),
        in_specs=[
            pl.BlockSpec((1, T, D), lambda b, t: (b, t, 0)),
            const((1, D)),
            hbm, hbm, hbm, hbm,
            const((1, 2 * D)),
            const((4, 8, STACK_ROWS)),
            const((SG_GROUPS, SG_BLOCK, SG_BLOCK)),
            const((SG_GROUPS, SG_BLOCK, LANES)),
            const((1, D)),
            const((1, D)),
            const((1, D)),
            const((3, T, LANES)),
            const((S // T, 8, LANES)),
        ],
        out_specs=pl.BlockSpec((1, T, D), lambda b, t: (b, t, 0)),
        scratch_shapes=[
            pltpu.VMEM((T, PAD_WIDTH), bf16),
            pltpu.VMEM((T, PAD_WIDTH), bf16),
            pltpu.VMEM((T, D), f32),
            pltpu.VMEM((4, HALO + T, LANES), bf16),
            pltpu.VMEM((HALO + T, LANES), f32),
            pltpu.VMEM((2, ATT_KV_HEADS, HEAD_DIM, HALO + T), bf16),
            pltpu.VMEM((T, PAD_WIDTH), bf16),
            pltpu.VMEM((T, PAD_WIDTH), bf16),
            pltpu.VMEM((T, PAD_WIDTH), bf16),
            pltpu.VMEM((T, D), f32),
            pltpu.VMEM((T, D), f32),
            pltpu.VMEM((D, IN_WIDTH), bf16),
            pltpu.VMEM((D, OUT_W_WIDTH), bf16),
            pltpu.VMEM((LOAD_SLOTS, LOAD_ROWS, IN_WIDTH), f32),
            pltpu.VMEM((LOAD_SLOTS, 3, LOAD_ROWS, D), f32),
            pltpu.SemaphoreType.DMA((LOAD_SLOTS, 4)),
        ],
    )
    return pl.pallas_call(
        _block_kernel,
        grid_spec=grid_spec,
        out_shape=jax.ShapeDtypeStruct((B, S, D), x.dtype),
        compiler_params=pltpu.CompilerParams(
            dimension_semantics=("arbitrary", "arbitrary"),
            vmem_limit_bytes=VMEM_LIMIT_BYTES,
        ),
        name="hybrid_block",
    )(
        x, norm_g.astype(f32), w_in.astype(f32), w_att_out.astype(f32), w_sg_out.astype(f32),
        w_o.astype(f32), b_merge.astype(f32), sink_rows,
        sg_w[0].astype(f32), sgb, sg_ln_g.astype(f32), sg_ln_b.astype(f32),
        final_g.reshape(1, D).astype(f32), row_tab, tile_tab,
    )
```

```python
import numpy as np
import jax
import jax.numpy as jnp
from jax.experimental import pallas as pl
from jax.experimental.pallas import tpu as pltpu

D_MODEL = 1024
CHUNK = 64
EPS = 1e-6
ATT_HEADS = 16
ATT_KV_HEADS = 2
HEAD_DIM = 64
ATT_GROUP = ATT_HEADS // ATT_KV_HEADS
WINDOW_CHUNKS = 2
HALO = WINDOW_CHUNKS * CHUNK
KEY_BLOCK = HALO + CHUNK
ROPE_DIM = HEAD_DIM // 4
ROPE_HALF = ROPE_DIM // 2
ROPE_THETA = 500000.0
SG_BLOCK = 128
SG_GROUPS = 8
LANES = 128
PAIRS_PER_KV = ATT_GROUP // 2
STACK_ROWS = PAIRS_PER_KV * CHUNK
MASK_VALUE = -1e30

OFF_Q, OFF_K, OFF_V, OFF_ZA = 0, 1024, 1152, 1280
OFF_U, OFF_VS, OFF_ZS, OFF_GM = 2304, 3328, 4352, 5376
IN_WIDTH = 7424
N_SIDE = 4
SIDE_U, SIDE_VS, SIDE_ZS, SIDE_GA = range(N_SIDE)
SIDE_OFF = (OFF_U, OFF_VS, OFF_ZS, OFF_GM)
OFF_GS = OFF_GM + 1024
OFF_WAO, OFF_WSO, OFF_WO = 0, 1024, 2048
PAD_WIDTH = D_MODEL + LANES
OUT_W_WIDTH = 3 * D_MODEL + LANES
LOG2E = np.float32(1.4426950408889634)
LOAD_ROWS = 32
N_LOADS = D_MODEL // LOAD_ROWS
LOAD_SLOTS = 4
assert N_LOADS >= LOAD_SLOTS

SEQ_TILE = 512
ROW_BLOCK = 256
N_ROW_BLOCKS = SEQ_TILE // ROW_BLOCK
N_CHUNKS = SEQ_TILE // CHUNK
VMEM_LIMIT_BYTES = 58 * 1024 * 1024

assert N_CHUNKS == N_SIDE * N_ROW_BLOCKS


def _dot(a, b):
    return jnp.dot(a, b, preferred_element_type=jnp.float32)


def _sigmoid(x):
    return 0.5 * jnp.tanh(0.5 * x) + 0.5


def _silu(x):
    hx = 0.5 * x
    return hx * (1.0 + jnp.tanh(hx))


def _gelu_exact(x):
    return 0.5 * x * (1.0 + jax.lax.erf(x * np.float32(1.0 / np.sqrt(2.0))))


def _load_weights(w_in_hbm, w_out_hbm, wa_ref, wout_ref, stage_in, stage_out, sems):
    n_out = len(w_out_hbm)
    ahead = LOAD_SLOTS - 1

    def copies(i, slot):
        rows = pl.ds(pl.multiple_of(i * LOAD_ROWS, LOAD_ROWS), LOAD_ROWS)
        cps = [pltpu.make_async_copy(w_in_hbm.at[0, rows, :], stage_in.at[slot], sems.at[slot, 0])]
        for k in range(n_out):
            cps.append(pltpu.make_async_copy(w_out_hbm[k].at[0, rows, :], stage_out.at[slot, k],
                                             sems.at[slot, 1 + k]))
        return cps

    for i in range(ahead):
        for cp in copies(i, i):
            cp.start()

    def body(i, carry):
        slot = i % LOAD_SLOTS

        @pl.when(i + ahead < N_LOADS)
        def _():
            for cp in copies(i + ahead, (i + ahead) % LOAD_SLOTS):
                cp.start()

        for cp in copies(i, slot):
            cp.wait()
        rows = pl.ds(pl.multiple_of(i * LOAD_ROWS, LOAD_ROWS), LOAD_ROWS)
        wa_ref[rows, :] = stage_in[slot].astype(jnp.bfloat16)
        for k in range(n_out):
            wout_ref[rows, k * D_MODEL:(k + 1) * D_MODEL] = stage_out[slot, k].astype(jnp.bfloat16)
        return carry

    jax.lax.fori_loop(0, N_LOADS, body, 0)


def _block_kernel(x_ref, ng_ref, w_in_hbm, wao_hbm, wso_hbm, wo_hbm, bm_ref, sink_ref, sgw_ref, sgb_ref,
                  lng_ref, lnb_ref, fg_ref,
                  rrow_ref, rtile_ref,
                  out_ref,
                  h_scr, qt_scr, za_scr, k_scr, v_scr, vt_scr, ag_scr, sg_scr, vn_scr, uz_scr, ga_scr,
                  wa_ref, wout_ref, stage_in, stage_out, load_sems):
    t = pl.program_id(1)
    T = SEQ_TILE
    RB = ROW_BLOCK
    bf16 = jnp.bfloat16

    @pl.when((pl.program_id(0) == 0) & (t == 0))
    def _():
        _load_weights(w_in_hbm, (wao_hbm, wso_hbm, wo_hbm), wa_ref, wout_ref,
                      stage_in, stage_out, load_sems)

    @pl.when(t == 0)
    def _():
        k_scr[:, 0:HALO, :] = jnp.zeros((4, HALO, LANES), bf16)
        v_scr[0:HALO, :] = jnp.zeros((HALO, LANES), jnp.float32)

    @pl.when(t != 0)
    def _():
        k_scr[:, 0:HALO, :] = k_scr[:, T:T + HALO, :]
        v_scr[0:HALO, :] = v_scr[T:T + HALO, :]

    lane = jax.lax.broadcasted_iota(jnp.int32, (RB, LANES), 1)
    first_half = (lane % HEAD_DIM) < ROPE_HALF
    low_half = lane < HEAD_DIM
    scale = np.float32(HEAD_DIM ** -0.5) * LOG2E

    for rb in range(N_ROW_BLOCKS):
        rows = slice(rb * RB, (rb + 1) * RB)
        x = x_ref[0, rows, :]
        ms = jnp.mean(x * x, axis=-1, keepdims=True)
        h = (x * jax.lax.rsqrt(ms + EPS) * ng_ref[...]).astype(bf16)
        h_scr[rows, 0:D_MODEL] = h

        cb, sb, sb_sgn = rrow_ref[0, rows, :], rrow_ref[1, rows, :], rrow_ref[2, rows, :]
        ca, sa, sa_sgn = rtile_ref[t, 0:1, :], rtile_ref[t, 1:2, :], rtile_ref[t, 2:3, :]
        cos_t = ca * cb - sa * sb
        sin_t = sa_sgn * cb + ca * sb_sgn

        def rope(v):
            nxt = pltpu.roll(v, LANES - ROPE_HALF, 1)
            prv = pltpu.roll(v, ROPE_HALF, 1)
            return v * cos_t + jnp.where(first_half, nxt, prv) * sin_t

        kv2 = _dot(h, wa_ref[:, OFF_K:OFF_V + LANES])
        k2 = rope(kv2[:, 0:LANES])
        zeros = jnp.zeros_like(k2)
        krows = slice(HALO + rb * RB, HALO + (rb + 1) * RB)
        h0_lo = jnp.where(low_half, k2, zeros)
        h1_hi = jnp.where(low_half, zeros, k2)
        h0_hi = pltpu.roll(h0_lo, HEAD_DIM, 1)
        h1_lo = pltpu.roll(h1_hi, HEAD_DIM, 1)
        k_scr[0, krows, :] = h0_lo.astype(bf16)
        k_scr[1, krows, :] = h0_hi.astype(bf16)
        k_scr[2, krows, :] = h1_lo.astype(bf16)
        k_scr[3, krows, :] = h1_hi.astype(bf16)
        v_scr[krows, :] = kv2[:, LANES:2 * LANES]

        q = _dot(h, wa_ref[:, OFF_Q:OFF_Q + 1024])
        q_tiles = [rope(q[:, j * LANES:(j + 1) * LANES]) * scale for j in range(1024 // LANES)]
        for hk in range(ATT_KV_HEADS):
            for cc in range(RB // CHUNK):
                stacked = jnp.concatenate(
                    [q_tiles[hk * PAIRS_PER_KV + j][cc * CHUNK:(cc + 1) * CHUNK] for j in range(PAIRS_PER_KV)],
                    axis=0)
                qt_scr[hk, rb * (RB // CHUNK) + cc] = stacked.T.astype(bf16)
        za_scr[rows, :] = _silu(_dot(h, wa_ref[:, OFF_ZA:OFF_ZA + 1024]))

    for a in range(2):
        vt = v_scr[a * CHUNK:HALO + T, :].T
        for hk in range(ATT_KV_HEADS):
            vt_scr[a, hk, :, 0:HALO + T - a * CHUNK] = vt[hk * HEAD_DIM:(hk + 1) * HEAD_DIM].astype(bf16)

    key_off = jax.lax.broadcasted_iota(jnp.int32, (KEY_BLOCK, 1), 0)
    units = [(hk, par) for hk in range(ATT_KV_HEADS) for par in range(2)]
    for c in range(N_CHUNKS):
        r0 = c * CHUNK
        rows = slice(r0, r0 + CHUNK)
        keys = slice(r0, r0 + KEY_BLOCK)
        valid = (key_off + (t * T + r0 - HALO)) >= 0
        scores = []
        for hk, par in units:
            scores.append(_dot(k_scr[2 * hk + par, keys, :], qt_scr[hk, c]))

        side, rb = divmod(c, N_ROW_BLOCKS)
        srows = slice(rb * RB, (rb + 1) * RB)
        raw = _dot(h_scr[srows, 0:D_MODEL],
                   wa_ref[:, SIDE_OFF[side]:SIDE_OFF[side] + D_MODEL])
        if side == SIDE_U:
            uz_scr[srows, :] = _gelu_exact(raw)
        elif side == SIDE_VS:
            vs = _gelu_exact(raw)
            mu = jnp.mean(vs, axis=-1, keepdims=True)
            vc = vs - mu
            var = jnp.mean(vc * vc, axis=-1, keepdims=True)
            vn_scr[srows, 0:D_MODEL] = (vc * jax.lax.rsqrt(var + EPS) * lng_ref[...]
                                        + lnb_ref[...]).astype(bf16)
        elif side == SIDE_ZS:
            uz_scr[srows, :] = uz_scr[srows, :] * _silu(raw)
        else:
            ga_scr[srows, :] = _sigmoid(raw + bm_ref[:, 0:1024])

        align, lane_tile = c % 2, c // 2
        vkeys = slice(lane_tile * LANES, lane_tile * LANES + KEY_BLOCK)
        outs_t = []
        for (hk, par), s in zip(units, scores):
            if r0 < HALO:
                s = jnp.where(valid, s, MASK_VALUE)
            sink = sink_ref[2 * hk + par, 0:1, :] * LOG2E
            m = jnp.maximum(jnp.max(s, axis=0, keepdims=True), sink)
            p = jnp.exp2(s - m)
            denom = jnp.sum(p, axis=0, keepdims=True) + jnp.exp2(sink - m)
            ot = _dot(vt_scr[align, hk, :, vkeys], p.astype(bf16))
            outs_t.append(ot * (1.0 / denom))
        for hk in range(ATT_KV_HEADS):
            o = jnp.concatenate([outs_t[2 * hk], outs_t[2 * hk + 1]], axis=0).T
            for j in range(PAIRS_PER_KV):
                c0 = (hk * PAIRS_PER_KV + j) * LANES
                gated = o[j * CHUNK:(j + 1) * CHUNK] * za_scr[rows, c0:c0 + LANES]
                ag_scr[rows, c0:c0 + LANES] = gated.astype(bf16)

    pos_i = jax.lax.broadcasted_iota(jnp.int32, (SG_BLOCK, SG_BLOCK), 0) // CHUNK
    pos_j = jax.lax.broadcasted_iota(jnp.int32, (SG_BLOCK, SG_BLOCK), 1) // CHUNK
    causal = pos_j <= pos_i
    nblk = T // SG_BLOCK
    for g in range(SG_GROUPS):
        wg = jnp.where(causal, sgw_ref[g], 0.0).astype(bf16)
        c0 = g * LANES
        rhs = jnp.concatenate(
            [vn_scr[b * SG_BLOCK:(b + 1) * SG_BLOCK, c0:c0 + LANES] for b in range(nblk)], axis=1)
        mixed = _dot(wg, rhs)
        for b in range(nblk):
            brows = slice(b * SG_BLOCK, (b + 1) * SG_BLOCK)
            blk = mixed[:, b * LANES:(b + 1) * LANES] + sgb_ref[g]
            sg_scr[brows, c0:c0 + LANES] = (uz_scr[brows, c0:c0 + LANES] * blk).astype(bf16)

    for rb in range(N_ROW_BLOCKS):
        rows = slice(rb * RB, (rb + 1) * RB)
        y = ga_scr[rows, :] * _dot(ag_scr[rows, 0:D_MODEL], wout_ref[:, OFF_WAO:OFF_WAO + D_MODEL])
        gs = _sigmoid(_dot(h_scr[rows, 0:D_MODEL], wa_ref[:, OFF_GS:OFF_GS + D_MODEL])
                      + bm_ref[:, 1024:2048])
        y = y + gs * _dot(sg_scr[rows, 0:D_MODEL], wout_ref[:, OFF_WSO:OFF_WSO + D_MODEL])
        xo = x_ref[0, rows, :] + _dot(y.astype(bf16), wout_ref[:, OFF_WO:OFF_WO + D_MODEL])
        ms2 = jnp.mean(xo * xo, axis=-1, keepdims=True)
        out_ref[0, rows, :] = xo * jax.lax.rsqrt(ms2 + EPS) * fg_ref[...]


def _rope_tables(seq, tile):
    lane = np.arange(LANES) % HEAD_DIM
    inv_freq = ROPE_THETA ** (-(np.arange(ROPE_HALF, dtype=np.float64) * 2.0) / ROPE_DIM)
    freq = np.where(lane < ROPE_DIM, inv_freq[lane % ROPE_HALF], 0.0)
    sign = np.where(lane < ROPE_HALF, -1.0, 1.0)

    def tables(pos):
        ang = pos.astype(np.float64)[:, None] * freq[None, :]
        return np.stack([np.cos(ang), np.sin(ang), np.sin(ang) * sign], axis=0)

    row_tab = tables(np.arange(tile))
    tile_tab = np.zeros((seq // tile, 8, LANES))
    tile_tab[:, 0:3, :] = np.transpose(tables(np.arange(seq // tile) * tile), (1, 0, 2))
    return jnp.asarray(row_tab, jnp.float32), jnp.asarray(tile_tab, jnp.float32)


@jax.jit
def kernel(x, norm_g, w_in, b_merge, att_sinks, sg_w, sg_b, sg_ln_g, sg_ln_b,
           w_att_out, w_sg_out, w_o, final_g):
    B, S, D = x.shape
    T = SEQ_TILE
    assert D == D_MODEL and S % T == 0 and w_in.shape == (1, D, IN_WIDTH)
    bf16 = jnp.bfloat16
    f32 = jnp.float32

    row_tab, tile_tab = _rope_tables(S, T)
    sinks = att_sinks[0].astype(f32).reshape(ATT_KV_HEADS, PAIRS_PER_KV, 2)
    sink_rows = jnp.repeat(jnp.transpose(sinks, (0, 2, 1)).reshape(4, PAIRS_PER_KV), CHUNK, axis=1)
    sink_rows = jnp.broadcast_to(sink_rows[:, None, :], (4, 8, STACK_ROWS))
    sgb = jnp.broadcast_to(sg_b[0].astype(f32)[:, :, None], (SG_GROUPS, SG_BLOCK, LANES))

    hbm = pl.BlockSpec(memory_space=pl.ANY)

    def const(shape):
        zeros = (0,) * len(shape)
        return pl.BlockSpec(shape, lambda b, t: zeros, pipeline_mode=pl.Buffered(1))

    grid_spec = pltpu.PrefetchScalarGridSpec(
        num_scalar_prefetch=0,
        grid=(B, S // T),
        in_specs=[
            pl.BlockSpec((1, T, D), lambda b, t: (b, t, 0)),
            const((1, D)),
            hbm, hbm, hbm, hbm,
            const((1, 2 * D)),
            const((4, 8, STACK_ROWS)),
            const((SG_GROUPS, SG_BLOCK, SG_BLOCK)),
            const((SG_GROUPS, SG_BLOCK, LANES)),
            const((1, D)),
            const((1, D)),
            const((1, D)),
            const((3, T, LANES)),
            const((S // T, 8, LANES)),
        ],
        out_specs=pl.BlockSpec((1, T, D), lambda b, t: (b, t, 0)),
        scratch_shapes=[
            pltpu.VMEM((T, PAD_WIDTH), bf16),
            pltpu.VMEM((ATT_KV_HEADS, N_CHUNKS, LANES, STACK_ROWS), bf16),
            pltpu.VMEM((T, D), f32),
            pltpu.VMEM((4, HALO + T, LANES), bf16),
            pltpu.VMEM((HALO + T, LANES), f32),
            pltpu.VMEM((2, ATT_KV_HEADS, HEAD_DIM, HALO + T), bf16),
            pltpu.VMEM((T, PAD_WIDTH), bf16),
            pltpu.VMEM((T, PAD_WIDTH), bf16),
            pltpu.VMEM((T, PAD_WIDTH), bf16),
            pltpu.VMEM((T, D), f32),
            pltpu.VMEM((T, D), f32),
            pltpu.VMEM((D, IN_WIDTH), bf16),
            pltpu.VMEM((D, OUT_W_WIDTH), bf16),
            pltpu.VMEM((LOAD_SLOTS, LOAD_ROWS, IN_WIDTH), f32),
            pltpu.VMEM((LOAD_SLOTS, 3, LOAD_ROWS, D), f32),
            pltpu.SemaphoreType.DMA((LOAD_SLOTS, 4)),
        ],
    )
    return pl.pallas_call(
        _block_kernel,
        grid_spec=grid_spec,
        out_shape=jax.ShapeDtypeStruct((B, S, D), x.dtype),
        compiler_params=pltpu.CompilerParams(
            dimension_semantics=("arbitrary", "arbitrary"),
            vmem_limit_bytes=VMEM_LIMIT_BYTES,
        ),
        name="hybrid_block",
    )(
        x, norm_g.astype(f32), w_in.astype(f32), w_att_out.astype(f32), w_sg_out.astype(f32),
        w_o.astype(f32), b_merge.astype(f32), sink_rows,
        sg_w[0].astype(f32), sgb, sg_ln_g.astype(f32), sg_ln_b.astype(f32),
        final_g.reshape(1, D).astype(f32), row_tab, tile_tab,
    )
```

```python
import numpy as np
import jax
import jax.numpy as jnp
from jax.experimental import pallas as pl
from jax.experimental.pallas import tpu as pltpu

D_MODEL = 1024
CHUNK = 64
EPS = 1e-6
ATT_HEADS = 16
ATT_KV_HEADS = 2
HEAD_DIM = 64
ATT_GROUP = ATT_HEADS // ATT_KV_HEADS
WINDOW_CHUNKS = 2
HALO = WINDOW_CHUNKS * CHUNK
KEY_BLOCK = HALO + CHUNK
ROPE_DIM = HEAD_DIM // 4
ROPE_HALF = ROPE_DIM // 2
ROPE_THETA = 500000.0
SG_BLOCK = 128
SG_GROUPS = 8
LANES = 128
PAIRS_PER_KV = ATT_GROUP // 2
STACK_ROWS = PAIRS_PER_KV * CHUNK
MASK_VALUE = -1e30

OFF_Q, OFF_K, OFF_V, OFF_ZA = 0, 1024, 1152, 1280
OFF_U, OFF_VS, OFF_ZS, OFF_GM = 2304, 3328, 4352, 5376
IN_WIDTH = 7424
N_SIDE = 4
SIDE_U, SIDE_VS, SIDE_ZS, SIDE_GA = range(N_SIDE)
SIDE_OFF = (OFF_U, OFF_VS, OFF_ZS, OFF_GM)
OFF_GS = OFF_GM + 1024
OFF_WAO, OFF_WSO, OFF_WO = 0, 1024, 2048
PAD_WIDTH = D_MODEL + LANES
OUT_W_WIDTH = 3 * D_MODEL + LANES
LOG2E = np.float32(1.4426950408889634)
LOAD_ROWS = 32
N_LOADS = D_MODEL // LOAD_ROWS
LOAD_SLOTS = 6
LOAD_COPIES = 5
assert N_LOADS >= LOAD_SLOTS

SEQ_TILE = 512
ROW_BLOCK = 256
N_ROW_BLOCKS = SEQ_TILE // ROW_BLOCK
N_CHUNKS = SEQ_TILE // CHUNK
VMEM_LIMIT_BYTES = 58 * 1024 * 1024

assert N_CHUNKS == N_SIDE * N_ROW_BLOCKS


def _dot(a, b):
    return jnp.dot(a, b, preferred_element_type=jnp.float32)


def _sigmoid(x):
    return 0.5 * jnp.tanh(0.5 * x) + 0.5


def _silu(x):
    hx = 0.5 * x
    return hx * (1.0 + jnp.tanh(hx))


def _gelu_exact(x):
    return 0.5 * x * (1.0 + jax.lax.erf(x * np.float32(1.0 / np.sqrt(2.0))))


def _load_weights(w_in_hbm, w_out_hbm, wa_ref, wout_ref, stage_in, stage_out, sems):
    n_out = len(w_out_hbm)
    assert 2 + n_out == LOAD_COPIES
    ahead = LOAD_SLOTS - 1

    half = IN_WIDTH // 2

    def copies(i, slot):
        rows = pl.ds(pl.multiple_of(i * LOAD_ROWS, LOAD_ROWS), LOAD_ROWS)
        cps = [pltpu.make_async_copy(w_in_hbm.at[0, rows, pl.ds(j * half, half)],
                                     stage_in.at[slot, :, pl.ds(j * half, half)], sems.at[slot, j])
               for j in range(2)]
        for k in range(n_out):
            cps.append(pltpu.make_async_copy(w_out_hbm[k].at[0, rows, :], stage_out.at[slot, k],
                                             sems.at[slot, 2 + k]))
        return cps

    def start_all(cps):
        for n, cp in enumerate(cps):
            cp.start(priority=n % 2)

    for i in range(ahead):
        start_all(copies(i, i))

    def body(i, carry):
        slot = i % LOAD_SLOTS

        @pl.when(i + ahead < N_LOADS)
        def _():
            start_all(copies(i + ahead, (i + ahead) % LOAD_SLOTS))

        for cp in copies(i, slot):
            cp.wait()
        rows = pl.ds(pl.multiple_of(i * LOAD_ROWS, LOAD_ROWS), LOAD_ROWS)
        wa_ref[rows, :] = stage_in[slot].astype(jnp.bfloat16)
        for k in range(n_out):
            wout_ref[rows, k * D_MODEL:(k + 1) * D_MODEL] = stage_out[slot, k].astype(jnp.bfloat16)
        return carry

    jax.lax.fori_loop(0, N_LOADS, body, 0)


def _block_kernel(x_ref, ng_ref, w_in_hbm, wao_hbm, wso_hbm, wo_hbm, bm_ref, sink_ref, sgw_ref, sgb_ref,
                  lng_ref, lnb_ref, fg_ref,
                  rrow_ref, rtile_ref,
                  out_ref,
                  h_scr, qt_scr, za_scr, k_scr, v_scr, vt_scr, ag_scr, sg_scr, vn_scr, uz_scr, ga_scr,
                  wa_ref, wout_ref, stage_in, stage_out, load_sems):
    t = pl.program_id(1)
    T = SEQ_TILE
    RB = ROW_BLOCK
    bf16 = jnp.bfloat16

    @pl.when((pl.program_id(0) == 0) & (t == 0))
    def _():
        _load_weights(w_in_hbm, (wao_hbm, wso_hbm, wo_hbm), wa_ref, wout_ref,
                      stage_in, stage_out, load_sems)

    @pl.when(t == 0)
    def _():
        k_scr[:, 0:HALO, :] = jnp.zeros((4, HALO, LANES), bf16)
        v_scr[0:HALO, :] = jnp.zeros((HALO, LANES), jnp.float32)

    @pl.when(t != 0)
    def _():
        k_scr[:, 0:HALO, :] = k_scr[:, T:T + HALO, :]
        v_scr[0:HALO, :] = v_scr[T:T + HALO, :]

    lane = jax.lax.broadcasted_iota(jnp.int32, (RB, LANES), 1)
    first_half = (lane % HEAD_DIM) < ROPE_HALF
    low_half = lane < HEAD_DIM
    scale = np.float32(HEAD_DIM ** -0.5) * LOG2E

    for rb in range(N_ROW_BLOCKS):
        rows = slice(rb * RB, (rb + 1) * RB)
        x = x_ref[0, rows, :]
        ms = jnp.mean(x * x, axis=-1, keepdims=True)
        h = (x * jax.lax.rsqrt(ms + EPS) * ng_ref[...]).astype(bf16)
        h_scr[rows, 0:D_MODEL] = h

        cb, sb, sb_sgn = rrow_ref[0, rows, :], rrow_ref[1, rows, :], rrow_ref[2, rows, :]
        ca, sa, sa_sgn = rtile_ref[t, 0:1, :], rtile_ref[t, 1:2, :], rtile_ref[t, 2:3, :]
        cos_t = ca * cb - sa * sb
        sin_t = sa_sgn * cb + ca * sb_sgn

        def rope(v):
            nxt = pltpu.roll(v, LANES - ROPE_HALF, 1)
            prv = pltpu.roll(v, ROPE_HALF, 1)
            return v * cos_t + jnp.where(first_half, nxt, prv) * sin_t

        kv2 = _dot(h, wa_ref[:, OFF_K:OFF_V + LANES])
        k2 = rope(kv2[:, 0:LANES])
        zeros = jnp.zeros_like(k2)
        krows = slice(HALO + rb * RB, HALO + (rb + 1) * RB)
        h0_lo = jnp.where(low_half, k2, zeros)
        h1_hi = jnp.where(low_half, zeros, k2)
        h0_hi = pltpu.roll(h0_lo, HEAD_DIM, 1)
        h1_lo = pltpu.roll(h1_hi, HEAD_DIM, 1)
        k_scr[0, krows, :] = h0_lo.astype(bf16)
        k_scr[1, krows, :] = h0_hi.astype(bf16)
        k_scr[2, krows, :] = h1_lo.astype(bf16)
        k_scr[3, krows, :] = h1_hi.astype(bf16)
        v_scr[krows, :] = kv2[:, LANES:2 * LANES]

        q = _dot(h, wa_ref[:, OFF_Q:OFF_Q + 1024])
        q_tiles = [rope(q[:, j * LANES:(j + 1) * LANES]) * scale for j in range(1024 // LANES)]
        for hk in range(ATT_KV_HEADS):
            for cc in range(RB // CHUNK):
                stacked = jnp.concatenate(
                    [q_tiles[hk * PAIRS_PER_KV + j][cc * CHUNK:(cc + 1) * CHUNK] for j in range(PAIRS_PER_KV)],
                    axis=0)
                qt_scr[hk, rb * (RB // CHUNK) + cc] = stacked.T.astype(bf16)
        za_scr[rows, :] = _silu(_dot(h, wa_ref[:, OFF_ZA:OFF_ZA + 1024]))

    for a in range(2):
        vt = v_scr[a * CHUNK:HALO + T, :].T
        for hk in range(ATT_KV_HEADS):
            vt_scr[a, hk, :, 0:HALO + T - a * CHUNK] = vt[hk * HEAD_DIM:(hk + 1) * HEAD_DIM].astype(bf16)

    key_off = jax.lax.broadcasted_iota(jnp.int32, (KEY_BLOCK, 1), 0)
    units = [(hk, par) for hk in range(ATT_KV_HEADS) for par in range(2)]
    for c in range(N_CHUNKS):
        r0 = c * CHUNK
        rows = slice(r0, r0 + CHUNK)
        keys = slice(r0, r0 + KEY_BLOCK)
        valid = (key_off + (t * T + r0 - HALO)) >= 0
        scores = []
        for hk, par in units:
            scores.append(_dot(k_scr[2 * hk + par, keys, :], qt_scr[hk, c]))

        side, rb = divmod(c, N_ROW_BLOCKS)
        srows = slice(rb * RB, (rb + 1) * RB)
        raw = _dot(h_scr[srows, 0:D_MODEL],
                   wa_ref[:, SIDE_OFF[side]:SIDE_OFF[side] + D_MODEL])
        if side == SIDE_U:
            uz_scr[srows, :] = _gelu_exact(raw)
        elif side == SIDE_VS:
            vs = _gelu_exact(raw)
            mu = jnp.mean(vs, axis=-1, keepdims=True)
            vc = vs - mu
            var = jnp.mean(vc * vc, axis=-1, keepdims=True)
            vn_scr[srows, 0:D_MODEL] = (vc * jax.lax.rsqrt(var + EPS) * lng_ref[...]
                                        + lnb_ref[...]).astype(bf16)
        elif side == SIDE_ZS:
            uz_scr[srows, :] = uz_scr[srows, :] * _silu(raw)
        else:
            ga_scr[srows, :] = _sigmoid(raw + bm_ref[:, 0:1024])

        align, lane_tile = c % 2, c // 2
        vkeys = slice(lane_tile * LANES, lane_tile * LANES + KEY_BLOCK)
        outs_t = []
        for (hk, par), s in zip(units, scores):
            if r0 < HALO:
                s = jnp.where(valid, s, MASK_VALUE)
            sink = sink_ref[2 * hk + par, 0:1, :] * LOG2E
            m = jnp.maximum(jnp.max(s, axis=0, keepdims=True), sink)
            p = jnp.exp2(s - m)
            denom = jnp.sum(p, axis=0, keepdims=True) + jnp.exp2(sink - m)
            ot = _dot(vt_scr[align, hk, :, vkeys], p.astype(bf16))
            outs_t.append(ot * (1.0 / denom))
        for hk in range(ATT_KV_HEADS):
            o = jnp.concatenate([outs_t[2 * hk], outs_t[2 * hk + 1]], axis=0).T
            for j in range(PAIRS_PER_KV):
                c0 = (hk * PAIRS_PER_KV + j) * LANES
                gated = o[j * CHUNK:(j + 1) * CHUNK] * za_scr[rows, c0:c0 + LANES]
                ag_scr[rows, c0:c0 + LANES] = gated.astype(bf16)

    pos_i = jax.lax.broadcasted_iota(jnp.int32, (SG_BLOCK, SG_BLOCK), 0) // CHUNK
    pos_j = jax.lax.broadcasted_iota(jnp.int32, (SG_BLOCK, SG_BLOCK), 1) // CHUNK
    causal = pos_j <= pos_i
    nblk = T // SG_BLOCK
    for g in range(SG_GROUPS):
        wg = jnp.where(causal, sgw_ref[g], 0.0).astype(bf16)
        c0 = g * LANES
        rhs = jnp.concatenate(
            [vn_scr[b * SG_BLOCK:(b + 1) * SG_BLOCK, c0:c0 + LANES] for b in range(nblk)], axis=1)
        mixed = _dot(wg, rhs)
        for b in range(nblk):
            brows = slice(b * SG_BLOCK, (b + 1) * SG_BLOCK)
            blk = mixed[:, b * LANES:(b + 1) * LANES] + sgb_ref[g]
            sg_scr[brows, c0:c0 + LANES] = (uz_scr[brows, c0:c0 + LANES] * blk).astype(bf16)

    for rb in range(N_ROW_BLOCKS):
        rows = slice(rb * RB, (rb + 1) * RB)
        y = ga_scr[rows, :] * _dot(ag_scr[rows, 0:D_MODEL], wout_ref[:, OFF_WAO:OFF_WAO + D_MODEL])
        gs = _sigmoid(_dot(h_scr[rows, 0:D_MODEL], wa_ref[:, OFF_GS:OFF_GS + D_MODEL])
                      + bm_ref[:, 1024:2048])
        y = y + gs * _dot(sg_scr[rows, 0:D_MODEL], wout_ref[:, OFF_WSO:OFF_WSO + D_MODEL])
        xo = x_ref[0, rows, :] + _dot(y.astype(bf16), wout_ref[:, OFF_WO:OFF_WO + D_MODEL])
        ms2 = jnp.mean(xo * xo, axis=-1, keepdims=True)
        out_ref[0, rows, :] = xo * jax.lax.rsqrt(ms2 + EPS) * fg_ref[...]


def _rope_tables(seq, tile):
    lane = np.arange(LANES) % HEAD_DIM
    inv_freq = ROPE_THETA ** (-(np.arange(ROPE_HALF, dtype=np.float64) * 2.0) / ROPE_DIM)
    freq = np.where(lane < ROPE_DIM, inv_freq[lane % ROPE_HALF], 0.0)
    sign = np.where(lane < ROPE_HALF, -1.0, 1.0)

    def tables(pos):
        ang = pos.astype(np.float64)[:, None] * freq[None, :]
        return np.stack([np.cos(ang), np.sin(ang), np.sin(ang) * sign], axis=0)

    row_tab = tables(np.arange(tile))
    tile_tab = np.zeros((seq // tile, 8, LANES))
    tile_tab[:, 0:3, :] = np.transpose(tables(np.arange(seq // tile) * tile), (1, 0, 2))
    return jnp.asarray(row_tab, jnp.float32), jnp.asarray(tile_tab, jnp.float32)


@jax.jit
def kernel(x, norm_g, w_in, b_merge, att_sinks, sg_w, sg_b, sg_ln_g, sg_ln_b,
           w_att_out, w_sg_out, w_o, final_g):
    B, S, D = x.shape
    T = SEQ_TILE
    assert D == D_MODEL and S % T == 0 and w_in.shape == (1, D, IN_WIDTH)
    bf16 = jnp.bfloat16
    f32 = jnp.float32

    row_tab, tile_tab = _rope_tables(S, T)
    sinks = att_sinks[0].astype(f32).reshape(ATT_KV_HEADS, PAIRS_PER_KV, 2)
    sink_rows = jnp.repeat(jnp.transpose(sinks, (0, 2, 1)).reshape(4, PAIRS_PER_KV), CHUNK, axis=1)
    sink_rows = jnp.broadcast_to(sink_rows[:, None, :], (4, 8, STACK_ROWS))
    sgb = jnp.broadcast_to(sg_b[0].astype(f32)[:, :, None], (SG_GROUPS, SG_BLOCK, LANES))

    hbm = pl.BlockSpec(memory_space=pl.ANY)

    def const(shape):
        zeros = (0,) * len(shape)
        return pl.BlockSpec(shape, lambda b, t: zeros, pipeline_mode=pl.Buffered(1))

    grid_spec = pltpu.PrefetchScalarGridSpec(
        num_scalar_prefetch=0,
        grid=(B, S // T),
        in_specs=[
            pl.BlockSpec((1, T, D), lambda b, t: (b, t, 0)),
            const((1, D)),
            hbm, hbm, hbm, hbm,
            const((1, 2 * D)),
            const((4, 8, STACK_ROWS)),
            const((SG_GROUPS, SG_BLOCK, SG_BLOCK)),
            const((SG_GROUPS, SG_BLOCK, LANES)),
            const((1, D)),
            const((1, D)),
            const((1, D)),
            const((3, T, LANES)),
            const((S // T, 8, LANES)),
        ],
        out_specs=pl.BlockSpec((1, T, D), lambda b, t: (b, t, 0)),
        scratch_shapes=[
            pltpu.VMEM((T, PAD_WIDTH), bf16),
            pltpu.VMEM((ATT_KV_HEADS, N_CHUNKS, LANES, STACK_ROWS), bf16),
            pltpu.VMEM((T, D), f32),
            pltpu.VMEM((4, HALO + T, LANES), bf16),
            pltpu.VMEM((HALO + T, LANES), f32),
            pltpu.VMEM((2, ATT_KV_HEADS, HEAD_DIM, HALO + T), bf16),
            pltpu.VMEM((T, PAD_WIDTH), bf16),
            pltpu.VMEM((T, PAD_WIDTH), bf16),
            pltpu.VMEM((T, PAD_WIDTH), bf16),
            pltpu.VMEM((T, D), f32),
            pltpu.VMEM((T, D), f32),
            pltpu.VMEM((D, IN_WIDTH), bf16),
            pltpu.VMEM((D, OUT_W_WIDTH), bf16),
            pltpu.VMEM((LOAD_SLOTS, LOAD_ROWS, IN_WIDTH), f32),
            pltpu.VMEM((LOAD_SLOTS, 3, LOAD_ROWS, D), f32),
            pltpu.SemaphoreType.DMA((LOAD_SLOTS, LOAD_COPIES)),
        ],
    )
    return pl.pallas_call(
        _block_kernel,
        grid_spec=grid_spec,
        out_shape=jax.ShapeDtypeStruct((B, S, D), x.dtype),
        compiler_params=pltpu.CompilerParams(
            dimension_semantics=("arbitrary", "arbitrary"),
            vmem_limit_bytes=VMEM_LIMIT_BYTES,
        ),
        name="hybrid_block",
    )(
        x, norm_g.astype(f32), w_in.astype(f32), w_att_out.astype(f32), w_sg_out.astype(f32),
        w_o.astype(f32), b_merge.astype(f32), sink_rows,
        sg_w[0].astype(f32), sgb, sg_ln_g.astype(f32), sg_ln_b.astype(f32),
        final_g.reshape(1, D).astype(f32), row_tab, tile_tab,
    )
```

```python
import numpy as np
import jax
import jax.numpy as jnp
from jax.experimental import pallas as pl
from jax.experimental.pallas import tpu as pltpu

D_MODEL = 1024
CHUNK = 64
EPS = 1e-6
ATT_HEADS = 16
ATT_KV_HEADS = 2
HEAD_DIM = 64
ATT_GROUP = ATT_HEADS // ATT_KV_HEADS
WINDOW_CHUNKS = 2
HALO = WINDOW_CHUNKS * CHUNK
KEY_BLOCK = HALO + CHUNK
ROPE_DIM = HEAD_DIM // 4
ROPE_HALF = ROPE_DIM // 2
ROPE_THETA = 500000.0
SG_BLOCK = 128
SG_GROUPS = 8
LANES = 128
PAIRS_PER_KV = ATT_GROUP // 2
STACK_ROWS = PAIRS_PER_KV * CHUNK
MASK_VALUE = -1e30

OFF_Q, OFF_K, OFF_V, OFF_ZA = 0, 1024, 1152, 1280
OFF_U, OFF_VS, OFF_ZS, OFF_GM = 2304, 3328, 4352, 5376
IN_WIDTH = 7424
N_SIDE = 4
SIDE_U, SIDE_VS, SIDE_ZS, SIDE_GA = range(N_SIDE)
SIDE_OFF = (OFF_U, OFF_VS, OFF_ZS, OFF_GM)
OFF_GS = OFF_GM + 1024
OFF_WAO, OFF_WSO, OFF_WO = 0, 1024, 2048
PAD_WIDTH = D_MODEL + LANES
OUT_W_WIDTH = 3 * D_MODEL + LANES
LOG2E = np.float32(1.4426950408889634)
LOAD_ROWS = 64
N_LOADS = D_MODEL // LOAD_ROWS
LOAD_SLOTS = 3
assert N_LOADS >= LOAD_SLOTS

SEQ_TILE = 512
ROW_BLOCK = 256
N_ROW_BLOCKS = SEQ_TILE // ROW_BLOCK
N_CHUNKS = SEQ_TILE // CHUNK
VMEM_LIMIT_BYTES = 58 * 1024 * 1024

assert N_CHUNKS == N_SIDE * N_ROW_BLOCKS


def _dot(a, b):
    return jnp.dot(a, b, preferred_element_type=jnp.float32)


def _sigmoid(x):
    return 0.5 * jnp.tanh(0.5 * x) + 0.5


def _silu(x):
    hx = 0.5 * x
    return hx * (1.0 + jnp.tanh(hx))


def _gelu_exact(x):
    return 0.5 * x * (1.0 + jax.lax.erf(x * np.float32(1.0 / np.sqrt(2.0))))


def _load_weights(w_in_hbm, w_out_hbm, wa_ref, wout_ref, stage_in, stage_out, sems):
    n_out = len(w_out_hbm)
    ahead = LOAD_SLOTS - 1

    def copies(i, slot):
        rows = pl.ds(pl.multiple_of(i * LOAD_ROWS, LOAD_ROWS), LOAD_ROWS)
        cps = [pltpu.make_async_copy(w_in_hbm.at[0, rows, :], stage_in.at[slot], sems.at[slot, 0])]
        for k in range(n_out):
            cps.append(pltpu.make_async_copy(w_out_hbm[k].at[0, rows, :], stage_out.at[slot, k],
                                             sems.at[slot, 1 + k]))
        return cps

    for i in range(ahead):
        for cp in copies(i, i):
            cp.start()

    def body(i, carry):
        slot = i % LOAD_SLOTS

        @pl.when(i + ahead < N_LOADS)
        def _():
            for cp in copies(i + ahead, (i + ahead) % LOAD_SLOTS):
                cp.start()

        for cp in copies(i, slot):
            cp.wait()
        rows = pl.ds(pl.multiple_of(i * LOAD_ROWS, LOAD_ROWS), LOAD_ROWS)
        wa_ref[rows, :] = stage_in[slot].astype(jnp.bfloat16)
        for k in range(n_out):
            wout_ref[rows, k * D_MODEL:(k + 1) * D_MODEL] = stage_out[slot, k].astype(jnp.bfloat16)
        return carry

    jax.lax.fori_loop(0, N_LOADS, body, 0)


def _block_kernel(x_ref, ng_ref, w_in_hbm, wao_hbm, wso_hbm, wo_hbm, bm_ref, sink_ref, sgw_ref, sgb_ref,
                  lng_ref, lnb_ref, fg_ref,
                  rrow_ref, rtile_ref,
                  out_ref,
                  h_scr, qt_scr, za_scr, k_scr, v_scr, vt_scr, ag_scr, sg_scr, vn_scr, uz_scr, ga_scr,
                  wa_ref, wout_ref, stage_in, stage_out, load_sems):
    t = pl.program_id(1)
    T = SEQ_TILE
    RB = ROW_BLOCK
    bf16 = jnp.bfloat16

    @pl.when((pl.program_id(0) == 0) & (t == 0))
    def _():
        _load_weights(w_in_hbm, (wao_hbm, wso_hbm, wo_hbm), wa_ref, wout_ref,
                      stage_in, stage_out, load_sems)

    @pl.when(t == 0)
    def _():
        k_scr[:, 0:HALO, :] = jnp.zeros((4, HALO, LANES), bf16)
        v_scr[0:HALO, :] = jnp.zeros((HALO, LANES), jnp.float32)

    @pl.when(t != 0)
    def _():
        k_scr[:, 0:HALO, :] = k_scr[:, T:T + HALO, :]
        v_scr[0:HALO, :] = v_scr[T:T + HALO, :]

    lane = jax.lax.broadcasted_iota(jnp.int32, (RB, LANES), 1)
    first_half = (lane % HEAD_DIM) < ROPE_HALF
    low_half = lane < HEAD_DIM
    scale = np.float32(HEAD_DIM ** -0.5) * LOG2E

    for rb in range(N_ROW_BLOCKS):
        rows = slice(rb * RB, (rb + 1) * RB)
        x = x_ref[0, rows, :]
        ms = jnp.mean(x * x, axis=-1, keepdims=True)
        h = (x * jax.lax.rsqrt(ms + EPS) * ng_ref[...]).astype(bf16)
        h_scr[rows, 0:D_MODEL] = h

        cb, sb, sb_sgn = rrow_ref[0, rows, :], rrow_ref[1, rows, :], rrow_ref[2, rows, :]
        ca, sa, sa_sgn = rtile_ref[t, 0:1, :], rtile_ref[t, 1:2, :], rtile_ref[t, 2:3, :]
        cos_t = ca * cb - sa * sb
        sin_t = sa_sgn * cb + ca * sb_sgn

        def rope(v):
            nxt = pltpu.roll(v, LANES - ROPE_HALF, 1)
            prv = pltpu.roll(v, ROPE_HALF, 1)
            return v * cos_t + jnp.where(first_half, nxt, prv) * sin_t

        kv2 = _dot(h, wa_ref[:, OFF_K:OFF_V + LANES])
        k2 = rope(kv2[:, 0:LANES])
        zeros = jnp.zeros_like(k2)
        krows = slice(HALO + rb * RB, HALO + (rb + 1) * RB)
        h0_lo = jnp.where(low_half, k2, zeros)
        h1_hi = jnp.where(low_half, zeros, k2)
        h0_hi = pltpu.roll(h0_lo, HEAD_DIM, 1)
        h1_lo = pltpu.roll(h1_hi, HEAD_DIM, 1)
        k_scr[0, krows, :] = h0_lo.astype(bf16)
        k_scr[1, krows, :] = h0_hi.astype(bf16)
        k_scr[2, krows, :] = h1_lo.astype(bf16)
        k_scr[3, krows, :] = h1_hi.astype(bf16)
        v_scr[krows, :] = kv2[:, LANES:2 * LANES]

        q = _dot(h, wa_ref[:, OFF_Q:OFF_Q + 1024])
        q_tiles = [rope(q[:, j * LANES:(j + 1) * LANES]) * scale for j in range(1024 // LANES)]
        for hk in range(ATT_KV_HEADS):
            for cc in range(RB // CHUNK):
                stacked = jnp.concatenate(
                    [q_tiles[hk * PAIRS_PER_KV + j][cc * CHUNK:(cc + 1) * CHUNK] for j in range(PAIRS_PER_KV)],
                    axis=0)
                qt_scr[hk, rb * (RB // CHUNK) + cc] = stacked.T.astype(bf16)
        za_scr[rows, :] = _silu(_dot(h, wa_ref[:, OFF_ZA:OFF_ZA + 1024]))

    for a in range(2):
        vt = v_scr[a * CHUNK:HALO + T, :].T
        for hk in range(ATT_KV_HEADS):
            vt_scr[a, hk, :, 0:HALO + T - a * CHUNK] = vt[hk * HEAD_DIM:(hk + 1) * HEAD_DIM].astype(bf16)

    key_off = jax.lax.broadcasted_iota(jnp.int32, (KEY_BLOCK, 1), 0)
    units = [(hk, par) for hk in range(ATT_KV_HEADS) for par in range(2)]
    for c in range(N_CHUNKS):
        r0 = c * CHUNK
        rows = slice(r0, r0 + CHUNK)
        keys = slice(r0, r0 + KEY_BLOCK)
        valid = (key_off + (t * T + r0 - HALO)) >= 0
        scores = []
        for hk, par in units:
            scores.append(_dot(k_scr[2 * hk + par, keys, :], qt_scr[hk, c]))

        side, rb = divmod(c, N_ROW_BLOCKS)
        srows = slice(rb * RB, (rb + 1) * RB)
        raw = _dot(h_scr[srows, 0:D_MODEL],
                   wa_ref[:, SIDE_OFF[side]:SIDE_OFF[side] + D_MODEL])
        if side == SIDE_U:
            uz_scr[srows, :] = _gelu_exact(raw)
        elif side == SIDE_VS:
            vs = _gelu_exact(raw)
            mu = jnp.mean(vs, axis=-1, keepdims=True)
            vc = vs - mu
            var = jnp.mean(vc * vc, axis=-1, keepdims=True)
            vn_scr[srows, 0:D_MODEL] = (vc * jax.lax.rsqrt(var + EPS) * lng_ref[...]
                                        + lnb_ref[...]).astype(bf16)
        elif side == SIDE_ZS:
            uz_scr[srows, :] = uz_scr[srows, :] * _silu(raw)
        else:
            ga_scr[srows, :] = _sigmoid(raw + bm_ref[:, 0:1024])

        align, lane_tile = c % 2, c // 2
        vkeys = slice(lane_tile * LANES, lane_tile * LANES + KEY_BLOCK)
        outs_t = []
        for (hk, par), s in zip(units, scores):
            if r0 < HALO:
                s = jnp.where(valid, s, MASK_VALUE)
            sink = sink_ref[2 * hk + par, 0:1, :] * LOG2E
            m = jnp.maximum(jnp.max(s, axis=0, keepdims=True), sink)
            p = jnp.exp2(s - m)
            denom = jnp.sum(p, axis=0, keepdims=True) + jnp.exp2(sink - m)
            ot = _dot(vt_scr[align, hk, :, vkeys], p.astype(bf16))
            outs_t.append(ot * (1.0 / denom))
        for hk in range(ATT_KV_HEADS):
            o = jnp.concatenate([outs_t[2 * hk], outs_t[2 * hk + 1]], axis=0).T
            for j in range(PAIRS_PER_KV):
                c0 = (hk * PAIRS_PER_KV + j) * LANES
                gated = o[j * CHUNK:(j + 1) * CHUNK] * za_scr[rows, c0:c0 + LANES]
                ag_scr[rows, c0:c0 + LANES] = gated.astype(bf16)

    pos_i = jax.lax.broadcasted_iota(jnp.int32, (SG_BLOCK, SG_BLOCK), 0) // CHUNK
    pos_j = jax.lax.broadcasted_iota(jnp.int32, (SG_BLOCK, SG_BLOCK), 1) // CHUNK
    causal = pos_j <= pos_i
    nblk = T // SG_BLOCK
    for g in range(SG_GROUPS):
        wg = jnp.where(causal, sgw_ref[g], 0.0).astype(bf16)
        c0 = g * LANES
        rhs = jnp.concatenate(
            [vn_scr[b * SG_BLOCK:(b + 1) * SG_BLOCK, c0:c0 + LANES] for b in range(nblk)], axis=1)
        mixed = _dot(wg, rhs)
        for b in range(nblk):
            brows = slice(b * SG_BLOCK, (b + 1) * SG_BLOCK)
            blk = mixed[:, b * LANES:(b + 1) * LANES] + sgb_ref[g]
            sg_scr[brows, c0:c0 + LANES] = (uz_scr[brows, c0:c0 + LANES] * blk).astype(bf16)

    for rb in range(N_ROW_BLOCKS):
        rows = slice(rb * RB, (rb + 1) * RB)
        y = ga_scr[rows, :] * _dot(ag_scr[rows, 0:D_MODEL], wout_ref[:, OFF_WAO:OFF_WAO + D_MODEL])
        gs = _sigmoid(_dot(h_scr[rows, 0:D_MODEL], wa_ref[:, OFF_GS:OFF_GS + D_MODEL])
                      + bm_ref[:, 1024:2048])
        y = y + gs * _dot(sg_scr[rows, 0:D_MODEL], wout_ref[:, OFF_WSO:OFF_WSO + D_MODEL])
        xo = x_ref[0, rows, :] + _dot(y.astype(bf16), wout_ref[:, OFF_WO:OFF_WO + D_MODEL])
        ms2 = jnp.mean(xo * xo, axis=-1, keepdims=True)
        out_ref[0, rows, :] = xo * jax.lax.rsqrt(ms2 + EPS) * fg_ref[...]


def _rope_tables(seq, tile):
    lane = np.arange(LANES) % HEAD_DIM
    inv_freq = ROPE_THETA ** (-(np.arange(ROPE_HALF, dtype=np.float64) * 2.0) / ROPE_DIM)
    freq = np.where(lane < ROPE_DIM, inv_freq[lane % ROPE_HALF], 0.0)
    sign = np.where(lane < ROPE_HALF, -1.0, 1.0)

    def tables(pos):
        ang = pos.astype(np.float64)[:, None] * freq[None, :]
        return np.stack([np.cos(ang), np.sin(ang), np.sin(ang) * sign], axis=0)

    row_tab = tables(np.arange(tile))
    tile_tab = np.zeros((seq // tile, 8, LANES))
    tile_tab[:, 0:3, :] = np.transpose(tables(np.arange(seq // tile) * tile), (1, 0, 2))
    return jnp.asarray(row_tab, jnp.float32), jnp.asarray(tile_tab, jnp.float32)


@jax.jit
def kernel(x, norm_g, w_in, b_merge, att_sinks, sg_w, sg_b, sg_ln_g, sg_ln_b,
           w_att_out, w_sg_out, w_o, final_g):
    B, S, D = x.shape
    T = SEQ_TILE
    assert D == D_MODEL and S % T == 0 and w_in.shape == (1, D, IN_WIDTH)
    bf16 = jnp.bfloat16
    f32 = jnp.float32

    row_tab, tile_tab = _rope_tables(S, T)
    sinks = att_sinks[0].astype(f32).reshape(ATT_KV_HEADS, PAIRS_PER_KV, 2)
    sink_rows = jnp.repeat(jnp.transpose(sinks, (0, 2, 1)).reshape(4, PAIRS_PER_KV), CHUNK, axis=1)
    sink_rows = jnp.broadcast_to(sink_rows[:, None, :], (4, 8, STACK_ROWS))
    sgb = jnp.broadcast_to(sg_b[0].astype(f32)[:, :, None], (SG_GROUPS, SG_BLOCK, LANES))

    hbm = pl.BlockSpec(memory_space=pl.ANY)

    def const(shape):
        zeros = (0,) * len(shape)
        return pl.BlockSpec(shape, lambda b, t: zeros, pipeline_mode=pl.Buffered(1))

    grid_spec = pltpu.PrefetchScalarGridSpec(
        num_scalar_prefetch=0,
        grid=(B, S // T),
        in_specs=[
            pl.BlockSpec((1, T, D), lambda b, t: (b, t, 0)),
            const((1, D)),
            hbm, hbm, hbm, hbm,
            const((1, 2 * D)),
            const((4, 8, STACK_ROWS)),
            const((SG_GROUPS, SG_BLOCK, SG_BLOCK)),
            const((SG_GROUPS, SG_BLOCK, LANES)),
            const((1, D)),
            const((1, D)),
            const((1, D)),
            const((3, T, LANES)),
            const((S // T, 8, LANES)),
        ],
        out_specs=pl.BlockSpec((1, T, D), lambda b, t: (b, t, 0)),
        scratch_shapes=[
            pltpu.VMEM((T, PAD_WIDTH), bf16),
            pltpu.VMEM((ATT_KV_HEADS, N_CHUNKS, LANES, STACK_ROWS), bf16),
            pltpu.VMEM((T, D), f32),
            pltpu.VMEM((4, HALO + T, LANES), bf16),
            pltpu.VMEM((HALO + T, LANES), f32),
            pltpu.VMEM((2, ATT_KV_HEADS, HEAD_DIM, HALO + T), bf16),
            pltpu.VMEM((T, PAD_WIDTH), bf16),
            pltpu.VMEM((T, PAD_WIDTH), bf16),
            pltpu.VMEM((T, PAD_WIDTH), bf16),
            pltpu.VMEM((T, D), f32),
            pltpu.VMEM((T, D), f32),
            pltpu.VMEM((D, IN_WIDTH), bf16),
            pltpu.VMEM((D, OUT_W_WIDTH), bf16),
            pltpu.VMEM((LOAD_SLOTS, LOAD_ROWS, IN_WIDTH), f32),
            pltpu.VMEM((LOAD_SLOTS, 3, LOAD_ROWS, D), f32),
            pltpu.SemaphoreType.DMA((LOAD_SLOTS, 4)),
        ],
    )
    return pl.pallas_call(
        _block_kernel,
        grid_spec=grid_spec,
        out_shape=jax.ShapeDtypeStruct((B, S, D), x.dtype),
        compiler_params=pltpu.CompilerParams(
            dimension_semantics=("arbitrary", "arbitrary"),
            vmem_limit_bytes=VMEM_LIMIT_BYTES,
        ),
        name="hybrid_block",
    )(
        x, norm_g.astype(f32), w_in.astype(f32), w_att_out.astype(f32), w_sg_out.astype(f32),
        w_o.astype(f32), b_merge.astype(f32), sink_rows,
        sg_w[0].astype(f32), sgb, sg_ln_g.astype(f32), sg_ln_b.astype(f32),
        final_g.reshape(1, D).astype(f32), row_tab, tile_tab,
    )
```

```python
import numpy as np
import jax
import jax.numpy as jnp
from jax.experimental import pallas as pl
from jax.experimental.pallas import tpu as pltpu

D_MODEL = 1024
CHUNK = 64
EPS = 1e-6
ATT_HEADS = 16
ATT_KV_HEADS = 2
HEAD_DIM = 64
ATT_GROUP = ATT_HEADS // ATT_KV_HEADS
WINDOW_CHUNKS = 2
HALO = WINDOW_CHUNKS * CHUNK
KEY_BLOCK = HALO + CHUNK
ROPE_DIM = HEAD_DIM // 4
ROPE_HALF = ROPE_DIM // 2
ROPE_THETA = 500000.0
SG_BLOCK = 128
SG_GROUPS = 8
LANES = 128
PAIRS_PER_KV = ATT_GROUP // 2
STACK_ROWS = PAIRS_PER_KV * CHUNK
MASK_VALUE = -1e30

OFF_Q, OFF_K, OFF_V, OFF_ZA = 0, 1024, 1152, 1280
OFF_U, OFF_VS, OFF_ZS, OFF_GM = 2304, 3328, 4352, 5376
IN_WIDTH = 7424
N_SIDE = 4
SIDE_U, SIDE_VS, SIDE_ZS, SIDE_GA = range(N_SIDE)
SIDE_OFF = (OFF_U, OFF_VS, OFF_ZS, OFF_GM)
OFF_GS = OFF_GM + 1024
OFF_WAO, OFF_WSO, OFF_WO = 0, 1024, 2048
PAD_WIDTH = D_MODEL + LANES
OUT_W_WIDTH = 3 * D_MODEL + LANES
LOG2E = np.float32(1.4426950408889634)
LOAD_ROWS = 64
N_LOADS = D_MODEL // LOAD_ROWS
LOAD_SLOTS = 3
assert N_LOADS >= LOAD_SLOTS

SEQ_TILE = 512
ROW_BLOCK = 256
N_ROW_BLOCKS = SEQ_TILE // ROW_BLOCK
N_CHUNKS = SEQ_TILE // CHUNK
VMEM_LIMIT_BYTES = 58 * 1024 * 1024

assert N_CHUNKS == N_SIDE * N_ROW_BLOCKS


def _dot(a, b):
    return jnp.dot(a, b, preferred_element_type=jnp.float32)


def _sigmoid(x):
    return 0.5 * jnp.tanh(0.5 * x) + 0.5


def _silu(x):
    hx = 0.5 * x
    return hx * (1.0 + jnp.tanh(hx))


def _gelu_exact(x):
    return 0.5 * x * (1.0 + jax.lax.erf(x * np.float32(1.0 / np.sqrt(2.0))))


def _load_weights(w_in_hbm, w_out_hbm, wa_ref, wout_ref, stage_in, stage_out, sems):
    n_out = len(w_out_hbm)
    ahead = LOAD_SLOTS - 1

    def copies(i, slot):
        rows = pl.ds(pl.multiple_of(i * LOAD_ROWS, LOAD_ROWS), LOAD_ROWS)
        cps = [pltpu.make_async_copy(w_in_hbm.at[0, rows, :], stage_in.at[slot], sems.at[slot, 0])]
        for k in range(n_out):
            cps.append(pltpu.make_async_copy(w_out_hbm[k].at[0, rows, :], stage_out.at[slot, k],
                                             sems.at[slot, 1 + k]))
        return cps

    for i in range(ahead):
        for cp in copies(i, i):
            cp.start()

    def body(i, carry):
        slot = i % LOAD_SLOTS

        @pl.when(i + ahead < N_LOADS)
        def _():
            for cp in copies(i + ahead, (i + ahead) % LOAD_SLOTS):
                cp.start()

        for cp in copies(i, slot):
            cp.wait()
        rows = pl.ds(pl.multiple_of(i * LOAD_ROWS, LOAD_ROWS), LOAD_ROWS)
        wa_ref[rows, :] = stage_in[slot].astype(jnp.bfloat16)
        for k in range(n_out):
            wout_ref[rows, k * D_MODEL:(k + 1) * D_MODEL] = stage_out[slot, k].astype(jnp.bfloat16)
        return carry

    jax.lax.fori_loop(0, N_LOADS, body, 0)


def _block_kernel(x_ref, ng_ref, w_in_hbm, wao_hbm, wso_hbm, wo_hbm, bm_ref, sink_ref, sgw_ref, sgb_ref,
                  lng_ref, lnb_ref, fg_ref,
                  rrow_ref, rtile_ref,
                  out_ref,
                  h_scr, qt_scr, za_scr, k_scr, v_scr, vt_scr, ag_scr, sg_scr, vn_scr, uz_scr, ga_scr,
                  wa_ref, wout_ref, stage_in, stage_out, load_sems):
    t = pl.program_id(1)
    T = SEQ_TILE
    RB = ROW_BLOCK
    bf16 = jnp.bfloat16

    @pl.when((pl.program_id(0) == 0) & (t == 0))
    def _():
        _load_weights(w_in_hbm, (wao_hbm, wso_hbm, wo_hbm), wa_ref, wout_ref,
                      stage_in, stage_out, load_sems)

    @pl.when(t == 0)
    def _():
        k_scr[:, 0:HALO, :] = jnp.zeros((4, HALO, LANES), bf16)
        v_scr[0:HALO, :] = jnp.zeros((HALO, LANES), jnp.float32)

    @pl.when(t != 0)
    def _():
        k_scr[:, 0:HALO, :] = k_scr[:, T:T + HALO, :]
        v_scr[0:HALO, :] = v_scr[T:T + HALO, :]

    lane = jax.lax.broadcasted_iota(jnp.int32, (RB, LANES), 1)
    first_half = (lane % HEAD_DIM) < ROPE_HALF
    low_half = lane < HEAD_DIM
    scale = np.float32(HEAD_DIM ** -0.5) * LOG2E

    for rb in range(N_ROW_BLOCKS):
        rows = slice(rb * RB, (rb + 1) * RB)
        x = x_ref[0, rows, :]
        ms = jnp.mean(x * x, axis=-1, keepdims=True)
        h = (x * jax.lax.rsqrt(ms + EPS) * ng_ref[...]).astype(bf16)
        h_scr[rows, 0:D_MODEL] = h

        cb, sb, sb_sgn = rrow_ref[0, rows, :], rrow_ref[1, rows, :], rrow_ref[2, rows, :]
        ca, sa, sa_sgn = rtile_ref[t, 0:1, :], rtile_ref[t, 1:2, :], rtile_ref[t, 2:3, :]
        cos_t = ca * cb - sa * sb
        sin_t = sa_sgn * cb + ca * sb_sgn

        def rope(v):
            nxt = pltpu.roll(v, LANES - ROPE_HALF, 1)
            prv = pltpu.roll(v, ROPE_HALF, 1)
            return v * cos_t + jnp.where(first_half, nxt, prv) * sin_t

        kv2 = _dot(h, wa_ref[:, OFF_K:OFF_V + LANES])
        k2 = rope(kv2[:, 0:LANES])
        zeros = jnp.zeros_like(k2)
        krows = slice(HALO + rb * RB, HALO + (rb + 1) * RB)
        h0_lo = jnp.where(low_half, k2, zeros)
        h1_hi = jnp.where(low_half, zeros, k2)
        h0_hi = pltpu.roll(h0_lo, HEAD_DIM, 1)
        h1_lo = pltpu.roll(h1_hi, HEAD_DIM, 1)
        k_scr[0, krows, :] = h0_lo.astype(bf16)
        k_scr[1, krows, :] = h0_hi.astype(bf16)
        k_scr[2, krows, :] = h1_lo.astype(bf16)
        k_scr[3, krows, :] = h1_hi.astype(bf16)
        v_scr[krows, :] = kv2[:, LANES:2 * LANES]

        q = _dot(h, wa_ref[:, OFF_Q:OFF_Q + 1024])
        q_tiles = [rope(q[:, j * LANES:(j + 1) * LANES]) * scale for j in range(1024 // LANES)]
        for hk in range(ATT_KV_HEADS):
            for cc in range(RB // CHUNK):
                stacked = jnp.concatenate(
                    [q_tiles[hk * PAIRS_PER_KV + j][cc * CHUNK:(cc + 1) * CHUNK] for j in range(PAIRS_PER_KV)],
                    axis=0)
                qt_scr[hk, rb * (RB // CHUNK) + cc] = stacked.T.astype(bf16)
        za_scr[rows, :] = _silu(_dot(h, wa_ref[:, OFF_ZA:OFF_ZA + 1024]))

    for a in range(2):
        vt = v_scr[a * CHUNK:HALO + T, :].T
        for hk in range(ATT_KV_HEADS):
            vt_scr[a, hk, :, 0:HALO + T - a * CHUNK] = vt[hk * HEAD_DIM:(hk + 1) * HEAD_DIM].astype(bf16)

    key_off = jax.lax.broadcasted_iota(jnp.int32, (KEY_BLOCK, 1), 0)
    units = [(hk, par) for hk in range(ATT_KV_HEADS) for par in range(2)]
    side_block = D_MODEL // len(units)

    def value_matmul(prev, n):
        pc, probs, recips = prev
        hk = units[n][0]
        vkeys = slice((pc // 2) * LANES, (pc // 2) * LANES + KEY_BLOCK)
        return _dot(vt_scr[pc % 2, hk, :, vkeys], probs[n]) * recips[n]

    def gate_and_store(pc, outs_t):
        prow = slice(pc * CHUNK, (pc + 1) * CHUNK)
        for hk in range(ATT_KV_HEADS):
            o = jnp.concatenate([outs_t[2 * hk], outs_t[2 * hk + 1]], axis=0).T
            for j in range(PAIRS_PER_KV):
                c0 = (hk * PAIRS_PER_KV + j) * LANES
                gated = o[j * CHUNK:(j + 1) * CHUNK] * za_scr[prow, c0:c0 + LANES]
                ag_scr[prow, c0:c0 + LANES] = gated.astype(bf16)

    prev = None
    for c in range(N_CHUNKS):
        r0 = c * CHUNK
        keys = slice(r0, r0 + KEY_BLOCK)
        valid = (key_off + (t * T + r0 - HALO)) >= 0
        side, rb = divmod(c, N_ROW_BLOCKS)
        srows = slice(rb * RB, (rb + 1) * RB)
        scores, raws, outs_t = [], [], []
        for n, (hk, par) in enumerate(units):
            w0 = SIDE_OFF[side] + n * side_block
            raws.append(_dot(h_scr[srows, 0:D_MODEL], wa_ref[:, w0:w0 + side_block]))
            if prev is not None:
                outs_t.append(value_matmul(prev, n))
            scores.append(_dot(k_scr[2 * hk + par, keys, :], qt_scr[hk, c]))
        if prev is not None:
            gate_and_store(prev[0], outs_t)

        raw = jnp.concatenate(raws, axis=1)
        if side == SIDE_U:
            uz_scr[srows, :] = _gelu_exact(raw)
        elif side == SIDE_VS:
            vs = _gelu_exact(raw)
            mu = jnp.mean(vs, axis=-1, keepdims=True)
            vc = vs - mu
            var = jnp.mean(vc * vc, axis=-1, keepdims=True)
            vn_scr[srows, 0:D_MODEL] = (vc * jax.lax.rsqrt(var + EPS) * lng_ref[...]
                                        + lnb_ref[...]).astype(bf16)
        elif side == SIDE_ZS:
            uz_scr[srows, :] = uz_scr[srows, :] * _silu(raw)
        else:
            ga_scr[srows, :] = _sigmoid(raw + bm_ref[:, 0:1024])

        probs, recips = [], []
        for (hk, par), s in zip(units, scores):
            if r0 < HALO:
                s = jnp.where(valid, s, MASK_VALUE)
            sink = sink_ref[2 * hk + par, 0:1, :] * LOG2E
            m = jnp.maximum(jnp.max(s, axis=0, keepdims=True), sink)
            p = jnp.exp2(s - m)
            denom = jnp.sum(p, axis=0, keepdims=True) + jnp.exp2(sink - m)
            probs.append(p.astype(bf16))
            recips.append(1.0 / denom)
        prev = (c, probs, recips)

    att_first, outs_t = [], []
    for n in range(len(units)):
        w0 = OFF_WAO + n * side_block
        att_first.append(_dot(ag_scr[0:RB, 0:D_MODEL], wout_ref[:, w0:w0 + side_block]))
        outs_t.append(value_matmul(prev, n))
    gate_and_store(prev[0], outs_t)
    att_first = jnp.concatenate(att_first, axis=1)

    pos_i = jax.lax.broadcasted_iota(jnp.int32, (SG_BLOCK, SG_BLOCK), 0) // CHUNK
    pos_j = jax.lax.broadcasted_iota(jnp.int32, (SG_BLOCK, SG_BLOCK), 1) // CHUNK
    causal = pos_j <= pos_i
    nblk = T // SG_BLOCK
    for g in range(SG_GROUPS):
        wg = jnp.where(causal, sgw_ref[g], 0.0).astype(bf16)
        c0 = g * LANES
        rhs = jnp.concatenate(
            [vn_scr[b * SG_BLOCK:(b + 1) * SG_BLOCK, c0:c0 + LANES] for b in range(nblk)], axis=1)
        mixed = _dot(wg, rhs)
        for b in range(nblk):
            brows = slice(b * SG_BLOCK, (b + 1) * SG_BLOCK)
            blk = mixed[:, b * LANES:(b + 1) * LANES] + sgb_ref[g]
            sg_scr[brows, c0:c0 + LANES] = (uz_scr[brows, c0:c0 + LANES] * blk).astype(bf16)

    for rb in range(N_ROW_BLOCKS):
        rows = slice(rb * RB, (rb + 1) * RB)
        att = att_first if rb == 0 else _dot(ag_scr[rows, 0:D_MODEL],
                                             wout_ref[:, OFF_WAO:OFF_WAO + D_MODEL])
        y = ga_scr[rows, :] * att
        gs = _sigmoid(_dot(h_scr[rows, 0:D_MODEL], wa_ref[:, OFF_GS:OFF_GS + D_MODEL])
                      + bm_ref[:, 1024:2048])
        y = y + gs * _dot(sg_scr[rows, 0:D_MODEL], wout_ref[:, OFF_WSO:OFF_WSO + D_MODEL])
        xo = x_ref[0, rows, :] + _dot(y.astype(bf16), wout_ref[:, OFF_WO:OFF_WO + D_MODEL])
        ms2 = jnp.mean(xo * xo, axis=-1, keepdims=True)
        out_ref[0, rows, :] = xo * jax.lax.rsqrt(ms2 + EPS) * fg_ref[...]


def _rope_tables(seq, tile):
    lane = np.arange(LANES) % HEAD_DIM
    inv_freq = ROPE_THETA ** (-(np.arange(ROPE_HALF, dtype=np.float64) * 2.0) / ROPE_DIM)
    freq = np.where(lane < ROPE_DIM, inv_freq[lane % ROPE_HALF], 0.0)
    sign = np.where(lane < ROPE_HALF, -1.0, 1.0)

    def tables(pos):
        ang = pos.astype(np.float64)[:, None] * freq[None, :]
        return np.stack([np.cos(ang), np.sin(ang), np.sin(ang) * sign], axis=0)

    row_tab = tables(np.arange(tile))
    tile_tab = np.zeros((seq // tile, 8, LANES))
    tile_tab[:, 0:3, :] = np.transpose(tables(np.arange(seq // tile) * tile), (1, 0, 2))
    return jnp.asarray(row_tab, jnp.float32), jnp.asarray(tile_tab, jnp.float32)


@jax.jit
def kernel(x, norm_g, w_in, b_merge, att_sinks, sg_w, sg_b, sg_ln_g, sg_ln_b,
           w_att_out, w_sg_out, w_o, final_g):
    B, S, D = x.shape
    T = SEQ_TILE
    assert D == D_MODEL and S % T == 0 and w_in.shape == (1, D, IN_WIDTH)
    bf16 = jnp.bfloat16
    f32 = jnp.float32

    row_tab, tile_tab = _rope_tables(S, T)
    sinks = att_sinks[0].astype(f32).reshape(ATT_KV_HEADS, PAIRS_PER_KV, 2)
    sink_rows = jnp.repeat(jnp.transpose(sinks, (0, 2, 1)).reshape(4, PAIRS_PER_KV), CHUNK, axis=1)
    sink_rows = jnp.broadcast_to(sink_rows[:, None, :], (4, 8, STACK_ROWS))
    sgb = jnp.broadcast_to(sg_b[0].astype(f32)[:, :, None], (SG_GROUPS, SG_BLOCK, LANES))

    hbm = pl.BlockSpec(memory_space=pl.ANY)

    def const(shape):
        zeros = (0,) * len(shape)
        return pl.BlockSpec(shape, lambda b, t: zeros, pipeline_mode=pl.Buffered(1))

    grid_spec = pltpu.PrefetchScalarGridSpec(
        num_scalar_prefetch=0,
        grid=(B, S // T),
        in_specs=[
            pl.BlockSpec((1, T, D), lambda b, t: (b, t, 0)),
            const((1, D)),
            hbm, hbm, hbm, hbm,
            const((1, 2 * D)),
            const((4, 8, STACK_ROWS)),
            const((SG_GROUPS, SG_BLOCK, SG_BLOCK)),
            const((SG_GROUPS, SG_BLOCK, LANES)),
            const((1, D)),
            const((1, D)),
            const((1, D)),
            const((3, T, LANES)),
            const((S // T, 8, LANES)),
        ],
        out_specs=pl.BlockSpec((1, T, D), lambda b, t: (b, t, 0)),
        scratch_shapes=[
            pltpu.VMEM((T, PAD_WIDTH), bf16),
            pltpu.VMEM((ATT_KV_HEADS, N_CHUNKS, LANES, STACK_ROWS), bf16),
            pltpu.VMEM((T, D), f32),
            pltpu.VMEM((4, HALO + T, LANES), bf16),
            pltpu.VMEM((HALO + T, LANES), f32),
            pltpu.VMEM((2, ATT_KV_HEADS, HEAD_DIM, HALO + T), bf16),
            pltpu.VMEM((T, PAD_WIDTH), bf16),
            pltpu.VMEM((T, PAD_WIDTH), bf16),
            pltpu.VMEM((T, PAD_WIDTH), bf16),
            pltpu.VMEM((T, D), f32),
            pltpu.VMEM((T, D), f32),
            pltpu.VMEM((D, IN_WIDTH), bf16),
            pltpu.VMEM((D, OUT_W_WIDTH), bf16),
            pltpu.VMEM((LOAD_SLOTS, LOAD_ROWS, IN_WIDTH), f32),
            pltpu.VMEM((LOAD_SLOTS, 3, LOAD_ROWS, D), f32),
            pltpu.SemaphoreType.DMA((LOAD_SLOTS, 4)),
        ],
    )
    return pl.pallas_call(
        _block_kernel,
        grid_spec=grid_spec,
        out_shape=jax.ShapeDtypeStruct((B, S, D), x.dtype),
        compiler_params=pltpu.CompilerParams(
            dimension_semantics=("arbitrary", "arbitrary"),
            vmem_limit_bytes=VMEM_LIMIT_BYTES,
        ),
        name="hybrid_block",
    )(
        x, norm_g.astype(f32), w_in.astype(f32), w_att_out.astype(f32), w_sg_out.astype(f32),
        w_o.astype(f32), b_merge.astype(f32), sink_rows,
        sg_w[0].astype(f32), sgb, sg_ln_g.astype(f32), sg_ln_b.astype(f32),
        final_g.reshape(1, D).astype(f32), row_tab, tile_tab,
    )
```

```python
import numpy as np
import jax
import jax.numpy as jnp
from jax.experimental import pallas as pl
from jax.experimental.pallas import tpu as pltpu

D_MODEL = 1024
CHUNK = 64
EPS = 1e-6
ATT_HEADS = 16
ATT_KV_HEADS = 2
HEAD_DIM = 64
ATT_GROUP = ATT_HEADS // ATT_KV_HEADS
WINDOW_CHUNKS = 2
HALO = WINDOW_CHUNKS * CHUNK
KEY_BLOCK = HALO + CHUNK
ROPE_DIM = HEAD_DIM // 4
ROPE_HALF = ROPE_DIM // 2
ROPE_THETA = 500000.0
SG_BLOCK = 128
SG_GROUPS = 8
LANES = 128
PAIRS_PER_KV = ATT_GROUP // 2
STACK_ROWS = PAIRS_PER_KV * CHUNK
MASK_VALUE = -1e30

OFF_Q, OFF_K, OFF_V, OFF_ZA = 0, 1024, 1152, 1280
OFF_U, OFF_VS, OFF_ZS, OFF_GM = 2304, 3328, 4352, 5376
IN_WIDTH = 7424
N_SIDE = 4
SIDE_U, SIDE_VS, SIDE_ZS, SIDE_GA = range(N_SIDE)
SIDE_OFF = (OFF_U, OFF_VS, OFF_ZS, OFF_GM)
OFF_GS = OFF_GM + 1024
OFF_WAO, OFF_WSO, OFF_WO = 0, 1024, 2048
PAD_WIDTH = D_MODEL + LANES
OUT_W_WIDTH = 3 * D_MODEL + LANES
LOG2E = np.float32(1.4426950408889634)
LOAD_ROWS = 64
N_LOADS = D_MODEL // LOAD_ROWS
LOAD_SLOTS = 3
assert N_LOADS >= LOAD_SLOTS

SEQ_TILE = 512
ROW_BLOCK = 256
N_ROW_BLOCKS = SEQ_TILE // ROW_BLOCK
N_CHUNKS = SEQ_TILE // CHUNK
VMEM_LIMIT_BYTES = 58 * 1024 * 1024

assert N_CHUNKS == N_SIDE * N_ROW_BLOCKS


def _dot(a, b):
    return jnp.dot(a, b, preferred_element_type=jnp.float32)


def _sigmoid(x):
    return 0.5 * jnp.tanh(0.5 * x) + 0.5


def _silu(x):
    hx = 0.5 * x
    return hx * (1.0 + jnp.tanh(hx))


def _gelu_exact(x):
    return 0.5 * x * (1.0 + jax.lax.erf(x * np.float32(1.0 / np.sqrt(2.0))))


def _load_weights(w_in_hbm, w_out_hbm, wa_ref, wout_ref, stage_in, stage_out, sems):
    n_out = len(w_out_hbm)
    ahead = LOAD_SLOTS - 1

    def copies(i, slot):
        rows = pl.ds(pl.multiple_of(i * LOAD_ROWS, LOAD_ROWS), LOAD_ROWS)
        cps = [pltpu.make_async_copy(w_in_hbm.at[0, rows, :], stage_in.at[slot], sems.at[slot, 0])]
        for k in range(n_out):
            cps.append(pltpu.make_async_copy(w_out_hbm[k].at[0, rows, :], stage_out.at[slot, k],
                                             sems.at[slot, 1 + k]))
        return cps

    for i in range(ahead):
        for cp in copies(i, i):
            cp.start()

    def body(i, carry):
        slot = i % LOAD_SLOTS

        @pl.when(i + ahead < N_LOADS)
        def _():
            for cp in copies(i + ahead, (i + ahead) % LOAD_SLOTS):
                cp.start()

        for cp in copies(i, slot):
            cp.wait()
        rows = pl.ds(pl.multiple_of(i * LOAD_ROWS, LOAD_ROWS), LOAD_ROWS)
        wa_ref[rows, :] = stage_in[slot].astype(jnp.bfloat16)
        for k in range(n_out):
            wout_ref[rows, k * D_MODEL:(k + 1) * D_MODEL] = stage_out[slot, k].astype(jnp.bfloat16)
        return carry

    jax.lax.fori_loop(0, N_LOADS, body, 0)


def _block_kernel(x_ref, ng_ref, w_in_hbm, wao_hbm, wso_hbm, wo_hbm, bm_ref, sink_ref, sgw_ref, sgb_ref,
                  lng_ref, lnb_ref, fg_ref,
                  rrow_ref, rtile_ref,
                  out_ref,
                  h_scr, qt_scr, za_scr, k_scr, v_scr, vt_scr, ag_scr, sg_scr, vn_scr, uz_scr, ga_scr,
                  wa_ref, wout_ref, stage_in, stage_out, load_sems):
    t = pl.program_id(1)
    T = SEQ_TILE
    RB = ROW_BLOCK
    bf16 = jnp.bfloat16

    @pl.when((pl.program_id(0) == 0) & (t == 0))
    def _():
        _load_weights(w_in_hbm, (wao_hbm, wso_hbm, wo_hbm), wa_ref, wout_ref,
                      stage_in, stage_out, load_sems)

    @pl.when(t == 0)
    def _():
        k_scr[:, 0:HALO, :] = jnp.zeros((4, HALO, LANES), bf16)
        v_scr[0:HALO, :] = jnp.zeros((HALO, LANES), jnp.float32)

    @pl.when(t != 0)
    def _():
        k_scr[:, 0:HALO, :] = k_scr[:, T:T + HALO, :]
        v_scr[0:HALO, :] = v_scr[T:T + HALO, :]

    lane = jax.lax.broadcasted_iota(jnp.int32, (RB, LANES), 1)
    first_half = (lane % HEAD_DIM) < ROPE_HALF
    low_half = lane < HEAD_DIM
    scale = np.float32(HEAD_DIM ** -0.5) * LOG2E

    for rb in range(N_ROW_BLOCKS):
        rows = slice(rb * RB, (rb + 1) * RB)
        x = x_ref[0, rows, :]
        ms = jnp.mean(x * x, axis=-1, keepdims=True)
        h = (x * jax.lax.rsqrt(ms + EPS) * ng_ref[...]).astype(bf16)
        h_scr[rows, 0:D_MODEL] = h

        cb, sb, sb_sgn = rrow_ref[0, rows, :], rrow_ref[1, rows, :], rrow_ref[2, rows, :]
        ca, sa, sa_sgn = rtile_ref[t, 0:1, :], rtile_ref[t, 1:2, :], rtile_ref[t, 2:3, :]
        cos_t = ca * cb - sa * sb
        sin_t = sa_sgn * cb + ca * sb_sgn

        def rope(v):
            nxt = pltpu.roll(v, LANES - ROPE_HALF, 1)
            prv = pltpu.roll(v, ROPE_HALF, 1)
            return v * cos_t + jnp.where(first_half, nxt, prv) * sin_t

        kv2 = _dot(h, wa_ref[:, OFF_K:OFF_V + LANES])
        k2 = rope(kv2[:, 0:LANES])
        zeros = jnp.zeros_like(k2)
        krows = slice(HALO + rb * RB, HALO + (rb + 1) * RB)
        h0_lo = jnp.where(low_half, k2, zeros)
        h1_hi = jnp.where(low_half, zeros, k2)
        h0_hi = pltpu.roll(h0_lo, HEAD_DIM, 1)
        h1_lo = pltpu.roll(h1_hi, HEAD_DIM, 1)
        k_scr[0, krows, :] = h0_lo.astype(bf16)
        k_scr[1, krows, :] = h0_hi.astype(bf16)
        k_scr[2, krows, :] = h1_lo.astype(bf16)
        k_scr[3, krows, :] = h1_hi.astype(bf16)
        v_scr[krows, :] = kv2[:, LANES:2 * LANES]

        q = _dot(h, wa_ref[:, OFF_Q:OFF_Q + 1024])
        q_tiles = [rope(q[:, j * LANES:(j + 1) * LANES]) * scale for j in range(1024 // LANES)]
        for hk in range(ATT_KV_HEADS):
            for cc in range(RB // CHUNK):
                stacked = jnp.concatenate(
                    [q_tiles[hk * PAIRS_PER_KV + j][cc * CHUNK:(cc + 1) * CHUNK] for j in range(PAIRS_PER_KV)],
                    axis=0)
                qt_scr[hk, rb * (RB // CHUNK) + cc] = stacked.T.astype(bf16)
        za_scr[rows, :] = _silu(_dot(h, wa_ref[:, OFF_ZA:OFF_ZA + 1024]))

    for a in range(2):
        vt = v_scr[a * CHUNK:HALO + T, :].T
        for hk in range(ATT_KV_HEADS):
            vt_scr[a, hk, :, 0:HALO + T - a * CHUNK] = vt[hk * HEAD_DIM:(hk + 1) * HEAD_DIM].astype(bf16)

    key_off = jax.lax.broadcasted_iota(jnp.int32, (KEY_BLOCK, 1), 0)
    units = [(hk, par) for hk in range(ATT_KV_HEADS) for par in range(2)]
    side_block = D_MODEL // len(units)

    def value_matmul(prev, n):
        pc, probs, recips = prev
        hk = units[n][0]
        vkeys = slice((pc // 2) * LANES, (pc // 2) * LANES + KEY_BLOCK)
        return _dot(vt_scr[pc % 2, hk, :, vkeys], probs[n]) * recips[n]

    def gate_and_store(pc, outs_t):
        prow = slice(pc * CHUNK, (pc + 1) * CHUNK)
        for hk in range(ATT_KV_HEADS):
            o = jnp.concatenate([outs_t[2 * hk], outs_t[2 * hk + 1]], axis=0).T
            for j in range(PAIRS_PER_KV):
                c0 = (hk * PAIRS_PER_KV + j) * LANES
                gated = o[j * CHUNK:(j + 1) * CHUNK] * za_scr[prow, c0:c0 + LANES]
                ag_scr[prow, c0:c0 + LANES] = gated.astype(bf16)

    prev = None
    for c in range(N_CHUNKS):
        r0 = c * CHUNK
        keys = slice(r0, r0 + KEY_BLOCK)
        valid = (key_off + (t * T + r0 - HALO)) >= 0
        side, rb = divmod(c, N_ROW_BLOCKS)
        srows = slice(rb * RB, (rb + 1) * RB)
        scores, raws, outs_t = [], [], []
        for n, (hk, par) in enumerate(units):
            w0 = SIDE_OFF[side] + n * side_block
            raws.append(_dot(h_scr[srows, 0:D_MODEL], wa_ref[:, w0:w0 + side_block]))
            if prev is not None:
                outs_t.append(value_matmul(prev, n))
            scores.append(_dot(k_scr[2 * hk + par, keys, :], qt_scr[hk, c]))
        if prev is not None:
            gate_and_store(prev[0], outs_t)

        raw = jnp.concatenate(raws, axis=1)
        if side == SIDE_U:
            uz_scr[srows, :] = _gelu_exact(raw)
        elif side == SIDE_VS:
            vs = _gelu_exact(raw)
            mu = jnp.mean(vs, axis=-1, keepdims=True)
            vc = vs - mu
            var = jnp.mean(vc * vc, axis=-1, keepdims=True)
            vn_scr[srows, 0:D_MODEL] = (vc * jax.lax.rsqrt(var + EPS) * lng_ref[...]
                                        + lnb_ref[...]).astype(bf16)
        elif side == SIDE_ZS:
            uz_scr[srows, :] = uz_scr[srows, :] * _silu(raw)
        else:
            ga_scr[srows, :] = _sigmoid(raw + bm_ref[:, 0:1024])

        probs, recips = [], []
        for (hk, par), s in zip(units, scores):
            if r0 < HALO:
                s = jnp.where(valid, s, MASK_VALUE)
            sink = sink_ref[2 * hk + par, 0:1, :] * LOG2E
            m = jnp.maximum(jnp.max(s, axis=0, keepdims=True), sink)
            p = jnp.exp2(s - m)
            denom = jnp.sum(p, axis=0, keepdims=True) + jnp.exp2(sink - m)
            probs.append(p.astype(bf16))
            recips.append(1.0 / denom)
        prev = (c, probs, recips)

    att_first, outs_t = [], []
    for n in range(len(units)):
        w0 = OFF_WAO + n * side_block
        att_first.append(_dot(ag_scr[0:RB, 0:D_MODEL], wout_ref[:, w0:w0 + side_block]))
        outs_t.append(value_matmul(prev, n))
    gate_and_store(prev[0], outs_t)
    att_first = jnp.concatenate(att_first, axis=1)

    pos_i = jax.lax.broadcasted_iota(jnp.int32, (SG_BLOCK, SG_BLOCK), 0) // CHUNK
    pos_j = jax.lax.broadcasted_iota(jnp.int32, (SG_BLOCK, SG_BLOCK), 1) // CHUNK
    causal = pos_j <= pos_i
    nblk = T // SG_BLOCK
    gs_blocks = [[] for _ in range(N_ROW_BLOCKS)]
    assert SG_GROUPS == N_ROW_BLOCKS * (D_MODEL // side_block)
    for g in range(SG_GROUPS):
        grb, n = divmod(g, D_MODEL // side_block)
        gs_blocks[grb].append(_dot(h_scr[grb * RB:(grb + 1) * RB, 0:D_MODEL],
                                   wa_ref[:, OFF_GS + n * side_block:OFF_GS + (n + 1) * side_block]))
        wg = jnp.where(causal, sgw_ref[g], 0.0).astype(bf16)
        c0 = g * LANES
        rhs = jnp.concatenate(
            [vn_scr[b * SG_BLOCK:(b + 1) * SG_BLOCK, c0:c0 + LANES] for b in range(nblk)], axis=1)
        mixed = _dot(wg, rhs)
        for b in range(nblk):
            brows = slice(b * SG_BLOCK, (b + 1) * SG_BLOCK)
            blk = mixed[:, b * LANES:(b + 1) * LANES] + sgb_ref[g]
            sg_scr[brows, c0:c0 + LANES] = (uz_scr[brows, c0:c0 + LANES] * blk).astype(bf16)

    for rb in range(N_ROW_BLOCKS):
        rows = slice(rb * RB, (rb + 1) * RB)
        att = att_first if rb == 0 else _dot(ag_scr[rows, 0:D_MODEL],
                                             wout_ref[:, OFF_WAO:OFF_WAO + D_MODEL])
        y = ga_scr[rows, :] * att
        gs = _sigmoid(jnp.concatenate(gs_blocks[rb], axis=1) + bm_ref[:, 1024:2048])
        y = y + gs * _dot(sg_scr[rows, 0:D_MODEL], wout_ref[:, OFF_WSO:OFF_WSO + D_MODEL])
        xo = x_ref[0, rows, :] + _dot(y.astype(bf16), wout_ref[:, OFF_WO:OFF_WO + D_MODEL])
        ms2 = jnp.mean(xo * xo, axis=-1, keepdims=True)
        out_ref[0, rows, :] = xo * jax.lax.rsqrt(ms2 + EPS) * fg_ref[...]


def _rope_tables(seq, tile):
    lane = np.arange(LANES) % HEAD_DIM
    inv_freq = ROPE_THETA ** (-(np.arange(ROPE_HALF, dtype=np.float64) * 2.0) / ROPE_DIM)
    freq = np.where(lane < ROPE_DIM, inv_freq[lane % ROPE_HALF], 0.0)
    sign = np.where(lane < ROPE_HALF, -1.0, 1.0)

    def tables(pos):
        ang = pos.astype(np.float64)[:, None] * freq[None, :]
        return np.stack([np.cos(ang), np.sin(ang), np.sin(ang) * sign], axis=0)

    row_tab = tables(np.arange(tile))
    tile_tab = np.zeros((seq // tile, 8, LANES))
    tile_tab[:, 0:3, :] = np.transpose(tables(np.arange(seq // tile) * tile), (1, 0, 2))
    return jnp.asarray(row_tab, jnp.float32), jnp.asarray(tile_tab, jnp.float32)


@jax.jit
def kernel(x, norm_g, w_in, b_merge, att_sinks, sg_w, sg_b, sg_ln_g, sg_ln_b,
           w_att_out, w_sg_out, w_o, final_g):
    B, S, D = x.shape
    T = SEQ_TILE
    assert D == D_MODEL and S % T == 0 and w_in.shape == (1, D, IN_WIDTH)
    bf16 = jnp.bfloat16
    f32 = jnp.float32

    row_tab, tile_tab = _rope_tables(S, T)
    sinks = att_sinks[0].astype(f32).reshape(ATT_KV_HEADS, PAIRS_PER_KV, 2)
    sink_rows = jnp.repeat(jnp.transpose(sinks, (0, 2, 1)).reshape(4, PAIRS_PER_KV), CHUNK, axis=1)
    sink_rows = jnp.broadcast_to(sink_rows[:, None, :], (4, 8, STACK_ROWS))
    sgb = jnp.broadcast_to(sg_b[0].astype(f32)[:, :, None], (SG_GROUPS, SG_BLOCK, LANES))

    hbm = pl.BlockSpec(memory_space=pl.ANY)

    def const(shape):
        zeros = (0,) * len(shape)
        return pl.BlockSpec(shape, lambda b, t: zeros, pipeline_mode=pl.Buffered(1))

    grid_spec = pltpu.PrefetchScalarGridSpec(
        num_scalar_prefetch=0,
        grid=(B, S // T),
        in_specs=[
            pl.BlockSpec((1, T, D), lambda b, t: (b, t, 0)),
            const((1, D)),
            hbm, hbm, hbm, hbm,
            const((1, 2 * D)),
            const((4, 8, STACK_ROWS)),
            const((SG_GROUPS, SG_BLOCK, SG_BLOCK)),
            const((SG_GROUPS, SG_BLOCK, LANES)),
            const((1, D)),
            const((1, D)),
            const((1, D)),
            const((3, T, LANES)),
            const((S // T, 8, LANES)),
        ],
        out_specs=pl.BlockSpec((1, T, D), lambda b, t: (b, t, 0)),
        scratch_shapes=[
            pltpu.VMEM((T, PAD_WIDTH), bf16),
            pltpu.VMEM((ATT_KV_HEADS, N_CHUNKS, LANES, STACK_ROWS), bf16),
            pltpu.VMEM((T, D), f32),
            pltpu.VMEM((4, HALO + T, LANES), bf16),
            pltpu.VMEM((HALO + T, LANES), f32),
            pltpu.VMEM((2, ATT_KV_HEADS, HEAD_DIM, HALO + T), bf16),
            pltpu.VMEM((T, PAD_WIDTH), bf16),
            pltpu.VMEM((T, PAD_WIDTH), bf16),
            pltpu.VMEM((T, PAD_WIDTH), bf16),
            pltpu.VMEM((T, D), f32),
            pltpu.VMEM((T, D), f32),
            pltpu.VMEM((D, IN_WIDTH), bf16),
            pltpu.VMEM((D, OUT_W_WIDTH), bf16),
            pltpu.VMEM((LOAD_SLOTS, LOAD_ROWS, IN_WIDTH), f32),
            pltpu.VMEM((LOAD_SLOTS, 3, LOAD_ROWS, D), f32),
            pltpu.SemaphoreType.DMA((LOAD_SLOTS, 4)),
        ],
    )
    return pl.pallas_call(
        _block_kernel,
        grid_spec=grid_spec,
        out_shape=jax.ShapeDtypeStruct((B, S, D), x.dtype),
        compiler_params=pltpu.CompilerParams(
            dimension_semantics=("arbitrary", "arbitrary"),
            vmem_limit_bytes=VMEM_LIMIT_BYTES,
        ),
        name="hybrid_block",
    )(
        x, norm_g.astype(f32), w_in.astype(f32), w_att_out.astype(f32), w_sg_out.astype(f32),
        w_o.astype(f32), b_merge.astype(f32), sink_rows,
        sg_w[0].astype(f32), sgb, sg_ln_g.astype(f32), sg_ln_b.astype(f32),
        final_g.reshape(1, D).astype(f32), row_tab, tile_tab,
    )
```

```python
import numpy as np
import jax
import jax.numpy as jnp
from jax.experimental import pallas as pl
from jax.experimental.pallas import tpu as pltpu

D_MODEL = 1024
CHUNK = 64
EPS = 1e-6
ATT_HEADS = 16
ATT_KV_HEADS = 2
HEAD_DIM = 64
ATT_GROUP = ATT_HEADS // ATT_KV_HEADS
WINDOW_CHUNKS = 2
HALO = WINDOW_CHUNKS * CHUNK
KEY_BLOCK = HALO + CHUNK
ROPE_DIM = HEAD_DIM // 4
ROPE_HALF = ROPE_DIM // 2
ROPE_THETA = 500000.0
SG_BLOCK = 128
SG_GROUPS = 8
LANES = 128
PAIRS_PER_KV = ATT_GROUP // 2
STACK_ROWS = PAIRS_PER_KV * CHUNK
MASK_VALUE = -1e30

OFF_Q, OFF_K, OFF_V, OFF_ZA = 0, 1024, 1152, 1280
OFF_U, OFF_VS, OFF_ZS, OFF_GM = 2304, 3328, 4352, 5376
IN_WIDTH = 7424
N_SIDE = 4
SIDE_U, SIDE_VS, SIDE_ZS, SIDE_GA = range(N_SIDE)
SIDE_OFF = (OFF_U, OFF_VS, OFF_ZS, OFF_GM)
OFF_GS = OFF_GM + 1024
OFF_WAO, OFF_WSO, OFF_WO = 0, 1024, 2048
PAD_WIDTH = D_MODEL + LANES
OUT_W_WIDTH = 3 * D_MODEL + LANES
LOG2E = np.float32(1.4426950408889634)
LOAD_ROWS = 64
N_LOADS = D_MODEL // LOAD_ROWS
LOAD_SLOTS = 3
assert N_LOADS >= LOAD_SLOTS

SEQ_TILE = 512
ROW_BLOCK = 256
N_ROW_BLOCKS = SEQ_TILE // ROW_BLOCK
N_CHUNKS = SEQ_TILE // CHUNK
VMEM_LIMIT_BYTES = 58 * 1024 * 1024

assert N_CHUNKS == N_SIDE * N_ROW_BLOCKS


def _dot(a, b):
    return jnp.dot(a, b, preferred_element_type=jnp.float32)


def _sigmoid(x):
    return 0.5 * jnp.tanh(0.5 * x) + 0.5


def _silu(x):
    hx = 0.5 * x
    return hx * (1.0 + jnp.tanh(hx))


def _gelu_exact(x):
    return 0.5 * x * (1.0 + jax.lax.erf(x * np.float32(1.0 / np.sqrt(2.0))))


def _load_weights(w_in_hbm, w_out_hbm, wa_ref, wout_ref, stage_in, stage_out, sems):
    n_out = len(w_out_hbm)
    ahead = LOAD_SLOTS - 1

    def copies(i, slot):
        rows = pl.ds(pl.multiple_of(i * LOAD_ROWS, LOAD_ROWS), LOAD_ROWS)
        cps = [pltpu.make_async_copy(w_in_hbm.at[0, rows, :], stage_in.at[slot], sems.at[slot, 0])]
        for k in range(n_out):
            cps.append(pltpu.make_async_copy(w_out_hbm[k].at[0, rows, :], stage_out.at[slot, k],
                                             sems.at[slot, 1 + k]))
        return cps

    for i in range(ahead):
        for cp in copies(i, i):
            cp.start()

    def body(i, carry):
        slot = i % LOAD_SLOTS

        @pl.when(i + ahead < N_LOADS)
        def _():
            for cp in copies(i + ahead, (i + ahead) % LOAD_SLOTS):
                cp.start()

        for cp in copies(i, slot):
            cp.wait()
        rows = pl.ds(pl.multiple_of(i * LOAD_ROWS, LOAD_ROWS), LOAD_ROWS)
        wa_ref[rows, :] = stage_in[slot].astype(jnp.bfloat16)
        for k in range(n_out):
            wout_ref[rows, k * D_MODEL:(k + 1) * D_MODEL] = stage_out[slot, k].astype(jnp.bfloat16)
        return carry

    jax.lax.fori_loop(0, N_LOADS, body, 0)


def _block_kernel(x_ref, ng_ref, w_in_hbm, wao_hbm, wso_hbm, wo_hbm, bm_ref, sink_ref, sgw_ref, sgb_ref,
                  lng_ref, lnb_ref, fg_ref,
                  rrow_ref, rtile_ref,
                  out_ref,
                  h_scr, qt_scr, za_scr, k_scr, v_scr, vt_scr, ag_scr, sg_scr, vn_scr, uz_scr, ga_scr,
                  wa_ref, wout_ref, stage_in, stage_out, load_sems):
    t = pl.program_id(1)
    T = SEQ_TILE
    RB = ROW_BLOCK
    bf16 = jnp.bfloat16

    @pl.when((pl.program_id(0) == 0) & (t == 0))
    def _():
        _load_weights(w_in_hbm, (wao_hbm, wso_hbm, wo_hbm), wa_ref, wout_ref,
                      stage_in, stage_out, load_sems)

    @pl.when(t == 0)
    def _():
        k_scr[:, 0:HALO, :] = jnp.zeros((4, HALO, LANES), bf16)
        v_scr[0:HALO, :] = jnp.zeros((HALO, LANES), jnp.float32)

    @pl.when(t != 0)
    def _():
        k_scr[:, 0:HALO, :] = k_scr[:, T:T + HALO, :]
        v_scr[0:HALO, :] = v_scr[T:T + HALO, :]

    lane = jax.lax.broadcasted_iota(jnp.int32, (RB, LANES), 1)
    first_half = (lane % HEAD_DIM) < ROPE_HALF
    low_half = lane < HEAD_DIM
    scale = np.float32(HEAD_DIM ** -0.5) * LOG2E

    for rb in range(N_ROW_BLOCKS):
        rows = slice(rb * RB, (rb + 1) * RB)
        x = x_ref[0, rows, :]
        ms = jnp.mean(x * x, axis=-1, keepdims=True)
        h = (x * jax.lax.rsqrt(ms + EPS) * ng_ref[...]).astype(bf16)
        h_scr[rows, 0:D_MODEL] = h

        cb, sb, sb_sgn = rrow_ref[0, rows, :], rrow_ref[1, rows, :], rrow_ref[2, rows, :]
        ca, sa, sa_sgn = rtile_ref[t, 0:1, :], rtile_ref[t, 1:2, :], rtile_ref[t, 2:3, :]
        cos_t = ca * cb - sa * sb
        sin_t = sa_sgn * cb + ca * sb_sgn

        def rope(v):
            nxt = pltpu.roll(v, LANES - ROPE_HALF, 1)
            prv = pltpu.roll(v, ROPE_HALF, 1)
            return v * cos_t + jnp.where(first_half, nxt, prv) * sin_t

        kv2 = _dot(h, wa_ref[:, OFF_K:OFF_V + LANES])
        k2 = rope(kv2[:, 0:LANES])
        zeros = jnp.zeros_like(k2)
        krows = slice(HALO + rb * RB, HALO + (rb + 1) * RB)
        h0_lo = jnp.where(low_half, k2, zeros)
        h1_hi = jnp.where(low_half, zeros, k2)
        h0_hi = pltpu.roll(h0_lo, HEAD_DIM, 1)
        h1_lo = pltpu.roll(h1_hi, HEAD_DIM, 1)
        k_scr[0, krows, :] = h0_lo.astype(bf16)
        k_scr[1, krows, :] = h0_hi.astype(bf16)
        k_scr[2, krows, :] = h1_lo.astype(bf16)
        k_scr[3, krows, :] = h1_hi.astype(bf16)
        v_scr[krows, :] = kv2[:, LANES:2 * LANES]

        q = _dot(h, wa_ref[:, OFF_Q:OFF_Q + 1024])
        q_tiles = [rope(q[:, j * LANES:(j + 1) * LANES]) * scale for j in range(1024 // LANES)]
        for hk in range(ATT_KV_HEADS):
            for cc in range(RB // CHUNK):
                stacked = jnp.concatenate(
                    [q_tiles[hk * PAIRS_PER_KV + j][cc * CHUNK:(cc + 1) * CHUNK] for j in range(PAIRS_PER_KV)],
                    axis=0)
                qt_scr[hk, rb * (RB // CHUNK) + cc] = stacked.T.astype(bf16)
        za_scr[rows, :] = _silu(_dot(h, wa_ref[:, OFF_ZA:OFF_ZA + 1024]))

    for a in range(2):
        vt = v_scr[a * CHUNK:HALO + T, :].T
        for hk in range(ATT_KV_HEADS):
            vt_scr[a, hk, :, 0:HALO + T - a * CHUNK] = vt[hk * HEAD_DIM:(hk + 1) * HEAD_DIM].astype(bf16)

    key_off = jax.lax.broadcasted_iota(jnp.int32, (KEY_BLOCK, 1), 0)
    units = [(hk, par) for hk in range(ATT_KV_HEADS) for par in range(2)]
    side_block = D_MODEL // len(units)

    def value_matmul(prev, n):
        pc, probs, recips = prev
        hk = units[n][0]
        vkeys = slice((pc // 2) * LANES, (pc // 2) * LANES + KEY_BLOCK)
        return _dot(vt_scr[pc % 2, hk, :, vkeys], probs[n]) * recips[n]

    def gate_and_store(pc, outs_t):
        prow = slice(pc * CHUNK, (pc + 1) * CHUNK)
        for hk in range(ATT_KV_HEADS):
            o = jnp.concatenate([outs_t[2 * hk], outs_t[2 * hk + 1]], axis=0).T
            for j in range(PAIRS_PER_KV):
                c0 = (hk * PAIRS_PER_KV + j) * LANES
                gated = o[j * CHUNK:(j + 1) * CHUNK] * za_scr[prow, c0:c0 + LANES]
                ag_scr[prow, c0:c0 + LANES] = gated.astype(bf16)

    prev = None
    for c in range(N_CHUNKS):
        r0 = c * CHUNK
        keys = slice(r0, r0 + KEY_BLOCK)
        valid = (key_off + (t * T + r0 - HALO)) >= 0
        side, rb = divmod(c, N_ROW_BLOCKS)
        srows = slice(rb * RB, (rb + 1) * RB)
        scores, raws, outs_t = [], [], []
        for n, (hk, par) in enumerate(units):
            w0 = SIDE_OFF[side] + n * side_block
            raws.append(_dot(h_scr[srows, 0:D_MODEL], wa_ref[:, w0:w0 + side_block]))
            if prev is not None:
                outs_t.append(value_matmul(prev, n))
            scores.append(_dot(k_scr[2 * hk + par, keys, :], qt_scr[hk, c]))
        if prev is not None:
            gate_and_store(prev[0], outs_t)

        raw = jnp.concatenate(raws, axis=1)
        if side == SIDE_U:
            uz_scr[srows, :] = _gelu_exact(raw)
        elif side == SIDE_VS:
            vs = _gelu_exact(raw)
            mu = jnp.mean(vs, axis=-1, keepdims=True)
            vc = vs - mu
            var = jnp.mean(vc * vc, axis=-1, keepdims=True)
            vn_scr[srows, 0:D_MODEL] = (vc * jax.lax.rsqrt(var + EPS) * lng_ref[...]
                                        + lnb_ref[...]).astype(bf16)
        elif side == SIDE_ZS:
            uz_scr[srows, :] = uz_scr[srows, :] * _silu(raw)
        else:
            ga_scr[srows, :] = _sigmoid(raw + bm_ref[:, 0:1024])

        probs, recips = [], []
        for (hk, par), s in zip(units, scores):
            if r0 < HALO:
                s = jnp.where(valid, s, MASK_VALUE)
            sink = sink_ref[2 * hk + par, 0:1, :] * LOG2E
            m = jnp.maximum(jnp.max(s, axis=0, keepdims=True), sink)
            p = jnp.exp2(s - m)
            denom = jnp.sum(p, axis=0, keepdims=True) + jnp.exp2(sink - m)
            probs.append(p.astype(bf16))
            recips.append(1.0 / denom)
        prev = (c, probs, recips)

    att_first, outs_t = [], []
    for n in range(len(units)):
        w0 = OFF_WAO + n * side_block
        att_first.append(_dot(ag_scr[0:RB, 0:D_MODEL], wout_ref[:, w0:w0 + side_block]))
        outs_t.append(value_matmul(prev, n))
    gate_and_store(prev[0], outs_t)
    att_first = jnp.concatenate(att_first, axis=1)

    pos_i = jax.lax.broadcasted_iota(jnp.int32, (SG_BLOCK, SG_BLOCK), 0) // CHUNK
    pos_j = jax.lax.broadcasted_iota(jnp.int32, (SG_BLOCK, SG_BLOCK), 1) // CHUNK
    causal = pos_j <= pos_i
    nblk = T // SG_BLOCK
    gs_blocks = [[] for _ in range(N_ROW_BLOCKS)]
    assert SG_GROUPS == N_ROW_BLOCKS * (D_MODEL // side_block)
    for g in range(SG_GROUPS):
        grb, n = divmod(g, D_MODEL // side_block)
        gs_blocks[grb].append(_dot(h_scr[grb * RB:(grb + 1) * RB, 0:D_MODEL],
                                   wa_ref[:, OFF_GS + n * side_block:OFF_GS + (n + 1) * side_block]))
        wg = jnp.where(causal, sgw_ref[g], 0.0).astype(bf16)
        c0 = g * LANES
        rhs = jnp.concatenate(
            [vn_scr[b * SG_BLOCK:(b + 1) * SG_BLOCK, c0:c0 + LANES] for b in range(nblk)], axis=1)
        mixed = _dot(wg, rhs)
        for b in range(nblk):
            brows = slice(b * SG_BLOCK, (b + 1) * SG_BLOCK)
            blk = mixed[:, b * LANES:(b + 1) * LANES] + sgb_ref[g]
            sg_scr[brows, c0:c0 + LANES] = (uz_scr[brows, c0:c0 + LANES] * blk).astype(bf16)

    blocks = [slice(rb * RB, (rb + 1) * RB) for rb in range(N_ROW_BLOCKS)]
    att = [att_first] + [_dot(ag_scr[rows, 0:D_MODEL], wout_ref[:, OFF_WAO:OFF_WAO + D_MODEL])
                         for rows in blocks[1:]]
    sgo = [_dot(sg_scr[rows, 0:D_MODEL], wout_ref[:, OFF_WSO:OFF_WSO + D_MODEL]) for rows in blocks]
    ys = []
    for rb, rows in enumerate(blocks):
        gs = _sigmoid(jnp.concatenate(gs_blocks[rb], axis=1) + bm_ref[:, 1024:2048])
        ys.append((ga_scr[rows, :] * att[rb] + gs * sgo[rb]).astype(bf16))
    for rb, rows in enumerate(blocks):
        xo = x_ref[0, rows, :] + _dot(ys[rb], wout_ref[:, OFF_WO:OFF_WO + D_MODEL])
        ms2 = jnp.mean(xo * xo, axis=-1, keepdims=True)
        out_ref[0, rows, :] = xo * jax.lax.rsqrt(ms2 + EPS) * fg_ref[...]


def _rope_tables(seq, tile):
    lane = np.arange(LANES) % HEAD_DIM
    inv_freq = ROPE_THETA ** (-(np.arange(ROPE_HALF, dtype=np.float64) * 2.0) / ROPE_DIM)
    freq = np.where(lane < ROPE_DIM, inv_freq[lane % ROPE_HALF], 0.0)
    sign = np.where(lane < ROPE_HALF, -1.0, 1.0)

    def tables(pos):
        ang = pos.astype(np.float64)[:, None] * freq[None, :]
        return np.stack([np.cos(ang), np.sin(ang), np.sin(ang) * sign], axis=0)

    row_tab = tables(np.arange(tile))
    tile_tab = np.zeros((seq // tile, 8, LANES))
    tile_tab[:, 0:3, :] = np.transpose(tables(np.arange(seq // tile) * tile), (1, 0, 2))
    return jnp.asarray(row_tab, jnp.float32), jnp.asarray(tile_tab, jnp.float32)


@jax.jit
def kernel(x, norm_g, w_in, b_merge, att_sinks, sg_w, sg_b, sg_ln_g, sg_ln_b,
           w_att_out, w_sg_out, w_o, final_g):
    B, S, D = x.shape
    T = SEQ_TILE
    assert D == D_MODEL and S % T == 0 and w_in.shape == (1, D, IN_WIDTH)
    bf16 = jnp.bfloat16
    f32 = jnp.float32

    row_tab, tile_tab = _rope_tables(S, T)
    sinks = att_sinks[0].astype(f32).reshape(ATT_KV_HEADS, PAIRS_PER_KV, 2)
    sink_rows = jnp.transpose(sinks, (0, 2, 1)).reshape(4, 1, PAIRS_PER_KV, 1)
    sink_rows = jnp.broadcast_to(sink_rows, (4, 8, PAIRS_PER_KV, CHUNK)).reshape(4, 8, STACK_ROWS)
    sgb = jnp.broadcast_to(sg_b[0].astype(f32)[:, :, None], (SG_GROUPS, SG_BLOCK, LANES))

    hbm = pl.BlockSpec(memory_space=pl.ANY)

    def const(shape):
        zeros = (0,) * len(shape)
        return pl.BlockSpec(shape, lambda b, t: zeros, pipeline_mode=pl.Buffered(1))

    grid_spec = pltpu.PrefetchScalarGridSpec(
        num_scalar_prefetch=0,
        grid=(B, S // T),
        in_specs=[
            pl.BlockSpec((1, T, D), lambda b, t: (b, t, 0)),
            const((1, D)),
            hbm, hbm, hbm, hbm,
            const((1, 2 * D)),
            const((4, 8, STACK_ROWS)),
            const((SG_GROUPS, SG_BLOCK, SG_BLOCK)),
            const((SG_GROUPS, SG_BLOCK, LANES)),
            const((1, D)),
            const((1, D)),
            const((1, D)),
            const((3, T, LANES)),
            const((S // T, 8, LANES)),
        ],
        out_specs=pl.BlockSpec((1, T, D), lambda b, t: (b, t, 0)),
        scratch_shapes=[
            pltpu.VMEM((T, PAD_WIDTH), bf16),
            pltpu.VMEM((ATT_KV_HEADS, N_CHUNKS, LANES, STACK_ROWS), bf16),
            pltpu.VMEM((T, D), f32),
            pltpu.VMEM((4, HALO + T, LANES), bf16),
            pltpu.VMEM((HALO + T, LANES), f32),
            pltpu.VMEM((2, ATT_KV_HEADS, HEAD_DIM, HALO + T), bf16),
            pltpu.VMEM((T, PAD_WIDTH), bf16),
            pltpu.VMEM((T, PAD_WIDTH), bf16),
            pltpu.VMEM((T, PAD_WIDTH), bf16),
            pltpu.VMEM((T, D), f32),
            pltpu.VMEM((T, D), f32),
            pltpu.VMEM((D, IN_WIDTH), bf16),
            pltpu.VMEM((D, OUT_W_WIDTH), bf16),
            pltpu.VMEM((LOAD_SLOTS, LOAD_ROWS, IN_WIDTH), f32),
            pltpu.VMEM((LOAD_SLOTS, 3, LOAD_ROWS, D), f32),
            pltpu.SemaphoreType.DMA((LOAD_SLOTS, 4)),
        ],
    )
    return pl.pallas_call(
        _block_kernel,
        grid_spec=grid_spec,
        out_shape=jax.ShapeDtypeStruct((B, S, D), x.dtype),
        compiler_params=pltpu.CompilerParams(
            dimension_semantics=("arbitrary", "arbitrary"),
            vmem_limit_bytes=VMEM_LIMIT_BYTES,
        ),
        name="hybrid_block",
    )(
        x, norm_g.astype(f32), w_in.astype(f32), w_att_out.astype(f32), w_sg_out.astype(f32),
        w_o.astype(f32), b_merge.astype(f32), sink_rows,
        sg_w[0].astype(f32), sgb, sg_ln_g.astype(f32), sg_ln_b.astype(f32),
        final_g.reshape(1, D).astype(f32), row_tab, tile_tab,
    )
```

```python
import numpy as np
import jax
import jax.numpy as jnp
from jax.experimental import pallas as pl
from jax.experimental.pallas import tpu as pltpu

D_MODEL = 1024
CHUNK = 64
EPS = 1e-6
ATT_HEADS = 16
ATT_KV_HEADS = 2
HEAD_DIM = 64
ATT_GROUP = ATT_HEADS // ATT_KV_HEADS
WINDOW_CHUNKS = 2
HALO = WINDOW_CHUNKS * CHUNK
KEY_BLOCK = HALO + CHUNK
ROPE_DIM = HEAD_DIM // 4
ROPE_HALF = ROPE_DIM // 2
ROPE_THETA = 500000.0
SG_BLOCK = 128
SG_GROUPS = 8
LANES = 128
PAIRS_PER_KV = ATT_GROUP // 2
STACK_ROWS = PAIRS_PER_KV * CHUNK
MASK_VALUE = -1e30

OFF_Q, OFF_K, OFF_V, OFF_ZA = 0, 1024, 1152, 1280
OFF_U, OFF_VS, OFF_ZS, OFF_GM = 2304, 3328, 4352, 5376
IN_WIDTH = 7424
N_SIDE = 4
SIDE_U, SIDE_VS, SIDE_ZS, SIDE_GA = range(N_SIDE)
SIDE_OFF = (OFF_U, OFF_VS, OFF_ZS, OFF_GM)
OFF_GS = OFF_GM + 1024
OFF_WAO, OFF_WSO, OFF_WO = 0, 1024, 2048
PAD_WIDTH = D_MODEL + LANES
OUT_W_WIDTH = 3 * D_MODEL + LANES
LOG2E = np.float32(1.4426950408889634)
LOAD_ROWS = 64
N_LOADS = D_MODEL // LOAD_ROWS
LOAD_SLOTS = 3
assert N_LOADS >= LOAD_SLOTS

SEQ_TILE = 512
ROW_BLOCK = 256
N_ROW_BLOCKS = SEQ_TILE // ROW_BLOCK
N_CHUNKS = SEQ_TILE // CHUNK
VMEM_LIMIT_BYTES = 58 * 1024 * 1024

assert N_CHUNKS == N_SIDE * N_ROW_BLOCKS


def _dot(a, b):
    return jnp.dot(a, b, preferred_element_type=jnp.float32)


def _sigmoid(x):
    return 0.5 * jnp.tanh(0.5 * x) + 0.5


def _silu(x):
    hx = 0.5 * x
    return hx * (1.0 + jnp.tanh(hx))


def _gelu_exact(x):
    return 0.5 * x * (1.0 + jax.lax.erf(x * np.float32(1.0 / np.sqrt(2.0))))


def _load_weights(w_in_hbm, w_out_hbm, wa_ref, wout_ref, stage_in, stage_out, sems):
    n_out = len(w_out_hbm)
    ahead = LOAD_SLOTS - 1

    def copies(i, slot):
        rows = pl.ds(pl.multiple_of(i * LOAD_ROWS, LOAD_ROWS), LOAD_ROWS)
        cps = [pltpu.make_async_copy(w_in_hbm.at[0, rows, :], stage_in.at[slot], sems.at[slot, 0])]
        for k in range(n_out):
            cps.append(pltpu.make_async_copy(w_out_hbm[k].at[0, rows, :], stage_out.at[slot, k],
                                             sems.at[slot, 1 + k]))
        return cps

    for i in range(ahead):
        for cp in copies(i, i):
            cp.start()

    def body(i, carry):
        slot = i % LOAD_SLOTS

        @pl.when(i + ahead < N_LOADS)
        def _():
            for cp in copies(i + ahead, (i + ahead) % LOAD_SLOTS):
                cp.start()

        for cp in copies(i, slot):
            cp.wait()
        rows = pl.ds(pl.multiple_of(i * LOAD_ROWS, LOAD_ROWS), LOAD_ROWS)
        wa_ref[rows, :] = stage_in[slot].astype(jnp.bfloat16)
        for k in range(n_out):
            wout_ref[rows, k * D_MODEL:(k + 1) * D_MODEL] = stage_out[slot, k].astype(jnp.bfloat16)
        return carry

    jax.lax.fori_loop(0, N_LOADS, body, 0)


def _block_kernel(x_ref, ng_ref, w_in_hbm, wao_hbm, wso_hbm, wo_hbm, bm_ref, sink_ref, sgw_ref, sgb_ref,
                  lng_ref, lnb_ref, fg_ref,
                  rrow_ref, rtile_ref,
                  out_ref,
                  h_scr, qt_scr, za_scr, k_scr, v_scr, vt_scr, ag_scr, sg_scr, vn_scr, uz_scr, ga_scr,
                  wa_ref, wout_ref, stage_in, stage_out, load_sems):
    t = pl.program_id(1)
    T = SEQ_TILE
    RB = ROW_BLOCK
    bf16 = jnp.bfloat16

    @pl.when((pl.program_id(0) == 0) & (t == 0))
    def _():
        _load_weights(w_in_hbm, (wao_hbm, wso_hbm, wo_hbm), wa_ref, wout_ref,
                      stage_in, stage_out, load_sems)

    @pl.when(t == 0)
    def _():
        k_scr[:, 0:HALO, :] = jnp.zeros((4, HALO, LANES), bf16)
        v_scr[0:HALO, :] = jnp.zeros((HALO, LANES), jnp.float32)

    @pl.when(t != 0)
    def _():
        k_scr[:, 0:HALO, :] = k_scr[:, T:T + HALO, :]
        v_scr[0:HALO, :] = v_scr[T:T + HALO, :]

    lane = jax.lax.broadcasted_iota(jnp.int32, (RB, LANES), 1)
    first_half = (lane % HEAD_DIM) < ROPE_HALF
    low_half = lane < HEAD_DIM
    scale = np.float32(HEAD_DIM ** -0.5) * LOG2E

    for rb in range(N_ROW_BLOCKS):
        rows = slice(rb * RB, (rb + 1) * RB)
        x = x_ref[0, rows, :]
        ms = jnp.mean(x * x, axis=-1, keepdims=True)
        h = (x * jax.lax.rsqrt(ms + EPS) * ng_ref[...]).astype(bf16)
        h_scr[rows, 0:D_MODEL] = h

        cb, sb, sb_sgn = rrow_ref[0, rows, :], rrow_ref[1, rows, :], rrow_ref[2, rows, :]
        ca, sa, sa_sgn = rtile_ref[t, 0:1, :], rtile_ref[t, 1:2, :], rtile_ref[t, 2:3, :]
        cos_t = ca * cb - sa * sb
        sin_t = sa_sgn * cb + ca * sb_sgn

        def rope(v):
            nxt = pltpu.roll(v, LANES - ROPE_HALF, 1)
            prv = pltpu.roll(v, ROPE_HALF, 1)
            return v * cos_t + jnp.where(first_half, nxt, prv) * sin_t

        kv2 = _dot(h, wa_ref[:, OFF_K:OFF_V + LANES])
        k2 = rope(kv2[:, 0:LANES])
        zeros = jnp.zeros_like(k2)
        krows = slice(HALO + rb * RB, HALO + (rb + 1) * RB)
        h0_lo = jnp.where(low_half, k2, zeros)
        h1_hi = jnp.where(low_half, zeros, k2)
        h0_hi = pltpu.roll(h0_lo, HEAD_DIM, 1)
        h1_lo = pltpu.roll(h1_hi, HEAD_DIM, 1)
        k_scr[0, krows, :] = h0_lo.astype(bf16)
        k_scr[1, krows, :] = h0_hi.astype(bf16)
        k_scr[2, krows, :] = h1_lo.astype(bf16)
        k_scr[3, krows, :] = h1_hi.astype(bf16)
        v_scr[krows, :] = kv2[:, LANES:2 * LANES]

        q = _dot(h, wa_ref[:, OFF_Q:OFF_Q + 1024])
        q_tiles = [rope(q[:, j * LANES:(j + 1) * LANES]) * scale for j in range(1024 // LANES)]
        for hk in range(ATT_KV_HEADS):
            for cc in range(RB // CHUNK):
                stacked = jnp.concatenate(
                    [q_tiles[hk * PAIRS_PER_KV + j][cc * CHUNK:(cc + 1) * CHUNK] for j in range(PAIRS_PER_KV)],
                    axis=0)
                qt_scr[hk, rb * (RB // CHUNK) + cc] = stacked.T.astype(bf16)
        za_scr[rows, :] = _silu(_dot(h, wa_ref[:, OFF_ZA:OFF_ZA + 1024]))

    for a in range(2):
        vt = v_scr[a * CHUNK:HALO + T, :].T
        for hk in range(ATT_KV_HEADS):
            vt_scr[a, hk, :, 0:HALO + T - a * CHUNK] = vt[hk * HEAD_DIM:(hk + 1) * HEAD_DIM].astype(bf16)

    key_off = jax.lax.broadcasted_iota(jnp.int32, (KEY_BLOCK, 1), 0)
    units = [(hk, par) for hk in range(ATT_KV_HEADS) for par in range(2)]
    side_block = D_MODEL // len(units)

    def value_matmul(prev, n):
        pc, probs, recips = prev
        hk = units[n][0]
        vkeys = slice((pc // 2) * LANES, (pc // 2) * LANES + KEY_BLOCK)
        return _dot(vt_scr[pc % 2, hk, :, vkeys], probs[n]) * recips[n]

    def gate_and_store(pc, outs_t):
        prow = slice(pc * CHUNK, (pc + 1) * CHUNK)
        for hk in range(ATT_KV_HEADS):
            o = jnp.concatenate([outs_t[2 * hk], outs_t[2 * hk + 1]], axis=0).T
            for j in range(PAIRS_PER_KV):
                c0 = (hk * PAIRS_PER_KV + j) * LANES
                gated = o[j * CHUNK:(j + 1) * CHUNK] * za_scr[prow, c0:c0 + LANES]
                ag_scr[prow, c0:c0 + LANES] = gated.astype(bf16)

    prev = None
    for c in range(N_CHUNKS):
        r0 = c * CHUNK
        keys = slice(r0, r0 + KEY_BLOCK)
        valid = (key_off + (t * T + r0 - HALO)) >= 0
        side, rb = divmod(c, N_ROW_BLOCKS)
        srows = slice(rb * RB, (rb + 1) * RB)
        scores, raws, outs_t = [], [], []
        for n, (hk, par) in enumerate(units):
            w0 = SIDE_OFF[side] + n * side_block
            raws.append(_dot(h_scr[srows, 0:D_MODEL], wa_ref[:, w0:w0 + side_block]))
            if prev is not None:
                outs_t.append(value_matmul(prev, n))
            scores.append(_dot(k_scr[2 * hk + par, keys, :], qt_scr[hk, c]))
        if prev is not None:
            gate_and_store(prev[0], outs_t)

        raw = jnp.concatenate(raws, axis=1)
        if side == SIDE_U:
            uz_scr[srows, :] = _gelu_exact(raw)
        elif side == SIDE_VS:
            vs = _gelu_exact(raw)
            mu = jnp.mean(vs, axis=-1, keepdims=True)
            vc = vs - mu
            var = jnp.mean(vc * vc, axis=-1, keepdims=True)
            vn_scr[srows, 0:D_MODEL] = (vc * jax.lax.rsqrt(var + EPS) * lng_ref[...]
                                        + lnb_ref[...]).astype(bf16)
        elif side == SIDE_ZS:
            uz_scr[srows, :] = uz_scr[srows, :] * _silu(raw)
        else:
            ga_scr[srows, :] = _sigmoid(raw + bm_ref[:, 0:1024])

        probs, recips = [], []
        for (hk, par), s in zip(units, scores):
            if r0 < HALO:
                s = jnp.where(valid, s, MASK_VALUE)
            sink = sink_ref[2 * hk + par, 0:1, :] * LOG2E
            m = jnp.maximum(jnp.max(s, axis=0, keepdims=True), sink)
            p = jnp.exp2(s - m)
            denom = jnp.sum(p, axis=0, keepdims=True) + jnp.exp2(sink - m)
            probs.append(p.astype(bf16))
            recips.append(1.0 / denom)
        prev = (c, probs, recips)

    att_first, outs_t = [], []
    for n in range(len(units)):
        w0 = OFF_WAO + n * side_block
        att_first.append(_dot(ag_scr[0:RB, 0:D_MODEL], wout_ref[:, w0:w0 + side_block]))
        outs_t.append(value_matmul(prev, n))
    gate_and_store(prev[0], outs_t)
    att_first = jnp.concatenate(att_first, axis=1)

    pos_i = jax.lax.broadcasted_iota(jnp.int32, (SG_BLOCK, SG_BLOCK), 0) // CHUNK
    pos_j = jax.lax.broadcasted_iota(jnp.int32, (SG_BLOCK, SG_BLOCK), 1) // CHUNK
    causal = pos_j <= pos_i
    nblk = T // SG_BLOCK
    gs_blocks = [[] for _ in range(N_ROW_BLOCKS)]
    assert SG_GROUPS == N_ROW_BLOCKS * (D_MODEL // side_block)
    for g in range(SG_GROUPS):
        grb, n = divmod(g, D_MODEL // side_block)
        gs_blocks[grb].append(_dot(h_scr[grb * RB:(grb + 1) * RB, 0:D_MODEL],
                                   wa_ref[:, OFF_GS + n * side_block:OFF_GS + (n + 1) * side_block]))
        wg = jnp.where(causal, sgw_ref[g], 0.0).astype(bf16)
        c0 = g * LANES
        rhs = jnp.concatenate(
            [vn_scr[b * SG_BLOCK:(b + 1) * SG_BLOCK, c0:c0 + LANES] for b in range(nblk)], axis=1)
        mixed = _dot(wg, rhs)
        for b in range(nblk):
            brows = slice(b * SG_BLOCK, (b + 1) * SG_BLOCK)
            blk = mixed[:, b * LANES:(b + 1) * LANES] + sgb_ref[g]
            sg_scr[brows, c0:c0 + LANES] = (uz_scr[brows, c0:c0 + LANES] * blk).astype(bf16)

    att = jnp.concatenate(
        [att_first, _dot(ag_scr[RB:T, 0:D_MODEL], wout_ref[:, OFF_WAO:OFF_WAO + D_MODEL])], axis=0)
    gs_raw = jnp.concatenate([jnp.concatenate(gs_blocks[rb], axis=1) for rb in range(N_ROW_BLOCKS)], axis=0)
    gs = _sigmoid(gs_raw + bm_ref[:, 1024:2048])
    y = ga_scr[...] * att + gs * _dot(sg_scr[:, 0:D_MODEL], wout_ref[:, OFF_WSO:OFF_WSO + D_MODEL])
    xo = x_ref[0] + _dot(y.astype(bf16), wout_ref[:, OFF_WO:OFF_WO + D_MODEL])
    ms2 = jnp.mean(xo * xo, axis=-1, keepdims=True)
    out_ref[0] = xo * jax.lax.rsqrt(ms2 + EPS) * fg_ref[...]


def _rope_tables(seq, tile):
    lane = np.arange(LANES) % HEAD_DIM
    inv_freq = ROPE_THETA ** (-(np.arange(ROPE_HALF, dtype=np.float64) * 2.0) / ROPE_DIM)
    freq = np.where(lane < ROPE_DIM, inv_freq[lane % ROPE_HALF], 0.0)
    sign = np.where(lane < ROPE_HALF, -1.0, 1.0)

    def tables(pos):
        ang = pos.astype(np.float64)[:, None] * freq[None, :]
        return np.stack([np.cos(ang), np.sin(ang), np.sin(ang) * sign], axis=0)

    row_tab = tables(np.arange(tile))
    tile_tab = np.zeros((seq // tile, 8, LANES))
    tile_tab[:, 0:3, :] = np.transpose(tables(np.arange(seq // tile) * tile), (1, 0, 2))
    return jnp.asarray(row_tab, jnp.float32), jnp.asarray(tile_tab, jnp.float32)


@jax.jit
def kernel(x, norm_g, w_in, b_merge, att_sinks, sg_w, sg_b, sg_ln_g, sg_ln_b,
           w_att_out, w_sg_out, w_o, final_g):
    B, S, D = x.shape
    T = SEQ_TILE
    assert D == D_MODEL and S % T == 0 and w_in.shape == (1, D, IN_WIDTH)
    bf16 = jnp.bfloat16
    f32 = jnp.float32

    row_tab, tile_tab = _rope_tables(S, T)
    sinks = att_sinks[0].astype(f32).reshape(ATT_KV_HEADS, PAIRS_PER_KV, 2)
    sink_rows = jnp.repeat(jnp.transpose(sinks, (0, 2, 1)).reshape(4, PAIRS_PER_KV), CHUNK, axis=1)
    sink_rows = jnp.broadcast_to(sink_rows[:, None, :], (4, 8, STACK_ROWS))
    sgb = jnp.broadcast_to(sg_b[0].astype(f32)[:, :, None], (SG_GROUPS, SG_BLOCK, LANES))

    hbm = pl.BlockSpec(memory_space=pl.ANY)

    def const(shape):
        zeros = (0,) * len(shape)
        return pl.BlockSpec(shape, lambda b, t: zeros, pipeline_mode=pl.Buffered(1))

    grid_spec = pltpu.PrefetchScalarGridSpec(
        num_scalar_prefetch=0,
        grid=(B, S // T),
        in_specs=[
            pl.BlockSpec((1, T, D), lambda b, t: (b, t, 0)),
            const((1, D)),
            hbm, hbm, hbm, hbm,
            const((1, 2 * D)),
            const((4, 8, STACK_ROWS)),
            const((SG_GROUPS, SG_BLOCK, SG_BLOCK)),
            const((SG_GROUPS, SG_BLOCK, LANES)),
            const((1, D)),
            const((1, D)),
            const((1, D)),
            const((3, T, LANES)),
            const((S // T, 8, LANES)),
        ],
        out_specs=pl.BlockSpec((1, T, D), lambda b, t: (b, t, 0)),
        scratch_shapes=[
            pltpu.VMEM((T, PAD_WIDTH), bf16),
            pltpu.VMEM((ATT_KV_HEADS, N_CHUNKS, LANES, STACK_ROWS), bf16),
            pltpu.VMEM((T, D), f32),
            pltpu.VMEM((4, HALO + T, LANES), bf16),
            pltpu.VMEM((HALO + T, LANES), f32),
            pltpu.VMEM((2, ATT_KV_HEADS, HEAD_DIM, HALO + T), bf16),
            pltpu.VMEM((T, PAD_WIDTH), bf16),
            pltpu.VMEM((T, PAD_WIDTH), bf16),
            pltpu.VMEM((T, PAD_WIDTH), bf16),
            pltpu.VMEM((T, D), f32),
            pltpu.VMEM((T, D), f32),
            pltpu.VMEM((D, IN_WIDTH), bf16),
            pltpu.VMEM((D, OUT_W_WIDTH), bf16),
            pltpu.VMEM((LOAD_SLOTS, LOAD_ROWS, IN_WIDTH), f32),
            pltpu.VMEM((LOAD_SLOTS, 3, LOAD_ROWS, D), f32),
            pltpu.SemaphoreType.DMA((LOAD_SLOTS, 4)),
        ],
    )
    return pl.pallas_call(
        _block_kernel,
        grid_spec=grid_spec,
        out_shape=jax.ShapeDtypeStruct((B, S, D), x.dtype),
        compiler_params=pltpu.CompilerParams(
            dimension_semantics=("arbitrary", "arbitrary"),
            vmem_limit_bytes=VMEM_LIMIT_BYTES,
        ),
        name="hybrid_block",
    )(
        x, norm_g.astype(f32), w_in.astype(f32), w_att_out.astype(f32), w_sg_out.astype(f32),
        w_o.astype(f32), b_merge.astype(f32), sink_rows,
        sg_w[0].astype(f32), sgb, sg_ln_g.astype(f32), sg_ln_b.astype(f32),
        final_g.reshape(1, D).astype(f32), row_tab, tile_tab,
    )
```

```python
import numpy as np
import jax
import jax.numpy as jnp
from jax.experimental import pallas as pl
from jax.experimental.pallas import tpu as pltpu

D_MODEL = 1024
CHUNK = 64
EPS = 1e-6
ATT_HEADS = 16
ATT_KV_HEADS = 2
HEAD_DIM = 64
ATT_GROUP = ATT_HEADS // ATT_KV_HEADS
WINDOW_CHUNKS = 2
HALO = WINDOW_CHUNKS * CHUNK
KEY_BLOCK = HALO + CHUNK
ROPE_DIM = HEAD_DIM // 4
ROPE_HALF = ROPE_DIM // 2
ROPE_THETA = 500000.0
SG_BLOCK = 128
SG_GROUPS = 8
LANES = 128
PAIRS_PER_KV = ATT_GROUP // 2
STACK_ROWS = PAIRS_PER_KV * CHUNK
MASK_VALUE = -1e30

OFF_Q, OFF_K, OFF_V, OFF_ZA = 0, 1024, 1152, 1280
OFF_U, OFF_VS, OFF_ZS, OFF_GM = 2304, 3328, 4352, 5376
IN_WIDTH = 7424
N_SIDE = 4
OFF_GS = OFF_GM + 1024
SIDE_U, SIDE_GS, SIDE_ZS, SIDE_GA = range(N_SIDE)
SIDE_OFF = (OFF_U, OFF_GS, OFF_ZS, OFF_GM)
OFF_WAO, OFF_WSO, OFF_WO = 0, 1024, 2048
PAD_WIDTH = D_MODEL + LANES
OUT_W_WIDTH = 3 * D_MODEL + LANES
LOG2E = np.float32(1.4426950408889634)
LOAD_ROWS = 64
N_LOADS = D_MODEL // LOAD_ROWS
LOAD_SLOTS = 3
assert N_LOADS >= LOAD_SLOTS

SEQ_TILE = 512
ROW_BLOCK = 256
N_ROW_BLOCKS = SEQ_TILE // ROW_BLOCK
N_CHUNKS = SEQ_TILE // CHUNK
VMEM_LIMIT_BYTES = 58 * 1024 * 1024

assert N_CHUNKS == N_SIDE * N_ROW_BLOCKS


def _dot(a, b):
    return jnp.dot(a, b, preferred_element_type=jnp.float32)


def _sigmoid(x):
    return 0.5 * jnp.tanh(0.5 * x) + 0.5


def _silu(x):
    hx = 0.5 * x
    return hx * (1.0 + jnp.tanh(hx))


def _gelu_exact(x):
    return 0.5 * x * (1.0 + jax.lax.erf(x * np.float32(1.0 / np.sqrt(2.0))))


def _load_weights(w_in_hbm, w_out_hbm, wa_ref, wout_ref, stage_in, stage_out, sems):
    n_out = len(w_out_hbm)
    ahead = LOAD_SLOTS - 1

    def copies(i, slot):
        rows = pl.ds(pl.multiple_of(i * LOAD_ROWS, LOAD_ROWS), LOAD_ROWS)
        cps = [pltpu.make_async_copy(w_in_hbm.at[0, rows, :], stage_in.at[slot], sems.at[slot, 0])]
        for k in range(n_out):
            cps.append(pltpu.make_async_copy(w_out_hbm[k].at[0, rows, :], stage_out.at[slot, k],
                                             sems.at[slot, 1 + k]))
        return cps

    for i in range(ahead):
        for cp in copies(i, i):
            cp.start()

    def body(i, carry):
        slot = i % LOAD_SLOTS

        @pl.when(i + ahead < N_LOADS)
        def _():
            for cp in copies(i + ahead, (i + ahead) % LOAD_SLOTS):
                cp.start()

        for cp in copies(i, slot):
            cp.wait()
        rows = pl.ds(pl.multiple_of(i * LOAD_ROWS, LOAD_ROWS), LOAD_ROWS)
        wa_ref[rows, :] = stage_in[slot].astype(jnp.bfloat16)
        for k in range(n_out):
            wout_ref[rows, k * D_MODEL:(k + 1) * D_MODEL] = stage_out[slot, k].astype(jnp.bfloat16)
        return carry

    jax.lax.fori_loop(0, N_LOADS, body, 0)


def _block_kernel(x_ref, ng_ref, w_in_hbm, wao_hbm, wso_hbm, wo_hbm, bm_ref, sink_ref, sgw_ref, sgb_ref,
                  lng_ref, lnb_ref, fg_ref,
                  rrow_ref, rtile_ref,
                  out_ref,
                  h_scr, qt_scr, za_scr, k_scr, v_scr, vt_scr, ag_scr, sg_scr, vn_scr, uz_scr, ga_scr, gs_scr,
                  wa_ref, wout_ref, stage_in, stage_out, load_sems):
    t = pl.program_id(1)
    T = SEQ_TILE
    RB = ROW_BLOCK
    bf16 = jnp.bfloat16

    @pl.when((pl.program_id(0) == 0) & (t == 0))
    def _():
        _load_weights(w_in_hbm, (wao_hbm, wso_hbm, wo_hbm), wa_ref, wout_ref,
                      stage_in, stage_out, load_sems)

    @pl.when(t == 0)
    def _():
        k_scr[:, 0:HALO, :] = jnp.zeros((4, HALO, LANES), bf16)
        v_scr[0:HALO, :] = jnp.zeros((HALO, LANES), jnp.float32)

    @pl.when(t != 0)
    def _():
        k_scr[:, 0:HALO, :] = k_scr[:, T:T + HALO, :]
        v_scr[0:HALO, :] = v_scr[T:T + HALO, :]

    lane = jax.lax.broadcasted_iota(jnp.int32, (RB, LANES), 1)
    first_half = (lane % HEAD_DIM) < ROPE_HALF
    low_half = lane < HEAD_DIM
    scale = np.float32(HEAD_DIM ** -0.5) * LOG2E

    for rb in range(N_ROW_BLOCKS):
        rows = slice(rb * RB, (rb + 1) * RB)
        x = x_ref[0, rows, :]
        ms = jnp.mean(x * x, axis=-1, keepdims=True)
        h = (x * jax.lax.rsqrt(ms + EPS) * ng_ref[...]).astype(bf16)
        h_scr[rows, 0:D_MODEL] = h

        cb, sb, sb_sgn = rrow_ref[0, rows, :], rrow_ref[1, rows, :], rrow_ref[2, rows, :]
        ca, sa, sa_sgn = rtile_ref[t, 0:1, :], rtile_ref[t, 1:2, :], rtile_ref[t, 2:3, :]
        cos_t = ca * cb - sa * sb
        sin_t = sa_sgn * cb + ca * sb_sgn

        def rope(v):
            nxt = pltpu.roll(v, LANES - ROPE_HALF, 1)
            prv = pltpu.roll(v, ROPE_HALF, 1)
            return v * cos_t + jnp.where(first_half, nxt, prv) * sin_t

        kv2 = _dot(h, wa_ref[:, OFF_K:OFF_V + LANES])
        k2 = rope(kv2[:, 0:LANES])
        zeros = jnp.zeros_like(k2)
        krows = slice(HALO + rb * RB, HALO + (rb + 1) * RB)
        h0_lo = jnp.where(low_half, k2, zeros)
        h1_hi = jnp.where(low_half, zeros, k2)
        h0_hi = pltpu.roll(h0_lo, HEAD_DIM, 1)
        h1_lo = pltpu.roll(h1_hi, HEAD_DIM, 1)
        k_scr[0, krows, :] = h0_lo.astype(bf16)
        k_scr[1, krows, :] = h0_hi.astype(bf16)
        k_scr[2, krows, :] = h1_lo.astype(bf16)
        k_scr[3, krows, :] = h1_hi.astype(bf16)
        v_scr[krows, :] = kv2[:, LANES:2 * LANES]

        q = _dot(h, wa_ref[:, OFF_Q:OFF_Q + 1024])
        q_tiles = [rope(q[:, j * LANES:(j + 1) * LANES]) * scale for j in range(1024 // LANES)]
        for hk in range(ATT_KV_HEADS):
            for cc in range(RB // CHUNK):
                stacked = jnp.concatenate(
                    [q_tiles[hk * PAIRS_PER_KV + j][cc * CHUNK:(cc + 1) * CHUNK] for j in range(PAIRS_PER_KV)],
                    axis=0)
                qt_scr[hk, rb * (RB // CHUNK) + cc] = stacked.T.astype(bf16)
        za_scr[rows, :] = _silu(_dot(h, wa_ref[:, OFF_ZA:OFF_ZA + 1024]))

        vs = _gelu_exact(_dot(h, wa_ref[:, OFF_VS:OFF_VS + 1024]))
        mu = jnp.mean(vs, axis=-1, keepdims=True)
        vc = vs - mu
        var = jnp.mean(vc * vc, axis=-1, keepdims=True)
        vn_scr[rows, 0:D_MODEL] = (vc * jax.lax.rsqrt(var + EPS) * lng_ref[...]
                                   + lnb_ref[...]).astype(bf16)

    for a in range(2):
        vt = v_scr[a * CHUNK:HALO + T, :].T
        for hk in range(ATT_KV_HEADS):
            vt_scr[a, hk, :, 0:HALO + T - a * CHUNK] = vt[hk * HEAD_DIM:(hk + 1) * HEAD_DIM].astype(bf16)

    key_off = jax.lax.broadcasted_iota(jnp.int32, (KEY_BLOCK, 1), 0)
    units = [(hk, par) for hk in range(ATT_KV_HEADS) for par in range(2)]
    side_block = D_MODEL // len(units)

    def value_matmul(prev, n):
        pc, probs, recips = prev
        hk = units[n][0]
        vkeys = slice((pc // 2) * LANES, (pc // 2) * LANES + KEY_BLOCK)
        return _dot(vt_scr[pc % 2, hk, :, vkeys], probs[n]) * recips[n]

    def gate_and_store(pc, outs_t):
        prow = slice(pc * CHUNK, (pc + 1) * CHUNK)
        for hk in range(ATT_KV_HEADS):
            o = jnp.concatenate([outs_t[2 * hk], outs_t[2 * hk + 1]], axis=0).T
            for j in range(PAIRS_PER_KV):
                c0 = (hk * PAIRS_PER_KV + j) * LANES
                gated = o[j * CHUNK:(j + 1) * CHUNK] * za_scr[prow, c0:c0 + LANES]
                ag_scr[prow, c0:c0 + LANES] = gated.astype(bf16)

    prev = None
    for c in range(N_CHUNKS):
        r0 = c * CHUNK
        keys = slice(r0, r0 + KEY_BLOCK)
        valid = (key_off + (t * T + r0 - HALO)) >= 0
        side, rb = divmod(c, N_ROW_BLOCKS)
        srows = slice(rb * RB, (rb + 1) * RB)
        scores, raws, outs_t = [], [], []
        for n, (hk, par) in enumerate(units):
            w0 = SIDE_OFF[side] + n * side_block
            raws.append(_dot(h_scr[srows, 0:D_MODEL], wa_ref[:, w0:w0 + side_block]))
            if prev is not None:
                outs_t.append(value_matmul(prev, n))
            scores.append(_dot(k_scr[2 * hk + par, keys, :], qt_scr[hk, c]))
        if prev is not None:
            gate_and_store(prev[0], outs_t)

        raw = jnp.concatenate(raws, axis=1)
        if side == SIDE_U:
            uz_scr[srows, :] = _gelu_exact(raw)
        elif side == SIDE_GS:
            gs_scr[srows, :] = _sigmoid(raw + bm_ref[:, 1024:2048])
        elif side == SIDE_ZS:
            uz_scr[srows, :] = uz_scr[srows, :] * _silu(raw)
        else:
            ga_scr[srows, :] = _sigmoid(raw + bm_ref[:, 0:1024])

        probs, recips = [], []
        for (hk, par), s in zip(units, scores):
            if r0 < HALO:
                s = jnp.where(valid, s, MASK_VALUE)
            sink = sink_ref[2 * hk + par, 0:1, :] * LOG2E
            m = jnp.maximum(jnp.max(s, axis=0, keepdims=True), sink)
            p = jnp.exp2(s - m)
            denom = jnp.sum(p, axis=0, keepdims=True) + jnp.exp2(sink - m)
            probs.append(p.astype(bf16))
            recips.append(1.0 / denom)
        prev = (c, probs, recips)

    att_first, outs_t = [], []
    for n in range(len(units)):
        w0 = OFF_WAO + n * side_block
        att_first.append(_dot(ag_scr[0:RB, 0:D_MODEL], wout_ref[:, w0:w0 + side_block]))
        outs_t.append(value_matmul(prev, n))
    gate_and_store(prev[0], outs_t)
    att_first = jnp.concatenate(att_first, axis=1)

    pos_i = jax.lax.broadcasted_iota(jnp.int32, (SG_BLOCK, SG_BLOCK), 0) // CHUNK
    pos_j = jax.lax.broadcasted_iota(jnp.int32, (SG_BLOCK, SG_BLOCK), 1) // CHUNK
    causal = pos_j <= pos_i
    nblk = T // SG_BLOCK
    att_rest = []
    assert SG_GROUPS == 2 * (D_MODEL // side_block) and N_ROW_BLOCKS == 2
    for g in range(SG_GROUPS):
        if g % 2 == 0:
            w0 = OFF_WAO + (g // 2) * side_block
            att_rest.append(_dot(ag_scr[RB:T, 0:D_MODEL], wout_ref[:, w0:w0 + side_block]))
        wg = jnp.where(causal, sgw_ref[g], 0.0).astype(bf16)
        c0 = g * LANES
        rhs = jnp.concatenate(
            [vn_scr[b * SG_BLOCK:(b + 1) * SG_BLOCK, c0:c0 + LANES] for b in range(nblk)], axis=1)
        mixed = _dot(wg, rhs)
        for b in range(nblk):
            brows = slice(b * SG_BLOCK, (b + 1) * SG_BLOCK)
            blk = mixed[:, b * LANES:(b + 1) * LANES] + sgb_ref[g]
            sg_scr[brows, c0:c0 + LANES] = (uz_scr[brows, c0:c0 + LANES] * blk).astype(bf16)

    for rb in range(N_ROW_BLOCKS):
        rows = slice(rb * RB, (rb + 1) * RB)
        att = att_first if rb == 0 else jnp.concatenate(att_rest, axis=1)
        y = ga_scr[rows, :] * att
        y = y + gs_scr[rows, :] * _dot(sg_scr[rows, 0:D_MODEL], wout_ref[:, OFF_WSO:OFF_WSO + D_MODEL])
        xo = x_ref[0, rows, :] + _dot(y.astype(bf16), wout_ref[:, OFF_WO:OFF_WO + D_MODEL])
        ms2 = jnp.mean(xo * xo, axis=-1, keepdims=True)
        out_ref[0, rows, :] = xo * jax.lax.rsqrt(ms2 + EPS) * fg_ref[...]


def _rope_tables(seq, tile):
    lane = np.arange(LANES) % HEAD_DIM
    inv_freq = ROPE_THETA ** (-(np.arange(ROPE_HALF, dtype=np.float64) * 2.0) / ROPE_DIM)
    freq = np.where(lane < ROPE_DIM, inv_freq[lane % ROPE_HALF], 0.0)
    sign = np.where(lane < ROPE_HALF, -1.0, 1.0)

    def tables(pos):
        ang = pos.astype(np.float64)[:, None] * freq[None, :]
        return np.stack([np.cos(ang), np.sin(ang), np.sin(ang) * sign], axis=0)

    row_tab = tables(np.arange(tile))
    tile_tab = np.zeros((seq // tile, 8, LANES))
    tile_tab[:, 0:3, :] = np.transpose(tables(np.arange(seq // tile) * tile), (1, 0, 2))
    return jnp.asarray(row_tab, jnp.float32), jnp.asarray(tile_tab, jnp.float32)


@jax.jit
def kernel(x, norm_g, w_in, b_merge, att_sinks, sg_w, sg_b, sg_ln_g, sg_ln_b,
           w_att_out, w_sg_out, w_o, final_g):
    B, S, D = x.shape
    T = SEQ_TILE
    assert D == D_MODEL and S % T == 0 and w_in.shape == (1, D, IN_WIDTH)
    bf16 = jnp.bfloat16
    f32 = jnp.float32

    row_tab, tile_tab = _rope_tables(S, T)
    sinks = att_sinks[0].astype(f32).reshape(ATT_KV_HEADS, PAIRS_PER_KV, 2)
    sink_rows = jnp.repeat(jnp.transpose(sinks, (0, 2, 1)).reshape(4, PAIRS_PER_KV), CHUNK, axis=1)
    sink_rows = jnp.broadcast_to(sink_rows[:, None, :], (4, 8, STACK_ROWS))
    sgb = jnp.broadcast_to(sg_b[0].astype(f32)[:, :, None], (SG_GROUPS, SG_BLOCK, LANES))

    hbm = pl.BlockSpec(memory_space=pl.ANY)

    def const(shape):
        zeros = (0,) * len(shape)
        return pl.BlockSpec(shape, lambda b, t: zeros, pipeline_mode=pl.Buffered(1))

    grid_spec = pltpu.PrefetchScalarGridSpec(
        num_scalar_prefetch=0,
        grid=(B, S // T),
        in_specs=[
            pl.BlockSpec((1, T, D), lambda b, t: (b, t, 0)),
            const((1, D)),
            hbm, hbm, hbm, hbm,
            const((1, 2 * D)),
            const((4, 8, STACK_ROWS)),
            const((SG_GROUPS, SG_BLOCK, SG_BLOCK)),
            const((SG_GROUPS, SG_BLOCK, LANES)),
            const((1, D)),
            const((1, D)),
            const((1, D)),
            const((3, T, LANES)),
            const((S // T, 8, LANES)),
        ],
        out_specs=pl.BlockSpec((1, T, D), lambda b, t: (b, t, 0)),
        scratch_shapes=[
            pltpu.VMEM((T, PAD_WIDTH), bf16),
            pltpu.VMEM((ATT_KV_HEADS, N_CHUNKS, LANES, STACK_ROWS), bf16),
            pltpu.VMEM((T, D), f32),
            pltpu.VMEM((4, HALO + T, LANES), bf16),
            pltpu.VMEM((HALO + T, LANES), f32),
            pltpu.VMEM((2, ATT_KV_HEADS, HEAD_DIM, HALO + T), bf16),
            pltpu.VMEM((T, PAD_WIDTH), bf16),
            pltpu.VMEM((T, PAD_WIDTH), bf16),
            pltpu.VMEM((T, PAD_WIDTH), bf16),
            pltpu.VMEM((T, D), f32),
            pltpu.VMEM((T, D), f32),
            pltpu.VMEM((T, D), f32),
            pltpu.VMEM((D, IN_WIDTH), bf16),
            pltpu.VMEM((D, OUT_W_WIDTH), bf16),
            pltpu.VMEM((LOAD_SLOTS, LOAD_ROWS, IN_WIDTH), f32),
            pltpu.VMEM((LOAD_SLOTS, 3, LOAD_ROWS, D), f32),
            pltpu.SemaphoreType.DMA((LOAD_SLOTS, 4)),
        ],
    )
    return pl.pallas_call(
        _block_kernel,
        grid_spec=grid_spec,
        out_shape=jax.ShapeDtypeStruct((B, S, D), x.dtype),
        compiler_params=pltpu.CompilerParams(
            dimension_semantics=("arbitrary", "arbitrary"),
            vmem_limit_bytes=VMEM_LIMIT_BYTES,
        ),
        name="hybrid_block",
    )(
        x, norm_g.astype(f32), w_in.astype(f32), w_att_out.astype(f32), w_sg_out.astype(f32),
        w_o.astype(f32), b_merge.astype(f32), sink_rows,
        sg_w[0].astype(f32), sgb, sg_ln_g.astype(f32), sg_ln_b.astype(f32),
        final_g.reshape(1, D).astype(f32), row_tab, tile_tab,
    )
```

```python
import numpy as np
import jax
import jax.numpy as jnp
from jax.experimental import pallas as pl
from jax.experimental.pallas import tpu as pltpu

D_MODEL = 1024
CHUNK = 64
EPS = 1e-6
ATT_HEADS = 16
ATT_KV_HEADS = 2
HEAD_DIM = 64
ATT_GROUP = ATT_HEADS // ATT_KV_HEADS
WINDOW_CHUNKS = 2
HALO = WINDOW_CHUNKS * CHUNK
KEY_BLOCK = HALO + CHUNK
ROPE_DIM = HEAD_DIM // 4
ROPE_HALF = ROPE_DIM // 2
ROPE_THETA = 500000.0
SG_BLOCK = 128
SG_GROUPS = 8
LANES = 128
PAIRS_PER_KV = ATT_GROUP // 2
STACK_ROWS = PAIRS_PER_KV * CHUNK
MASK_VALUE = -1e30

OFF_Q, OFF_K, OFF_V, OFF_ZA = 0, 1024, 1152, 1280
OFF_U, OFF_VS, OFF_ZS, OFF_GM = 2304, 3328, 4352, 5376
IN_WIDTH = 7424
N_SIDE = 4
OFF_GS = OFF_GM + 1024
SIDE_U, SIDE_GS, SIDE_ZS, SIDE_GA = range(N_SIDE)
SIDE_OFF = (OFF_U, OFF_GS, OFF_ZS, OFF_GM)
OFF_WAO, OFF_WSO, OFF_WO = 0, 1024, 2048
PAD_WIDTH = D_MODEL + LANES
OUT_W_WIDTH = 3 * D_MODEL + LANES
LOG2E = np.float32(1.4426950408889634)
LOAD_ROWS = 64
N_LOADS = D_MODEL // LOAD_ROWS
LOAD_SLOTS = 3
assert N_LOADS >= LOAD_SLOTS

SEQ_TILE = 512
ROW_BLOCK = 256
N_ROW_BLOCKS = SEQ_TILE // ROW_BLOCK
N_CHUNKS = SEQ_TILE // CHUNK
VMEM_LIMIT_BYTES = 58 * 1024 * 1024

assert N_CHUNKS == N_SIDE * N_ROW_BLOCKS


def _dot(a, b):
    return jnp.dot(a, b, preferred_element_type=jnp.float32)


def _sigmoid(x):
    return 0.5 * jnp.tanh(0.5 * x) + 0.5


def _silu(x):
    hx = 0.5 * x
    return hx * (1.0 + jnp.tanh(hx))


def _gelu_exact(x):
    return 0.5 * x * (1.0 + jax.lax.erf(x * np.float32(1.0 / np.sqrt(2.0))))


def _load_weights(w_in_hbm, w_out_hbm, wa_ref, wout_ref, stage_in, stage_out, sems):
    n_out = len(w_out_hbm)
    ahead = LOAD_SLOTS - 1

    def copies(i, slot):
        rows = pl.ds(pl.multiple_of(i * LOAD_ROWS, LOAD_ROWS), LOAD_ROWS)
        cps = [pltpu.make_async_copy(w_in_hbm.at[0, rows, :], stage_in.at[slot], sems.at[slot, 0])]
        for k in range(n_out):
            cps.append(pltpu.make_async_copy(w_out_hbm[k].at[0, rows, :], stage_out.at[slot, k],
                                             sems.at[slot, 1 + k]))
        return cps

    for i in range(ahead):
        for cp in copies(i, i):
            cp.start()

    def body(i, carry):
        slot = i % LOAD_SLOTS

        @pl.when(i + ahead < N_LOADS)
        def _():
            for cp in copies(i + ahead, (i + ahead) % LOAD_SLOTS):
                cp.start()

        for cp in copies(i, slot):
            cp.wait()
        rows = pl.ds(pl.multiple_of(i * LOAD_ROWS, LOAD_ROWS), LOAD_ROWS)
        wa_ref[rows, :] = stage_in[slot].astype(jnp.bfloat16)
        for k in range(n_out):
            wout_ref[rows, k * D_MODEL:(k + 1) * D_MODEL] = stage_out[slot, k].astype(jnp.bfloat16)
        return carry

    jax.lax.fori_loop(0, N_LOADS, body, 0)


def _pre_norm(x, ng_ref):
    ms = jnp.mean(x * x, axis=-1, keepdims=True)
    return (x * jax.lax.rsqrt(ms + EPS) * ng_ref[...]).astype(jnp.bfloat16)


def _block_kernel(x_ref, xnext_ref, ng_ref, w_in_hbm, wao_hbm, wso_hbm, wo_hbm, bm_ref, sink_ref,
                  sgw_ref, sgb_ref,
                  lng_ref, lnb_ref, fg_ref,
                  rrow_ref, rtile_ref,
                  out_ref,
                  hnext_scr, h_scr, qt_scr, za_scr, k_scr, v_scr, vt_scr, ag_scr, sg_scr, vn_scr, uz_scr, ga_scr, gs_scr,
                  wa_ref, wout_ref, stage_in, stage_out, load_sems):
    t = pl.program_id(1)
    T = SEQ_TILE
    RB = ROW_BLOCK
    bf16 = jnp.bfloat16

    @pl.when((pl.program_id(0) == 0) & (t == 0))
    def _():
        _load_weights(w_in_hbm, (wao_hbm, wso_hbm, wo_hbm), wa_ref, wout_ref,
                      stage_in, stage_out, load_sems)
        hnext_scr[:, 0:D_MODEL] = _pre_norm(x_ref[0, 0:RB, :], ng_ref)

    @pl.when(t == 0)
    def _():
        k_scr[:, 0:HALO, :] = jnp.zeros((4, HALO, LANES), bf16)
        v_scr[0:HALO, :] = jnp.zeros((HALO, LANES), jnp.float32)

    @pl.when(t != 0)
    def _():
        k_scr[:, 0:HALO, :] = k_scr[:, T:T + HALO, :]
        v_scr[0:HALO, :] = v_scr[T:T + HALO, :]

    lane = jax.lax.broadcasted_iota(jnp.int32, (RB, LANES), 1)
    first_half = (lane % HEAD_DIM) < ROPE_HALF
    low_half = lane < HEAD_DIM
    scale = np.float32(HEAD_DIM ** -0.5) * LOG2E

    for rb in range(N_ROW_BLOCKS):
        rows = slice(rb * RB, (rb + 1) * RB)
        h = hnext_scr[:, 0:D_MODEL] if rb == 0 else _pre_norm(x_ref[0, rows, :], ng_ref)
        h_scr[rows, 0:D_MODEL] = h

        cb, sb, sb_sgn = rrow_ref[0, rows, :], rrow_ref[1, rows, :], rrow_ref[2, rows, :]
        ca, sa, sa_sgn = rtile_ref[t, 0:1, :], rtile_ref[t, 1:2, :], rtile_ref[t, 2:3, :]
        cos_t = ca * cb - sa * sb
        sin_t = sa_sgn * cb + ca * sb_sgn

        def rope(v):
            nxt = pltpu.roll(v, LANES - ROPE_HALF, 1)
            prv = pltpu.roll(v, ROPE_HALF, 1)
            return v * cos_t + jnp.where(first_half, nxt, prv) * sin_t

        kv2 = _dot(h, wa_ref[:, OFF_K:OFF_V + LANES])
        k2 = rope(kv2[:, 0:LANES])
        zeros = jnp.zeros_like(k2)
        krows = slice(HALO + rb * RB, HALO + (rb + 1) * RB)
        h0_lo = jnp.where(low_half, k2, zeros)
        h1_hi = jnp.where(low_half, zeros, k2)
        h0_hi = pltpu.roll(h0_lo, HEAD_DIM, 1)
        h1_lo = pltpu.roll(h1_hi, HEAD_DIM, 1)
        k_scr[0, krows, :] = h0_lo.astype(bf16)
        k_scr[1, krows, :] = h0_hi.astype(bf16)
        k_scr[2, krows, :] = h1_lo.astype(bf16)
        k_scr[3, krows, :] = h1_hi.astype(bf16)
        v_scr[krows, :] = kv2[:, LANES:2 * LANES]

        q = _dot(h, wa_ref[:, OFF_Q:OFF_Q + 1024])
        q_tiles = [rope(q[:, j * LANES:(j + 1) * LANES]) * scale for j in range(1024 // LANES)]
        for hk in range(ATT_KV_HEADS):
            for cc in range(RB // CHUNK):
                stacked = jnp.concatenate(
                    [q_tiles[hk * PAIRS_PER_KV + j][cc * CHUNK:(cc + 1) * CHUNK] for j in range(PAIRS_PER_KV)],
                    axis=0)
                qt_scr[hk, rb * (RB // CHUNK) + cc] = stacked.T.astype(bf16)
        za_scr[rows, :] = _silu(_dot(h, wa_ref[:, OFF_ZA:OFF_ZA + 1024]))

        vs = _gelu_exact(_dot(h, wa_ref[:, OFF_VS:OFF_VS + 1024]))
        mu = jnp.mean(vs, axis=-1, keepdims=True)
        vc = vs - mu
        var = jnp.mean(vc * vc, axis=-1, keepdims=True)
        vn_scr[rows, 0:D_MODEL] = (vc * jax.lax.rsqrt(var + EPS) * lng_ref[...]
                                   + lnb_ref[...]).astype(bf16)

    for a in range(2):
        vt = v_scr[a * CHUNK:HALO + T, :].T
        for hk in range(ATT_KV_HEADS):
            vt_scr[a, hk, :, 0:HALO + T - a * CHUNK] = vt[hk * HEAD_DIM:(hk + 1) * HEAD_DIM].astype(bf16)

    key_off = jax.lax.broadcasted_iota(jnp.int32, (KEY_BLOCK, 1), 0)
    units = [(hk, par) for hk in range(ATT_KV_HEADS) for par in range(2)]
    side_block = D_MODEL // len(units)

    def value_matmul(prev, n):
        pc, probs, recips = prev
        hk = units[n][0]
        vkeys = slice((pc // 2) * LANES, (pc // 2) * LANES + KEY_BLOCK)
        return _dot(vt_scr[pc % 2, hk, :, vkeys], probs[n]) * recips[n]

    def gate_and_store(pc, outs_t):
        prow = slice(pc * CHUNK, (pc + 1) * CHUNK)
        for hk in range(ATT_KV_HEADS):
            o = jnp.concatenate([outs_t[2 * hk], outs_t[2 * hk + 1]], axis=0).T
            for j in range(PAIRS_PER_KV):
                c0 = (hk * PAIRS_PER_KV + j) * LANES
                gated = o[j * CHUNK:(j + 1) * CHUNK] * za_scr[prow, c0:c0 + LANES]
                ag_scr[prow, c0:c0 + LANES] = gated.astype(bf16)

    prev = None
    for c in range(N_CHUNKS):
        r0 = c * CHUNK
        keys = slice(r0, r0 + KEY_BLOCK)
        valid = (key_off + (t * T + r0 - HALO)) >= 0
        side, rb = divmod(c, N_ROW_BLOCKS)
        srows = slice(rb * RB, (rb + 1) * RB)
        scores, raws, outs_t = [], [], []
        for n, (hk, par) in enumerate(units):
            w0 = SIDE_OFF[side] + n * side_block
            raws.append(_dot(h_scr[srows, 0:D_MODEL], wa_ref[:, w0:w0 + side_block]))
            if prev is not None:
                outs_t.append(value_matmul(prev, n))
            scores.append(_dot(k_scr[2 * hk + par, keys, :], qt_scr[hk, c]))
        if prev is not None:
            gate_and_store(prev[0], outs_t)

        raw = jnp.concatenate(raws, axis=1)
        if side == SIDE_U:
            uz_scr[srows, :] = _gelu_exact(raw)
        elif side == SIDE_GS:
            gs_scr[srows, :] = _sigmoid(raw + bm_ref[:, 1024:2048])
        elif side == SIDE_ZS:
            uz_scr[srows, :] = uz_scr[srows, :] * _silu(raw)
        else:
            ga_scr[srows, :] = _sigmoid(raw + bm_ref[:, 0:1024])

        probs, recips = [], []
        for (hk, par), s in zip(units, scores):
            if r0 < HALO:
                s = jnp.where(valid, s, MASK_VALUE)
            sink = sink_ref[2 * hk + par, 0:1, :] * LOG2E
            m = jnp.maximum(jnp.max(s, axis=0, keepdims=True), sink)
            p = jnp.exp2(s - m)
            denom = jnp.sum(p, axis=0, keepdims=True) + jnp.exp2(sink - m)
            probs.append(p.astype(bf16))
            recips.append(1.0 / denom)
        prev = (c, probs, recips)

    att_first, outs_t = [], []
    for n in range(len(units)):
        w0 = OFF_WAO + n * side_block
        att_first.append(_dot(ag_scr[0:RB, 0:D_MODEL], wout_ref[:, w0:w0 + side_block]))
        outs_t.append(value_matmul(prev, n))
    gate_and_store(prev[0], outs_t)
    att_first = jnp.concatenate(att_first, axis=1)

    pos_i = jax.lax.broadcasted_iota(jnp.int32, (SG_BLOCK, SG_BLOCK), 0) // CHUNK
    pos_j = jax.lax.broadcasted_iota(jnp.int32, (SG_BLOCK, SG_BLOCK), 1) // CHUNK
    causal = pos_j <= pos_i
    nblk = T // SG_BLOCK
    att_rest = []
    assert SG_GROUPS == 2 * (D_MODEL // side_block) and N_ROW_BLOCKS == 2
    for g in range(SG_GROUPS):
        if g % 2 == 0:
            w0 = OFF_WAO + (g // 2) * side_block
            att_rest.append(_dot(ag_scr[RB:T, 0:D_MODEL], wout_ref[:, w0:w0 + side_block]))
        wg = jnp.where(causal, sgw_ref[g], 0.0).astype(bf16)
        c0 = g * LANES
        rhs = jnp.concatenate(
            [vn_scr[b * SG_BLOCK:(b + 1) * SG_BLOCK, c0:c0 + LANES] for b in range(nblk)], axis=1)
        mixed = _dot(wg, rhs)
        for b in range(nblk):
            brows = slice(b * SG_BLOCK, (b + 1) * SG_BLOCK)
            blk = mixed[:, b * LANES:(b + 1) * LANES] + sgb_ref[g]
            sg_scr[brows, c0:c0 + LANES] = (uz_scr[brows, c0:c0 + LANES] * blk).astype(bf16)

    for rb in range(N_ROW_BLOCKS):
        rows = slice(rb * RB, (rb + 1) * RB)
        att = att_first if rb == 0 else jnp.concatenate(att_rest, axis=1)
        y = ga_scr[rows, :] * att
        y = y + gs_scr[rows, :] * _dot(sg_scr[rows, 0:D_MODEL], wout_ref[:, OFF_WSO:OFF_WSO + D_MODEL])
        xo = x_ref[0, rows, :] + _dot(y.astype(bf16), wout_ref[:, OFF_WO:OFF_WO + D_MODEL])
        ms2 = jnp.mean(xo * xo, axis=-1, keepdims=True)
        out_ref[0, rows, :] = xo * jax.lax.rsqrt(ms2 + EPS) * fg_ref[...]

    hnext_scr[:, 0:D_MODEL] = _pre_norm(xnext_ref[0], ng_ref)


def _rope_tables(seq, tile):
    lane = np.arange(LANES) % HEAD_DIM
    inv_freq = ROPE_THETA ** (-(np.arange(ROPE_HALF, dtype=np.float64) * 2.0) / ROPE_DIM)
    freq = np.where(lane < ROPE_DIM, inv_freq[lane % ROPE_HALF], 0.0)
    sign = np.where(lane < ROPE_HALF, -1.0, 1.0)

    def tables(pos):
        ang = pos.astype(np.float64)[:, None] * freq[None, :]
        return np.stack([np.cos(ang), np.sin(ang), np.sin(ang) * sign], axis=0)

    row_tab = tables(np.arange(tile))
    tile_tab = np.zeros((seq // tile, 8, LANES))
    tile_tab[:, 0:3, :] = np.transpose(tables(np.arange(seq // tile) * tile), (1, 0, 2))
    return jnp.asarray(row_tab, jnp.float32), jnp.asarray(tile_tab, jnp.float32)


@jax.jit
def kernel(x, norm_g, w_in, b_merge, att_sinks, sg_w, sg_b, sg_ln_g, sg_ln_b,
           w_att_out, w_sg_out, w_o, final_g):
    B, S, D = x.shape
    T = SEQ_TILE
    assert D == D_MODEL and S % T == 0 and w_in.shape == (1, D, IN_WIDTH)
    bf16 = jnp.bfloat16
    f32 = jnp.float32

    row_tab, tile_tab = _rope_tables(S, T)
    sinks = att_sinks[0].astype(f32).reshape(ATT_KV_HEADS, PAIRS_PER_KV, 2)
    sink_rows = jnp.repeat(jnp.transpose(sinks, (0, 2, 1)).reshape(4, PAIRS_PER_KV), CHUNK, axis=1)
    sink_rows = jnp.broadcast_to(sink_rows[:, None, :], (4, 8, STACK_ROWS))
    sgb = jnp.broadcast_to(sg_b[0].astype(f32)[:, :, None], (SG_GROUPS, SG_BLOCK, LANES))

    hbm = pl.BlockSpec(memory_space=pl.ANY)

    n_tiles = S // T

    def next_first_rows(b, t):
        flat = jnp.minimum(b * n_tiles + t + 1, B * n_tiles - 1)
        return (flat // n_tiles, (flat % n_tiles) * N_ROW_BLOCKS, 0)

    def const(shape):
        zeros = (0,) * len(shape)
        return pl.BlockSpec(shape, lambda b, t: zeros, pipeline_mode=pl.Buffered(1))

    grid_spec = pltpu.PrefetchScalarGridSpec(
        num_scalar_prefetch=0,
        grid=(B, S // T),
        in_specs=[
            pl.BlockSpec((1, T, D), lambda b, t: (b, t, 0)),
            pl.BlockSpec((1, ROW_BLOCK, D), next_first_rows),
            const((1, D)),
            hbm, hbm, hbm, hbm,
            const((1, 2 * D)),
            const((4, 8, STACK_ROWS)),
            const((SG_GROUPS, SG_BLOCK, SG_BLOCK)),
            const((SG_GROUPS, SG_BLOCK, LANES)),
            const((1, D)),
            const((1, D)),
            const((1, D)),
            const((3, T, LANES)),
            const((S // T, 8, LANES)),
        ],
        out_specs=pl.BlockSpec((1, T, D), lambda b, t: (b, t, 0)),
        scratch_shapes=[
            pltpu.VMEM((ROW_BLOCK, PAD_WIDTH), bf16),
            pltpu.VMEM((T, PAD_WIDTH), bf16),
            pltpu.VMEM((ATT_KV_HEADS, N_CHUNKS, LANES, STACK_ROWS), bf16),
            pltpu.VMEM((T, D), f32),
            pltpu.VMEM((4, HALO + T, LANES), bf16),
            pltpu.VMEM((HALO + T, LANES), f32),
            pltpu.VMEM((2, ATT_KV_HEADS, HEAD_DIM, HALO + T), bf16),
            pltpu.VMEM((T, PAD_WIDTH), bf16),
            pltpu.VMEM((T, PAD_WIDTH), bf16),
            pltpu.VMEM((T, PAD_WIDTH), bf16),
            pltpu.VMEM((T, D), f32),
            pltpu.VMEM((T, D), f32),
            pltpu.VMEM((T, D), f32),
            pltpu.VMEM((D, IN_WIDTH), bf16),
            pltpu.VMEM((D, OUT_W_WIDTH), bf16),
            pltpu.VMEM((LOAD_SLOTS, LOAD_ROWS, IN_WIDTH), f32),
            pltpu.VMEM((LOAD_SLOTS, 3, LOAD_ROWS, D), f32),
            pltpu.SemaphoreType.DMA((LOAD_SLOTS, 4)),
        ],
    )
    return pl.pallas_call(
        _block_kernel,
        grid_spec=grid_spec,
        out_shape=jax.ShapeDtypeStruct((B, S, D), x.dtype),
        compiler_params=pltpu.CompilerParams(
            dimension_semantics=("arbitrary", "arbitrary"),
            vmem_limit_bytes=VMEM_LIMIT_BYTES,
        ),
        name="hybrid_block",
    )(
        x, x, norm_g.astype(f32), w_in.astype(f32), w_att_out.astype(f32), w_sg_out.astype(f32),
        w_o.astype(f32), b_merge.astype(f32), sink_rows,
        sg_w[0].astype(f32), sgb, sg_ln_g.astype(f32), sg_ln_b.astype(f32),
        final_g.reshape(1, D).astype(f32), row_tab, tile_tab,
    )
```

```python
import numpy as np
import jax
import jax.numpy as jnp
from jax.experimental import pallas as pl
from jax.experimental.pallas import tpu as pltpu

D_MODEL = 1024
CHUNK = 64
EPS = 1e-6
ATT_HEADS = 16
ATT_KV_HEADS = 2
HEAD_DIM = 64
ATT_GROUP = ATT_HEADS // ATT_KV_HEADS
WINDOW_CHUNKS = 2
HALO = WINDOW_CHUNKS * CHUNK
KEY_BLOCK = HALO + CHUNK
ROPE_DIM = HEAD_DIM // 4
ROPE_HALF = ROPE_DIM // 2
ROPE_THETA = 500000.0
SG_BLOCK = 128
SG_GROUPS = 8
LANES = 128
PAIRS_PER_KV = ATT_GROUP // 2
STACK_ROWS = PAIRS_PER_KV * CHUNK
MASK_VALUE = -1e30

OFF_Q, OFF_K, OFF_V, OFF_ZA = 0, 1024, 1152, 1280
OFF_U, OFF_VS, OFF_ZS, OFF_GM = 2304, 3328, 4352, 5376
IN_WIDTH = 7424
N_SIDE = 4
OFF_GS = OFF_GM + 1024
SIDE_U, SIDE_GS, SIDE_ZS, SIDE_GA = range(N_SIDE)
SIDE_OFF = (OFF_U, OFF_GS, OFF_ZS, OFF_GM)
OFF_WAO, OFF_WSO, OFF_WO = 0, 1024, 2048
PAD_WIDTH = D_MODEL + LANES
OUT_W_WIDTH = 3 * D_MODEL + LANES
LOG2E = np.float32(1.4426950408889634)
LOAD_ROWS = 64
N_LOADS = D_MODEL // LOAD_ROWS
LOAD_SLOTS = 3
assert N_LOADS >= LOAD_SLOTS

SEQ_TILE = 512
ROW_BLOCK = 256
N_ROW_BLOCKS = SEQ_TILE // ROW_BLOCK
N_CHUNKS = SEQ_TILE // CHUNK
VMEM_LIMIT_BYTES = 58 * 1024 * 1024

assert N_CHUNKS == N_SIDE * N_ROW_BLOCKS


def _dot(a, b):
    return jnp.dot(a, b, preferred_element_type=jnp.float32)


def _sigmoid(x):
    return 0.5 * jnp.tanh(0.5 * x) + 0.5


def _silu(x):
    hx = 0.5 * x
    return hx * (1.0 + jnp.tanh(hx))


def _gelu_exact(x):
    return 0.5 * x * (1.0 + jax.lax.erf(x * np.float32(1.0 / np.sqrt(2.0))))


def _load_weights(w_in_hbm, w_out_hbm, wa_ref, wout_ref, stage_in, stage_out, sems):
    n_out = len(w_out_hbm)
    ahead = LOAD_SLOTS - 1

    def copies(i, slot):
        rows = pl.ds(pl.multiple_of(i * LOAD_ROWS, LOAD_ROWS), LOAD_ROWS)
        cps = [pltpu.make_async_copy(w_in_hbm.at[0, rows, :], stage_in.at[slot], sems.at[slot, 0])]
        for k in range(n_out):
            cps.append(pltpu.make_async_copy(w_out_hbm[k].at[0, rows, :], stage_out.at[slot, k],
                                             sems.at[slot, 1 + k]))
        return cps

    for i in range(ahead):
        for cp in copies(i, i):
            cp.start()

    def body(i, carry):
        slot = i % LOAD_SLOTS

        @pl.when(i + ahead < N_LOADS)
        def _():
            for cp in copies(i + ahead, (i + ahead) % LOAD_SLOTS):
                cp.start()

        for cp in copies(i, slot):
            cp.wait()
        rows = pl.ds(pl.multiple_of(i * LOAD_ROWS, LOAD_ROWS), LOAD_ROWS)
        wa_ref[rows, :] = stage_in[slot].astype(jnp.bfloat16)
        for k in range(n_out):
            wout_ref[rows, k * D_MODEL:(k + 1) * D_MODEL] = stage_out[slot, k].astype(jnp.bfloat16)
        return carry

    jax.lax.fori_loop(0, N_LOADS, body, 0)


def _pre_norm(x, ng_ref):
    ms = jnp.mean(x * x, axis=-1, keepdims=True)
    return (x * jax.lax.rsqrt(ms + EPS) * ng_ref[...]).astype(jnp.bfloat16)


def _block_kernel(x_ref, xnext_ref, ng_ref, w_in_hbm, wao_hbm, wso_hbm, wo_hbm, bm_ref, sink_ref,
                  sgw_ref, sgb_ref,
                  lng_ref, lnb_ref, fg_ref,
                  rrow_ref, rtile_ref,
                  out_ref,
                  hnext_scr, h_scr, qt_scr, za_scr, k_scr, v_scr, vt_scr, ag_scr, sg_scr, vn_scr, uz_scr, ga_scr, gs_scr,
                  wa_ref, wout_ref, stage_in, stage_out, load_sems):
    t = pl.program_id(1)
    T = SEQ_TILE
    RB = ROW_BLOCK
    bf16 = jnp.bfloat16

    @pl.when((pl.program_id(0) == 0) & (t == 0))
    def _():
        _load_weights(w_in_hbm, (wao_hbm, wso_hbm, wo_hbm), wa_ref, wout_ref,
                      stage_in, stage_out, load_sems)
        hnext_scr[:, 0:D_MODEL] = _pre_norm(x_ref[0, 0:RB, :], ng_ref)

    @pl.when(t == 0)
    def _():
        k_scr[:, 0:HALO, :] = jnp.zeros((4, HALO, LANES), bf16)
        v_scr[0:HALO, :] = jnp.zeros((HALO, LANES), jnp.float32)

    @pl.when(t != 0)
    def _():
        k_scr[:, 0:HALO, :] = k_scr[:, T:T + HALO, :]
        v_scr[0:HALO, :] = v_scr[T:T + HALO, :]

    lane = jax.lax.broadcasted_iota(jnp.int32, (RB, LANES), 1)
    first_half = (lane % HEAD_DIM) < ROPE_HALF
    low_half = lane < HEAD_DIM
    scale = np.float32(HEAD_DIM ** -0.5) * LOG2E

    for rb in range(N_ROW_BLOCKS):
        rows = slice(rb * RB, (rb + 1) * RB)
        h = hnext_scr[:, 0:D_MODEL] if rb == 0 else _pre_norm(x_ref[0, rows, :], ng_ref)
        h_scr[rows, 0:D_MODEL] = h

        cb, sb, sb_sgn = rrow_ref[0, rows, :], rrow_ref[1, rows, :], rrow_ref[2, rows, :]
        ca, sa, sa_sgn = rtile_ref[t, 0:1, :], rtile_ref[t, 1:2, :], rtile_ref[t, 2:3, :]
        cos_t = ca * cb - sa * sb
        sin_t = sa_sgn * cb + ca * sb_sgn

        def rope(v):
            nxt = pltpu.roll(v, LANES - ROPE_HALF, 1)
            prv = pltpu.roll(v, ROPE_HALF, 1)
            return v * cos_t + jnp.where(first_half, nxt, prv) * sin_t

        vs = _gelu_exact(_dot(h, wa_ref[:, OFF_VS:OFF_VS + 1024]))
        mu = jnp.mean(vs, axis=-1, keepdims=True)
        vc = vs - mu
        var = jnp.mean(vc * vc, axis=-1, keepdims=True)
        vn_scr[rows, 0:D_MODEL] = (vc * jax.lax.rsqrt(var + EPS) * lng_ref[...]
                                   + lnb_ref[...]).astype(bf16)

        kv2 = _dot(h, wa_ref[:, OFF_K:OFF_V + LANES])
        k2 = rope(kv2[:, 0:LANES])
        zeros = jnp.zeros_like(k2)
        krows = slice(HALO + rb * RB, HALO + (rb + 1) * RB)
        h0_lo = jnp.where(low_half, k2, zeros)
        h1_hi = jnp.where(low_half, zeros, k2)
        h0_hi = pltpu.roll(h0_lo, HEAD_DIM, 1)
        h1_lo = pltpu.roll(h1_hi, HEAD_DIM, 1)
        k_scr[0, krows, :] = h0_lo.astype(bf16)
        k_scr[1, krows, :] = h0_hi.astype(bf16)
        k_scr[2, krows, :] = h1_lo.astype(bf16)
        k_scr[3, krows, :] = h1_hi.astype(bf16)
        v_scr[krows, :] = kv2[:, LANES:2 * LANES]

        q = _dot(h, wa_ref[:, OFF_Q:OFF_Q + 1024])
        q_tiles = [rope(q[:, j * LANES:(j + 1) * LANES]) * scale for j in range(1024 // LANES)]
        for hk in range(ATT_KV_HEADS):
            for cc in range(RB // CHUNK):
                stacked = jnp.concatenate(
                    [q_tiles[hk * PAIRS_PER_KV + j][cc * CHUNK:(cc + 1) * CHUNK] for j in range(PAIRS_PER_KV)],
                    axis=0)
                qt_scr[hk, rb * (RB // CHUNK) + cc] = stacked.T.astype(bf16)
        za_scr[rows, :] = _silu(_dot(h, wa_ref[:, OFF_ZA:OFF_ZA + 1024]))

    for a in range(2):
        vt = v_scr[a * CHUNK:HALO + T, :].T
        for hk in range(ATT_KV_HEADS):
            vt_scr[a, hk, :, 0:HALO + T - a * CHUNK] = vt[hk * HEAD_DIM:(hk + 1) * HEAD_DIM].astype(bf16)

    key_off = jax.lax.broadcasted_iota(jnp.int32, (KEY_BLOCK, 1), 0)
    units = [(hk, par) for hk in range(ATT_KV_HEADS) for par in range(2)]
    side_block = D_MODEL // len(units)

    def value_matmul(prev, n):
        pc, probs, recips = prev
        hk = units[n][0]
        vkeys = slice((pc // 2) * LANES, (pc // 2) * LANES + KEY_BLOCK)
        return _dot(vt_scr[pc % 2, hk, :, vkeys], probs[n]) * recips[n]

    def gate_and_store(pc, outs_t):
        prow = slice(pc * CHUNK, (pc + 1) * CHUNK)
        for hk in range(ATT_KV_HEADS):
            o = jnp.concatenate([outs_t[2 * hk], outs_t[2 * hk + 1]], axis=0).T
            for j in range(PAIRS_PER_KV):
                c0 = (hk * PAIRS_PER_KV + j) * LANES
                gated = o[j * CHUNK:(j + 1) * CHUNK] * za_scr[prow, c0:c0 + LANES]
                ag_scr[prow, c0:c0 + LANES] = gated.astype(bf16)

    prev = None
    for c in range(N_CHUNKS):
        r0 = c * CHUNK
        keys = slice(r0, r0 + KEY_BLOCK)
        valid = (key_off + (t * T + r0 - HALO)) >= 0
        side, rb = divmod(c, N_ROW_BLOCKS)
        srows = slice(rb * RB, (rb + 1) * RB)
        scores, raws, outs_t = [], [], []
        for n, (hk, par) in enumerate(units):
            w0 = SIDE_OFF[side] + n * side_block
            raws.append(_dot(h_scr[srows, 0:D_MODEL], wa_ref[:, w0:w0 + side_block]))
            if prev is not None:
                outs_t.append(value_matmul(prev, n))
            scores.append(_dot(k_scr[2 * hk + par, keys, :], qt_scr[hk, c]))
        if prev is not None:
            gate_and_store(prev[0], outs_t)

        raw = jnp.concatenate(raws, axis=1)
        if side == SIDE_U:
            uz_scr[srows, :] = _gelu_exact(raw)
        elif side == SIDE_GS:
            gs_scr[srows, :] = _sigmoid(raw + bm_ref[:, 1024:2048])
        elif side == SIDE_ZS:
            uz_scr[srows, :] = uz_scr[srows, :] * _silu(raw)
        else:
            ga_scr[srows, :] = _sigmoid(raw + bm_ref[:, 0:1024])

        probs, recips = [], []
        for (hk, par), s in zip(units, scores):
            if r0 < HALO:
                s = jnp.where(valid, s, MASK_VALUE)
            sink = sink_ref[2 * hk + par, 0:1, :] * LOG2E
            m = jnp.maximum(jnp.max(s, axis=0, keepdims=True), sink)
            p = jnp.exp2(s - m)
            denom = jnp.sum(p, axis=0, keepdims=True) + jnp.exp2(sink - m)
            probs.append(p.astype(bf16))
            recips.append(1.0 / denom)
        prev = (c, probs, recips)

    att_first, outs_t = [], []
    for n in range(len(units)):
        w0 = OFF_WAO + n * side_block
        att_first.append(_dot(ag_scr[0:RB, 0:D_MODEL], wout_ref[:, w0:w0 + side_block]))
        outs_t.append(value_matmul(prev, n))
    gate_and_store(prev[0], outs_t)
    att_first = jnp.concatenate(att_first, axis=1)

    pos_i = jax.lax.broadcasted_iota(jnp.int32, (SG_BLOCK, SG_BLOCK), 0) // CHUNK
    pos_j = jax.lax.broadcasted_iota(jnp.int32, (SG_BLOCK, SG_BLOCK), 1) // CHUNK
    causal = pos_j <= pos_i
    nblk = T // SG_BLOCK
    att_rest = []
    assert SG_GROUPS == 2 * (D_MODEL // side_block) and N_ROW_BLOCKS == 2
    for g in range(SG_GROUPS):
        if g % 2 == 0:
            w0 = OFF_WAO + (g // 2) * side_block
            att_rest.append(_dot(ag_scr[RB:T, 0:D_MODEL], wout_ref[:, w0:w0 + side_block]))
        wg = jnp.where(causal, sgw_ref[g], 0.0).astype(bf16)
        c0 = g * LANES
        rhs = jnp.concatenate(
            [vn_scr[b * SG_BLOCK:(b + 1) * SG_BLOCK, c0:c0 + LANES] for b in range(nblk)], axis=1)
        mixed = _dot(wg, rhs)
        for b in range(nblk):
            brows = slice(b * SG_BLOCK, (b + 1) * SG_BLOCK)
            blk = mixed[:, b * LANES:(b + 1) * LANES] + sgb_ref[g]
            sg_scr[brows, c0:c0 + LANES] = (uz_scr[brows, c0:c0 + LANES] * blk).astype(bf16)

    for rb in range(N_ROW_BLOCKS):
        rows = slice(rb * RB, (rb + 1) * RB)
        att = att_first if rb == 0 else jnp.concatenate(att_rest, axis=1)
        y = ga_scr[rows, :] * att
        y = y + gs_scr[rows, :] * _dot(sg_scr[rows, 0:D_MODEL], wout_ref[:, OFF_WSO:OFF_WSO + D_MODEL])
        xo = x_ref[0, rows, :] + _dot(y.astype(bf16), wout_ref[:, OFF_WO:OFF_WO + D_MODEL])
        ms2 = jnp.mean(xo * xo, axis=-1, keepdims=True)
        out_ref[0, rows, :] = xo * jax.lax.rsqrt(ms2 + EPS) * fg_ref[...]

    hnext_scr[:, 0:D_MODEL] = _pre_norm(xnext_ref[0], ng_ref)


def _rope_tables(seq, tile):
    lane = np.arange(LANES) % HEAD_DIM
    inv_freq = ROPE_THETA ** (-(np.arange(ROPE_HALF, dtype=np.float64) * 2.0) / ROPE_DIM)
    freq = np.where(lane < ROPE_DIM, inv_freq[lane % ROPE_HALF], 0.0)
    sign = np.where(lane < ROPE_HALF, -1.0, 1.0)

    def tables(pos):
        ang = pos.astype(np.float64)[:, None] * freq[None, :]
        return np.stack([np.cos(ang), np.sin(ang), np.sin(ang) * sign], axis=0)

    row_tab = tables(np.arange(tile))
    tile_tab = np.zeros((seq // tile, 8, LANES))
    tile_tab[:, 0:3, :] = np.transpose(tables(np.arange(seq // tile) * tile), (1, 0, 2))
    return jnp.asarray(row_tab, jnp.float32), jnp.asarray(tile_tab, jnp.float32)


@jax.jit
def kernel(x, norm_g, w_in, b_merge, att_sinks, sg_w, sg_b, sg_ln_g, sg_ln_b,
           w_att_out, w_sg_out, w_o, final_g):
    B, S, D = x.shape
    T = SEQ_TILE
    assert D == D_MODEL and S % T == 0 and w_in.shape == (1, D, IN_WIDTH)
    bf16 = jnp.bfloat16
    f32 = jnp.float32

    row_tab, tile_tab = _rope_tables(S, T)
    sinks = att_sinks[0].astype(f32).reshape(ATT_KV_HEADS, PAIRS_PER_KV, 2)
    sink_rows = jnp.repeat(jnp.transpose(sinks, (0, 2, 1)).reshape(4, PAIRS_PER_KV), CHUNK, axis=1)
    sink_rows = jnp.broadcast_to(sink_rows[:, None, :], (4, 8, STACK_ROWS))
    sgb = jnp.broadcast_to(sg_b[0].astype(f32)[:, :, None], (SG_GROUPS, SG_BLOCK, LANES))

    hbm = pl.BlockSpec(memory_space=pl.ANY)

    n_tiles = S // T

    def next_first_rows(b, t):
        flat = jnp.minimum(b * n_tiles + t + 1, B * n_tiles - 1)
        return (flat // n_tiles, (flat % n_tiles) * N_ROW_BLOCKS, 0)

    def const(shape):
        zeros = (0,) * len(shape)
        return pl.BlockSpec(shape, lambda b, t: zeros, pipeline_mode=pl.Buffered(1))

    grid_spec = pltpu.PrefetchScalarGridSpec(
        num_scalar_prefetch=0,
        grid=(B, S // T),
        in_specs=[
            pl.BlockSpec((1, T, D), lambda b, t: (b, t, 0)),
            pl.BlockSpec((1, ROW_BLOCK, D), next_first_rows),
            const((1, D)),
            hbm, hbm, hbm, hbm,
            const((1, 2 * D)),
            const((4, 8, STACK_ROWS)),
            const((SG_GROUPS, SG_BLOCK, SG_BLOCK)),
            const((SG_GROUPS, SG_BLOCK, LANES)),
            const((1, D)),
            const((1, D)),
            const((1, D)),
            const((3, T, LANES)),
            const((S // T, 8, LANES)),
        ],
        out_specs=pl.BlockSpec((1, T, D), lambda b, t: (b, t, 0)),
        scratch_shapes=[
            pltpu.VMEM((ROW_BLOCK, PAD_WIDTH), bf16),
            pltpu.VMEM((T, PAD_WIDTH), bf16),
            pltpu.VMEM((ATT_KV_HEADS, N_CHUNKS, LANES, STACK_ROWS), bf16),
            pltpu.VMEM((T, D), f32),
            pltpu.VMEM((4, HALO + T, LANES), bf16),
            pltpu.VMEM((HALO + T, LANES), f32),
            pltpu.VMEM((2, ATT_KV_HEADS, HEAD_DIM, HALO + T), bf16),
            pltpu.VMEM((T, PAD_WIDTH), bf16),
            pltpu.VMEM((T, PAD_WIDTH), bf16),
            pltpu.VMEM((T, PAD_WIDTH), bf16),
            pltpu.VMEM((T, D), f32),
            pltpu.VMEM((T, D), f32),
            pltpu.VMEM((T, D), f32),
            pltpu.VMEM((D, IN_WIDTH), bf16),
            pltpu.VMEM((D, OUT_W_WIDTH), bf16),
            pltpu.VMEM((LOAD_SLOTS, LOAD_ROWS, IN_WIDTH), f32),
            pltpu.VMEM((LOAD_SLOTS, 3, LOAD_ROWS, D), f32),
            pltpu.SemaphoreType.DMA((LOAD_SLOTS, 4)),
        ],
    )
    return pl.pallas_call(
        _block_kernel,
        grid_spec=grid_spec,
        out_shape=jax.ShapeDtypeStruct((B, S, D), x.dtype),
        compiler_params=pltpu.CompilerParams(
            dimension_semantics=("arbitrary", "arbitrary"),
            vmem_limit_bytes=VMEM_LIMIT_BYTES,
        ),
        name="hybrid_block",
    )(
        x, x, norm_g.astype(f32), w_in.astype(f32), w_att_out.astype(f32), w_sg_out.astype(f32),
        w_o.astype(f32), b_merge.astype(f32), sink_rows,
        sg_w[0].astype(f32), sgb, sg_ln_g.astype(f32), sg_ln_b.astype(f32),
        final_g.reshape(1, D).astype(f32), row_tab, tile_tab,
    )
```

```python
import numpy as np
import jax
import jax.numpy as jnp
from jax.experimental import pallas as pl
from jax.experimental.pallas import tpu as pltpu

D_MODEL = 1024
CHUNK = 64
EPS = 1e-6
ATT_HEADS = 16
ATT_KV_HEADS = 2
HEAD_DIM = 64
ATT_GROUP = ATT_HEADS // ATT_KV_HEADS
WINDOW_CHUNKS = 2
HALO = WINDOW_CHUNKS * CHUNK
KEY_BLOCK = HALO + CHUNK
ROPE_DIM = HEAD_DIM // 4
ROPE_HALF = ROPE_DIM // 2
ROPE_THETA = 500000.0
SG_BLOCK = 128
SG_GROUPS = 8
LANES = 128
PAIRS_PER_KV = ATT_GROUP // 2
STACK_ROWS = PAIRS_PER_KV * CHUNK
MASK_VALUE = -1e30

OFF_Q, OFF_K, OFF_V, OFF_ZA = 0, 1024, 1152, 1280
OFF_U, OFF_VS, OFF_ZS, OFF_GM = 2304, 3328, 4352, 5376
IN_WIDTH = 7424
N_SIDE = 4
OFF_GS = OFF_GM + 1024
SIDE_U, SIDE_GS, SIDE_ZS, SIDE_GA = range(N_SIDE)
SIDE_OFF = (OFF_U, OFF_GS, OFF_ZS, OFF_GM)
OFF_WAO, OFF_WSO, OFF_WO = 0, 1024, 2048
PAD_WIDTH = D_MODEL + LANES
OUT_W_WIDTH = 3 * D_MODEL + LANES
LOG2E = np.float32(1.4426950408889634)
LOAD_ROWS = 64
N_LOADS = D_MODEL // LOAD_ROWS
LOAD_SLOTS = 3
assert N_LOADS >= LOAD_SLOTS

SEQ_TILE = 512
ROW_BLOCK = 256
N_ROW_BLOCKS = SEQ_TILE // ROW_BLOCK
N_CHUNKS = SEQ_TILE // CHUNK
VMEM_LIMIT_BYTES = 58 * 1024 * 1024

assert N_CHUNKS == N_SIDE * N_ROW_BLOCKS


def _dot(a, b):
    return jnp.dot(a, b, preferred_element_type=jnp.float32)


def _sigmoid(x):
    return 0.5 * jnp.tanh(0.5 * x) + 0.5


def _silu(x):
    hx = 0.5 * x
    return hx * (1.0 + jnp.tanh(hx))


def _gelu_exact(x):
    return 0.5 * x * (1.0 + jax.lax.erf(x * np.float32(1.0 / np.sqrt(2.0))))


def _load_weights(w_in_hbm, w_out_hbm, wa_ref, wout_ref, stage_in, stage_out, sems):
    n_out = len(w_out_hbm)
    ahead = LOAD_SLOTS - 1

    def copies(i, slot):
        rows = pl.ds(pl.multiple_of(i * LOAD_ROWS, LOAD_ROWS), LOAD_ROWS)
        cps = [pltpu.make_async_copy(w_in_hbm.at[0, rows, :], stage_in.at[slot], sems.at[slot, 0])]
        for k in range(n_out):
            cps.append(pltpu.make_async_copy(w_out_hbm[k].at[0, rows, :], stage_out.at[slot, k],
                                             sems.at[slot, 1 + k]))
        return cps

    for i in range(ahead):
        for cp in copies(i, i):
            cp.start()

    def body(i, carry):
        slot = i % LOAD_SLOTS

        @pl.when(i + ahead < N_LOADS)
        def _():
            for cp in copies(i + ahead, (i + ahead) % LOAD_SLOTS):
                cp.start()

        for cp in copies(i, slot):
            cp.wait()
        rows = pl.ds(pl.multiple_of(i * LOAD_ROWS, LOAD_ROWS), LOAD_ROWS)
        wa_ref[rows, :] = stage_in[slot].astype(jnp.bfloat16)
        for k in range(n_out):
            wout_ref[rows, k * D_MODEL:(k + 1) * D_MODEL] = stage_out[slot, k].astype(jnp.bfloat16)
        return carry

    jax.lax.fori_loop(0, N_LOADS, body, 0)


def _pre_norm(x, ng_ref):
    ms = jnp.mean(x * x, axis=-1, keepdims=True)
    return (x * jax.lax.rsqrt(ms + EPS) * ng_ref[...]).astype(jnp.bfloat16)


def _block_kernel(x_ref, xnext_ref, ng_ref, w_in_hbm, wao_hbm, wso_hbm, wo_hbm, bm_ref, sink_ref,
                  sgw_ref, sgb_ref,
                  lng_ref, lnb_ref, fg_ref,
                  rrow_ref, rtile_ref,
                  out_ref,
                  hnext_scr, h_scr, qt_scr, za_scr, k_scr, v_scr, vt_scr, ag_scr, sg_scr, vn_scr, uz_scr, ga_scr, gs_scr,
                  wa_ref, wout_ref, stage_in, stage_out, load_sems):
    t = pl.program_id(1)
    T = SEQ_TILE
    RB = ROW_BLOCK
    bf16 = jnp.bfloat16

    @pl.when((pl.program_id(0) == 0) & (t == 0))
    def _():
        _load_weights(w_in_hbm, (wao_hbm, wso_hbm, wo_hbm), wa_ref, wout_ref,
                      stage_in, stage_out, load_sems)
        hnext_scr[:, 0:D_MODEL] = _pre_norm(x_ref[0, 0:RB, :], ng_ref)
        k_scr[:, T:T + HALO, :] = jnp.zeros((4, HALO, LANES), bf16)
        v_scr[T:T + HALO, :] = jnp.zeros((HALO, LANES), jnp.float32)

    carry = t != 0
    k_scr[:, 0:HALO, :] = jnp.where(carry, k_scr[:, T:T + HALO, :], jnp.zeros((4, HALO, LANES), bf16))
    v_scr[0:HALO, :] = jnp.where(carry, v_scr[T:T + HALO, :], 0.0)

    lane = jax.lax.broadcasted_iota(jnp.int32, (RB, LANES), 1)
    first_half = (lane % HEAD_DIM) < ROPE_HALF
    low_half = lane < HEAD_DIM
    scale = np.float32(HEAD_DIM ** -0.5) * LOG2E

    for rb in range(N_ROW_BLOCKS):
        rows = slice(rb * RB, (rb + 1) * RB)
        h = hnext_scr[:, 0:D_MODEL] if rb == 0 else _pre_norm(x_ref[0, rows, :], ng_ref)
        h_scr[rows, 0:D_MODEL] = h

        cb, sb, sb_sgn = rrow_ref[0, rows, :], rrow_ref[1, rows, :], rrow_ref[2, rows, :]
        ca, sa, sa_sgn = rtile_ref[t, 0:1, :], rtile_ref[t, 1:2, :], rtile_ref[t, 2:3, :]
        cos_t = ca * cb - sa * sb
        sin_t = sa_sgn * cb + ca * sb_sgn

        def rope(v):
            nxt = pltpu.roll(v, LANES - ROPE_HALF, 1)
            prv = pltpu.roll(v, ROPE_HALF, 1)
            return v * cos_t + jnp.where(first_half, nxt, prv) * sin_t

        kv2 = _dot(h, wa_ref[:, OFF_K:OFF_V + LANES])
        k2 = rope(kv2[:, 0:LANES])
        zeros = jnp.zeros_like(k2)
        krows = slice(HALO + rb * RB, HALO + (rb + 1) * RB)
        h0_lo = jnp.where(low_half, k2, zeros)
        h1_hi = jnp.where(low_half, zeros, k2)
        h0_hi = pltpu.roll(h0_lo, HEAD_DIM, 1)
        h1_lo = pltpu.roll(h1_hi, HEAD_DIM, 1)
        k_scr[0, krows, :] = h0_lo.astype(bf16)
        k_scr[1, krows, :] = h0_hi.astype(bf16)
        k_scr[2, krows, :] = h1_lo.astype(bf16)
        k_scr[3, krows, :] = h1_hi.astype(bf16)
        v_scr[krows, :] = kv2[:, LANES:2 * LANES]

        q = _dot(h, wa_ref[:, OFF_Q:OFF_Q + 1024])
        q_tiles = [rope(q[:, j * LANES:(j + 1) * LANES]) * scale for j in range(1024 // LANES)]
        for hk in range(ATT_KV_HEADS):
            for cc in range(RB // CHUNK):
                stacked = jnp.concatenate(
                    [q_tiles[hk * PAIRS_PER_KV + j][cc * CHUNK:(cc + 1) * CHUNK] for j in range(PAIRS_PER_KV)],
                    axis=0)
                qt_scr[hk, rb * (RB // CHUNK) + cc] = stacked.T.astype(bf16)
        za_scr[rows, :] = _silu(_dot(h, wa_ref[:, OFF_ZA:OFF_ZA + 1024]))

        vs = _gelu_exact(_dot(h, wa_ref[:, OFF_VS:OFF_VS + 1024]))
        mu = jnp.mean(vs, axis=-1, keepdims=True)
        vc = vs - mu
        var = jnp.mean(vc * vc, axis=-1, keepdims=True)
        vn_scr[rows, 0:D_MODEL] = (vc * jax.lax.rsqrt(var + EPS) * lng_ref[...]
                                   + lnb_ref[...]).astype(bf16)

    for a in range(2):
        vt = v_scr[a * CHUNK:HALO + T, :].T
        for hk in range(ATT_KV_HEADS):
            vt_scr[a, hk, :, 0:HALO + T - a * CHUNK] = vt[hk * HEAD_DIM:(hk + 1) * HEAD_DIM].astype(bf16)

    key_off = jax.lax.broadcasted_iota(jnp.int32, (KEY_BLOCK, 1), 0)
    units = [(hk, par) for hk in range(ATT_KV_HEADS) for par in range(2)]
    side_block = D_MODEL // len(units)

    def value_matmul(prev, n):
        pc, probs, recips = prev
        hk = units[n][0]
        vkeys = slice((pc // 2) * LANES, (pc // 2) * LANES + KEY_BLOCK)
        return _dot(vt_scr[pc % 2, hk, :, vkeys], probs[n]) * recips[n]

    def gate_and_store(pc, outs_t):
        prow = slice(pc * CHUNK, (pc + 1) * CHUNK)
        for hk in range(ATT_KV_HEADS):
            o = jnp.concatenate([outs_t[2 * hk], outs_t[2 * hk + 1]], axis=0).T
            for j in range(PAIRS_PER_KV):
                c0 = (hk * PAIRS_PER_KV + j) * LANES
                gated = o[j * CHUNK:(j + 1) * CHUNK] * za_scr[prow, c0:c0 + LANES]
                ag_scr[prow, c0:c0 + LANES] = gated.astype(bf16)

    prev = None
    for c in range(N_CHUNKS):
        r0 = c * CHUNK
        keys = slice(r0, r0 + KEY_BLOCK)
        valid = (key_off + (t * T + r0 - HALO)) >= 0
        side, rb = divmod(c, N_ROW_BLOCKS)
        srows = slice(rb * RB, (rb + 1) * RB)
        scores, raws, outs_t = [], [], []
        for n, (hk, par) in enumerate(units):
            w0 = SIDE_OFF[side] + n * side_block
            raws.append(_dot(h_scr[srows, 0:D_MODEL], wa_ref[:, w0:w0 + side_block]))
            if prev is not None:
                outs_t.append(value_matmul(prev, n))
            scores.append(_dot(k_scr[2 * hk + par, keys, :], qt_scr[hk, c]))
        if prev is not None:
            gate_and_store(prev[0], outs_t)

        raw = jnp.concatenate(raws, axis=1)
        if side == SIDE_U:
            uz_scr[srows, :] = _gelu_exact(raw)
        elif side == SIDE_GS:
            gs_scr[srows, :] = _sigmoid(raw + bm_ref[:, 1024:2048])
        elif side == SIDE_ZS:
            uz_scr[srows, :] = uz_scr[srows, :] * _silu(raw)
        else:
            ga_scr[srows, :] = _sigmoid(raw + bm_ref[:, 0:1024])

        probs, recips = [], []
        for (hk, par), s in zip(units, scores):
            if r0 < HALO:
                s = jnp.where(valid, s, MASK_VALUE)
            sink = sink_ref[2 * hk + par, 0:1, :] * LOG2E
            m = jnp.maximum(jnp.max(s, axis=0, keepdims=True), sink)
            p = jnp.exp2(s - m)
            denom = jnp.sum(p, axis=0, keepdims=True) + jnp.exp2(sink - m)
            probs.append(p.astype(bf16))
            recips.append(1.0 / denom)
        prev = (c, probs, recips)

    att_first, outs_t = [], []
    for n in range(len(units)):
        w0 = OFF_WAO + n * side_block
        att_first.append(_dot(ag_scr[0:RB, 0:D_MODEL], wout_ref[:, w0:w0 + side_block]))
        outs_t.append(value_matmul(prev, n))
    gate_and_store(prev[0], outs_t)
    att_first = jnp.concatenate(att_first, axis=1)

    pos_i = jax.lax.broadcasted_iota(jnp.int32, (SG_BLOCK, SG_BLOCK), 0) // CHUNK
    pos_j = jax.lax.broadcasted_iota(jnp.int32, (SG_BLOCK, SG_BLOCK), 1) // CHUNK
    causal = pos_j <= pos_i
    nblk = T // SG_BLOCK
    att_rest = []
    assert SG_GROUPS == 2 * (D_MODEL // side_block) and N_ROW_BLOCKS == 2
    for g in range(SG_GROUPS):
        if g % 2 == 0:
            w0 = OFF_WAO + (g // 2) * side_block
            att_rest.append(_dot(ag_scr[RB:T, 0:D_MODEL], wout_ref[:, w0:w0 + side_block]))
        wg = jnp.where(causal, sgw_ref[g], 0.0).astype(bf16)
        c0 = g * LANES
        rhs = jnp.concatenate(
            [vn_scr[b * SG_BLOCK:(b + 1) * SG_BLOCK, c0:c0 + LANES] for b in range(nblk)], axis=1)
        mixed = _dot(wg, rhs)
        for b in range(nblk):
            brows = slice(b * SG_BLOCK, (b + 1) * SG_BLOCK)
            blk = mixed[:, b * LANES:(b + 1) * LANES] + sgb_ref[g]
            sg_scr[brows, c0:c0 + LANES] = (uz_scr[brows, c0:c0 + LANES] * blk).astype(bf16)

    for rb in range(N_ROW_BLOCKS):
        rows = slice(rb * RB, (rb + 1) * RB)
        att = att_first if rb == 0 else jnp.concatenate(att_rest, axis=1)
        y = ga_scr[rows, :] * att
        y = y + gs_scr[rows, :] * _dot(sg_scr[rows, 0:D_MODEL], wout_ref[:, OFF_WSO:OFF_WSO + D_MODEL])
        xo = x_ref[0, rows, :] + _dot(y.astype(bf16), wout_ref[:, OFF_WO:OFF_WO + D_MODEL])
        ms2 = jnp.mean(xo * xo, axis=-1, keepdims=True)
        out_ref[0, rows, :] = xo * jax.lax.rsqrt(ms2 + EPS) * fg_ref[...]

    hnext_scr[:, 0:D_MODEL] = _pre_norm(xnext_ref[0], ng_ref)


def _rope_tables(seq, tile):
    lane = np.arange(LANES) % HEAD_DIM
    inv_freq = ROPE_THETA ** (-(np.arange(ROPE_HALF, dtype=np.float64) * 2.0) / ROPE_DIM)
    freq = np.where(lane < ROPE_DIM, inv_freq[lane % ROPE_HALF], 0.0)
    sign = np.where(lane < ROPE_HALF, -1.0, 1.0)

    def tables(pos):
        ang = pos.astype(np.float64)[:, None] * freq[None, :]
        return np.stack([np.cos(ang), np.sin(ang), np.sin(ang) * sign], axis=0)

    row_tab = tables(np.arange(tile))
    tile_tab = np.zeros((seq // tile, 8, LANES))
    tile_tab[:, 0:3, :] = np.transpose(tables(np.arange(seq // tile) * tile), (1, 0, 2))
    return jnp.asarray(row_tab, jnp.float32), jnp.asarray(tile_tab, jnp.float32)


@jax.jit
def kernel(x, norm_g, w_in, b_merge, att_sinks, sg_w, sg_b, sg_ln_g, sg_ln_b,
           w_att_out, w_sg_out, w_o, final_g):
    B, S, D = x.shape
    T = SEQ_TILE
    assert D == D_MODEL and S % T == 0 and w_in.shape == (1, D, IN_WIDTH)
    bf16 = jnp.bfloat16
    f32 = jnp.float32

    row_tab, tile_tab = _rope_tables(S, T)
    sinks = att_sinks[0].astype(f32).reshape(ATT_KV_HEADS, PAIRS_PER_KV, 2)
    sink_rows = jnp.repeat(jnp.transpose(sinks, (0, 2, 1)).reshape(4, PAIRS_PER_KV), CHUNK, axis=1)
    sink_rows = jnp.broadcast_to(sink_rows[:, None, :], (4, 8, STACK_ROWS))
    sgb = jnp.broadcast_to(sg_b[0].astype(f32)[:, :, None], (SG_GROUPS, SG_BLOCK, LANES))

    hbm = pl.BlockSpec(memory_space=pl.ANY)

    n_tiles = S // T

    def next_first_rows(b, t):
        flat = jnp.minimum(b * n_tiles + t + 1, B * n_tiles - 1)
        return (flat // n_tiles, (flat % n_tiles) * N_ROW_BLOCKS, 0)

    def const(shape):
        zeros = (0,) * len(shape)
        return pl.BlockSpec(shape, lambda b, t: zeros, pipeline_mode=pl.Buffered(1))

    grid_spec = pltpu.PrefetchScalarGridSpec(
        num_scalar_prefetch=0,
        grid=(B, S // T),
        in_specs=[
            pl.BlockSpec((1, T, D), lambda b, t: (b, t, 0)),
            pl.BlockSpec((1, ROW_BLOCK, D), next_first_rows),
            const((1, D)),
            hbm, hbm, hbm, hbm,
            const((1, 2 * D)),
            const((4, 8, STACK_ROWS)),
            const((SG_GROUPS, SG_BLOCK, SG_BLOCK)),
            const((SG_GROUPS, SG_BLOCK, LANES)),
            const((1, D)),
            const((1, D)),
            const((1, D)),
            const((3, T, LANES)),
            const((S // T, 8, LANES)),
        ],
        out_specs=pl.BlockSpec((1, T, D), lambda b, t: (b, t, 0)),
        scratch_shapes=[
            pltpu.VMEM((ROW_BLOCK, PAD_WIDTH), bf16),
            pltpu.VMEM((T, PAD_WIDTH), bf16),
            pltpu.VMEM((ATT_KV_HEADS, N_CHUNKS, LANES, STACK_ROWS), bf16),
            pltpu.VMEM((T, D), f32),
            pltpu.VMEM((4, HALO + T, LANES), bf16),
            pltpu.VMEM((HALO + T, LANES), f32),
            pltpu.VMEM((2, ATT_KV_HEADS, HEAD_DIM, HALO + T), bf16),
            pltpu.VMEM((T, PAD_WIDTH), bf16),
            pltpu.VMEM((T, PAD_WIDTH), bf16),
            pltpu.VMEM((T, PAD_WIDTH), bf16),
            pltpu.VMEM((T, D), f32),
            pltpu.VMEM((T, D), f32),
            pltpu.VMEM((T, D), f32),
            pltpu.VMEM((D, IN_WIDTH), bf16),
            pltpu.VMEM((D, OUT_W_WIDTH), bf16),
            pltpu.VMEM((LOAD_SLOTS, LOAD_ROWS, IN_WIDTH), f32),
            pltpu.VMEM((LOAD_SLOTS, 3, LOAD_ROWS, D), f32),
            pltpu.SemaphoreType.DMA((LOAD_SLOTS, 4)),
        ],
    )
    return pl.pallas_call(
        _block_kernel,
        grid_spec=grid_spec,
        out_shape=jax.ShapeDtypeStruct((B, S, D), x.dtype),
        compiler_params=pltpu.CompilerParams(
            dimension_semantics=("arbitrary", "arbitrary"),
            vmem_limit_bytes=VMEM_LIMIT_BYTES,
        ),
        name="hybrid_block",
    )(
        x, x, norm_g.astype(f32), w_in.astype(f32), w_att_out.astype(f32), w_sg_out.astype(f32),
        w_o.astype(f32), b_merge.astype(f32), sink_rows,
        sg_w[0].astype(f32), sgb, sg_ln_g.astype(f32), sg_ln_b.astype(f32),
        final_g.reshape(1, D).astype(f32), row_tab, tile_tab,
    )
```

```python
import numpy as np
import jax
import jax.numpy as jnp
from jax.experimental import pallas as pl
from jax.experimental.pallas import tpu as pltpu

D_MODEL = 1024
CHUNK = 64
EPS = 1e-6
ATT_HEADS = 16
ATT_KV_HEADS = 2
HEAD_DIM = 64
ATT_GROUP = ATT_HEADS // ATT_KV_HEADS
WINDOW_CHUNKS = 2
HALO = WINDOW_CHUNKS * CHUNK
KEY_BLOCK = HALO + CHUNK
ROPE_DIM = HEAD_DIM // 4
ROPE_HALF = ROPE_DIM // 2
ROPE_THETA = 500000.0
SG_BLOCK = 128
SG_GROUPS = 8
LANES = 128
PAIRS_PER_KV = ATT_GROUP // 2
STACK_ROWS = PAIRS_PER_KV * CHUNK
MASK_VALUE = -1e30

OFF_Q, OFF_K, OFF_V, OFF_ZA = 0, 1024, 1152, 1280
OFF_U, OFF_VS, OFF_ZS, OFF_GM = 2304, 3328, 4352, 5376
IN_WIDTH = 7424
N_SIDE = 4
OFF_GS = OFF_GM + 1024
SIDE_U, SIDE_GS, SIDE_ZS, SIDE_GA = range(N_SIDE)
SIDE_OFF = (OFF_U, OFF_GS, OFF_ZS, OFF_GM)
OFF_WAO, OFF_WSO, OFF_WO = 0, 1024, 2048
PAD_WIDTH = D_MODEL + LANES
OUT_W_WIDTH = 3 * D_MODEL + LANES
LOG2E = np.float32(1.4426950408889634)
LOAD_ROWS = 64
N_LOADS = D_MODEL // LOAD_ROWS
LOAD_SLOTS = 3
assert N_LOADS >= LOAD_SLOTS

SEQ_TILE = 512
ROW_BLOCK = 256
N_ROW_BLOCKS = SEQ_TILE // ROW_BLOCK
N_CHUNKS = SEQ_TILE // CHUNK
VMEM_LIMIT_BYTES = 58 * 1024 * 1024

assert N_CHUNKS == N_SIDE * N_ROW_BLOCKS


def _dot(a, b):
    return jnp.dot(a, b, preferred_element_type=jnp.float32)


def _sigmoid(x):
    return 0.5 * jnp.tanh(0.5 * x) + 0.5


def _silu(x):
    hx = 0.5 * x
    return hx * (1.0 + jnp.tanh(hx))


def _gelu_exact(x):
    return 0.5 * x * (1.0 + jax.lax.erf(x * np.float32(1.0 / np.sqrt(2.0))))


def _load_weights(w_in_hbm, w_out_hbm, wa_ref, wout_ref, stage_in, stage_out, sems):
    n_out = len(w_out_hbm)
    ahead = LOAD_SLOTS - 1

    def copies(i, slot):
        rows = pl.ds(pl.multiple_of(i * LOAD_ROWS, LOAD_ROWS), LOAD_ROWS)
        cps = [pltpu.make_async_copy(w_in_hbm.at[0, rows, :], stage_in.at[slot], sems.at[slot, 0])]
        for k in range(n_out):
            cps.append(pltpu.make_async_copy(w_out_hbm[k].at[0, rows, :], stage_out.at[slot, k],
                                             sems.at[slot, 1 + k]))
        return cps

    for i in range(ahead):
        for cp in copies(i, i):
            cp.start()

    def body(i, carry):
        slot = i % LOAD_SLOTS

        @pl.when(i + ahead < N_LOADS)
        def _():
            for cp in copies(i + ahead, (i + ahead) % LOAD_SLOTS):
                cp.start()

        for cp in copies(i, slot):
            cp.wait()
        rows = pl.ds(pl.multiple_of(i * LOAD_ROWS, LOAD_ROWS), LOAD_ROWS)
        wa_ref[rows, :] = stage_in[slot].astype(jnp.bfloat16)
        for k in range(n_out):
            wout_ref[rows, k * D_MODEL:(k + 1) * D_MODEL] = stage_out[slot, k].astype(jnp.bfloat16)
        return carry

    jax.lax.fori_loop(0, N_LOADS, body, 0)


def _pre_norm(x, ng_ref):
    ms = jnp.mean(x * x, axis=-1, keepdims=True)
    return (x * jax.lax.rsqrt(ms + EPS) * ng_ref[...]).astype(jnp.bfloat16)


def _block_kernel(x_ref, xnext_ref, ng_ref, w_in_hbm, wao_hbm, wso_hbm, wo_hbm, bm_ref, sink_ref,
                  sgw_ref, sgb_ref,
                  lng_ref, lnb_ref, fg_ref,
                  rrow_ref, rtile_ref,
                  out_ref,
                  hnext_scr, h_scr, qt_scr, za_scr, k_scr, v_scr, vt_scr, ag_scr, sg_scr, vn_scr, uz_scr, ga_scr, gs_scr,
                  wa_ref, wout_ref, stage_in, stage_out, load_sems):
    t = pl.program_id(1)
    T = SEQ_TILE
    RB = ROW_BLOCK
    bf16 = jnp.bfloat16

    @pl.when((pl.program_id(0) == 0) & (t == 0))
    def _():
        _load_weights(w_in_hbm, (wao_hbm, wso_hbm, wo_hbm), wa_ref, wout_ref,
                      stage_in, stage_out, load_sems)
        hnext_scr[:, 0:D_MODEL] = _pre_norm(x_ref[0, 0:RB, :], ng_ref)
        k_scr[:, T:T + HALO, :] = jnp.zeros((4, HALO, LANES), bf16)
        v_scr[T:T + HALO, :] = jnp.zeros((HALO, LANES), jnp.float32)

    carry = t != 0
    k_scr[:, 0:HALO, :] = jnp.where(carry, k_scr[:, T:T + HALO, :], jnp.zeros((4, HALO, LANES), bf16))
    v_scr[0:HALO, :] = jnp.where(carry, v_scr[T:T + HALO, :], 0.0)

    lane = jax.lax.broadcasted_iota(jnp.int32, (RB, LANES), 1)
    first_half = (lane % HEAD_DIM) < ROPE_HALF
    low_half = lane < HEAD_DIM
    scale = np.float32(HEAD_DIM ** -0.5) * LOG2E

    for rb in range(N_ROW_BLOCKS):
        rows = slice(rb * RB, (rb + 1) * RB)
        h = hnext_scr[:, 0:D_MODEL] if rb == 0 else _pre_norm(x_ref[0, rows, :], ng_ref)
        h_scr[rows, 0:D_MODEL] = h

        cb, sb, sb_sgn = rrow_ref[0, rows, :], rrow_ref[1, rows, :], rrow_ref[2, rows, :]
        ca, sa, sa_sgn = rtile_ref[t, 0:1, :], rtile_ref[t, 1:2, :], rtile_ref[t, 2:3, :]
        cos_t = ca * cb - sa * sb
        sin_t = sa_sgn * cb + ca * sb_sgn

        def rope(v):
            nxt = pltpu.roll(v, LANES - ROPE_HALF, 1)
            prv = pltpu.roll(v, ROPE_HALF, 1)
            return v * cos_t + jnp.where(first_half, nxt, prv) * sin_t

        kv2 = _dot(h, wa_ref[:, OFF_K:OFF_V + LANES])
        k2 = rope(kv2[:, 0:LANES])
        zeros = jnp.zeros_like(k2)
        krows = slice(HALO + rb * RB, HALO + (rb + 1) * RB)
        h0_lo = jnp.where(low_half, k2, zeros)
        h1_hi = jnp.where(low_half, zeros, k2)
        h0_hi = pltpu.roll(h0_lo, HEAD_DIM, 1)
        h1_lo = pltpu.roll(h1_hi, HEAD_DIM, 1)
        k_scr[0, krows, :] = h0_lo.astype(bf16)
        k_scr[1, krows, :] = h0_hi.astype(bf16)
        k_scr[2, krows, :] = h1_lo.astype(bf16)
        k_scr[3, krows, :] = h1_hi.astype(bf16)
        v_scr[krows, :] = kv2[:, LANES:2 * LANES]

        q = _dot(h, wa_ref[:, OFF_Q:OFF_Q + 1024])
        q_tiles = [rope(q[:, j * LANES:(j + 1) * LANES]) * scale for j in range(1024 // LANES)]
        for hk in range(ATT_KV_HEADS):
            for cc in range(RB // CHUNK):
                stacked = jnp.concatenate(
                    [q_tiles[hk * PAIRS_PER_KV + j][cc * CHUNK:(cc + 1) * CHUNK] for j in range(PAIRS_PER_KV)],
                    axis=0)
                qt_scr[hk, rb * (RB // CHUNK) + cc] = stacked.T.astype(bf16)
        za_scr[rows, :] = _silu(_dot(h, wa_ref[:, OFF_ZA:OFF_ZA + 1024]))

        vs = _gelu_exact(_dot(h, wa_ref[:, OFF_VS:OFF_VS + 1024]))
        mu = jnp.mean(vs, axis=-1, keepdims=True)
        vc = vs - mu
        var = jnp.mean(vc * vc, axis=-1, keepdims=True)
        vn_scr[rows, 0:D_MODEL] = (vc * jax.lax.rsqrt(var + EPS) * lng_ref[...]
                                   + lnb_ref[...]).astype(bf16)

    for a in range(2):
        vt = v_scr[a * CHUNK:HALO + T, :].T
        for hk in range(ATT_KV_HEADS):
            vt_scr[a, hk, :, 0:HALO + T - a * CHUNK] = vt[hk * HEAD_DIM:(hk + 1) * HEAD_DIM].astype(bf16)

    key_off = jax.lax.broadcasted_iota(jnp.int32, (KEY_BLOCK, 1), 0)
    units = [(hk, par) for hk in range(ATT_KV_HEADS) for par in range(2)]
    side_block = D_MODEL // len(units)

    def value_matmul(prev, n):
        pc, probs, recips = prev
        hk = units[n][0]
        vkeys = slice((pc // 2) * LANES, (pc // 2) * LANES + KEY_BLOCK)
        return _dot(vt_scr[pc % 2, hk, :, vkeys], probs[n]) * recips[n]

    def gate_and_store(pc, outs_t):
        prow = slice(pc * CHUNK, (pc + 1) * CHUNK)
        for hk in range(ATT_KV_HEADS):
            o = jnp.concatenate([outs_t[2 * hk], outs_t[2 * hk + 1]], axis=0).T
            for j in range(PAIRS_PER_KV):
                c0 = (hk * PAIRS_PER_KV + j) * LANES
                gated = o[j * CHUNK:(j + 1) * CHUNK] * za_scr[prow, c0:c0 + LANES]
                ag_scr[prow, c0:c0 + LANES] = gated.astype(bf16)

    prev = None
    for c in range(N_CHUNKS):
        r0 = c * CHUNK
        keys = slice(r0, r0 + KEY_BLOCK)
        valid = (key_off + (t * T + r0 - HALO)) >= 0
        side, rb = divmod(c, N_ROW_BLOCKS)
        srows = slice(rb * RB, (rb + 1) * RB)
        scores, raws, outs_t = [], [], []
        for n, (hk, par) in enumerate(units):
            w0 = SIDE_OFF[side] + n * side_block
            raws.append(_dot(h_scr[srows, 0:D_MODEL], wa_ref[:, w0:w0 + side_block]))
            if prev is not None:
                outs_t.append(value_matmul(prev, n))
            scores.append(_dot(k_scr[2 * hk + par, keys, :], qt_scr[hk, c]))
        if prev is not None:
            gate_and_store(prev[0], outs_t)

        raw = jnp.concatenate(raws, axis=1)
        if side == SIDE_U:
            uz_scr[srows, :] = _gelu_exact(raw)
        elif side == SIDE_GS:
            gs_scr[srows, :] = _sigmoid(raw + bm_ref[:, 1024:2048])
        elif side == SIDE_ZS:
            uz_scr[srows, :] = uz_scr[srows, :] * _silu(raw)
        else:
            ga_scr[srows, :] = _sigmoid(raw + bm_ref[:, 0:1024])

        probs, recips = [], []
        for (hk, par), s in zip(units, scores):
            if r0 < HALO:
                s = jnp.where(valid, s, MASK_VALUE)
            sink = sink_ref[2 * hk + par, 0:1, :]
            m = jnp.maximum(jnp.max(s, axis=0, keepdims=True), sink)
            p = jnp.exp2(s - m)
            denom = jnp.sum(p, axis=0, keepdims=True) + jnp.exp2(sink - m)
            probs.append(p.astype(bf16))
            recips.append(1.0 / denom)
        prev = (c, probs, recips)

    att_first, outs_t = [], []
    for n in range(len(units)):
        w0 = OFF_WAO + n * side_block
        att_first.append(_dot(ag_scr[0:RB, 0:D_MODEL], wout_ref[:, w0:w0 + side_block]))
        outs_t.append(value_matmul(prev, n))
    gate_and_store(prev[0], outs_t)
    att_first = jnp.concatenate(att_first, axis=1)

    pos_i = jax.lax.broadcasted_iota(jnp.int32, (SG_BLOCK, SG_BLOCK), 0) // CHUNK
    pos_j = jax.lax.broadcasted_iota(jnp.int32, (SG_BLOCK, SG_BLOCK), 1) // CHUNK
    causal = pos_j <= pos_i
    nblk = T // SG_BLOCK
    att_rest = []
    assert SG_GROUPS == 2 * (D_MODEL // side_block) and N_ROW_BLOCKS == 2
    for g in range(SG_GROUPS):
        if g % 2 == 0:
            w0 = OFF_WAO + (g // 2) * side_block
            att_rest.append(_dot(ag_scr[RB:T, 0:D_MODEL], wout_ref[:, w0:w0 + side_block]))
        wg = jnp.where(causal, sgw_ref[g], 0.0).astype(bf16)
        c0 = g * LANES
        rhs = jnp.concatenate(
            [vn_scr[b * SG_BLOCK:(b + 1) * SG_BLOCK, c0:c0 + LANES] for b in range(nblk)], axis=1)
        mixed = _dot(wg, rhs)
        for b in range(nblk):
            brows = slice(b * SG_BLOCK, (b + 1) * SG_BLOCK)
            blk = mixed[:, b * LANES:(b + 1) * LANES] + sgb_ref[g]
            sg_scr[brows, c0:c0 + LANES] = (uz_scr[brows, c0:c0 + LANES] * blk).astype(bf16)

    for rb in range(N_ROW_BLOCKS):
        rows = slice(rb * RB, (rb + 1) * RB)
        att = att_first if rb == 0 else jnp.concatenate(att_rest, axis=1)
        y = ga_scr[rows, :] * att
        y = y + gs_scr[rows, :] * _dot(sg_scr[rows, 0:D_MODEL], wout_ref[:, OFF_WSO:OFF_WSO + D_MODEL])
        xo = x_ref[0, rows, :] + _dot(y.astype(bf16), wout_ref[:, OFF_WO:OFF_WO + D_MODEL])
        ms2 = jnp.mean(xo * xo, axis=-1, keepdims=True)
        out_ref[0, rows, :] = xo * jax.lax.rsqrt(ms2 + EPS) * fg_ref[...]

    hnext_scr[:, 0:D_MODEL] = _pre_norm(xnext_ref[0], ng_ref)


def _rope_tables(seq, tile):
    lane = np.arange(LANES) % HEAD_DIM
    inv_freq = ROPE_THETA ** (-(np.arange(ROPE_HALF, dtype=np.float64) * 2.0) / ROPE_DIM)
    freq = np.where(lane < ROPE_DIM, inv_freq[lane % ROPE_HALF], 0.0)
    sign = np.where(lane < ROPE_HALF, -1.0, 1.0)

    def tables(pos):
        ang = pos.astype(np.float64)[:, None] * freq[None, :]
        return np.stack([np.cos(ang), np.sin(ang), np.sin(ang) * sign], axis=0)

    row_tab = tables(np.arange(tile))
    tile_tab = np.zeros((seq // tile, 8, LANES))
    tile_tab[:, 0:3, :] = np.transpose(tables(np.arange(seq // tile) * tile), (1, 0, 2))
    return jnp.asarray(row_tab, jnp.float32), jnp.asarray(tile_tab, jnp.float32)


@jax.jit
def kernel(x, norm_g, w_in, b_merge, att_sinks, sg_w, sg_b, sg_ln_g, sg_ln_b,
           w_att_out, w_sg_out, w_o, final_g):
    B, S, D = x.shape
    T = SEQ_TILE
    assert D == D_MODEL and S % T == 0 and w_in.shape == (1, D, IN_WIDTH)
    bf16 = jnp.bfloat16
    f32 = jnp.float32

    row_tab, tile_tab = _rope_tables(S, T)
    sinks = (att_sinks[0].astype(f32) * LOG2E).reshape(ATT_KV_HEADS, PAIRS_PER_KV, 2)
    sink_rows = jnp.repeat(jnp.transpose(sinks, (0, 2, 1)).reshape(4, PAIRS_PER_KV), CHUNK, axis=1)
    sink_rows = jnp.broadcast_to(sink_rows[:, None, :], (4, 8, STACK_ROWS))
    sgb = jnp.broadcast_to(sg_b[0].astype(f32)[:, :, None], (SG_GROUPS, SG_BLOCK, LANES))

    hbm = pl.BlockSpec(memory_space=pl.ANY)

    n_tiles = S // T

    def next_first_rows(b, t):
        flat = jnp.minimum(b * n_tiles + t + 1, B * n_tiles - 1)
        return (flat // n_tiles, (flat % n_tiles) * N_ROW_BLOCKS, 0)

    def const(shape):
        zeros = (0,) * len(shape)
        return pl.BlockSpec(shape, lambda b, t: zeros, pipeline_mode=pl.Buffered(1))

    grid_spec = pltpu.PrefetchScalarGridSpec(
        num_scalar_prefetch=0,
        grid=(B, S // T),
        in_specs=[
            pl.BlockSpec((1, T, D), lambda b, t: (b, t, 0)),
            pl.BlockSpec((1, ROW_BLOCK, D), next_first_rows),
            const((1, D)),
            hbm, hbm, hbm, hbm,
            const((1, 2 * D)),
            const((4, 8, STACK_ROWS)),
            const((SG_GROUPS, SG_BLOCK, SG_BLOCK)),
            const((SG_GROUPS, SG_BLOCK, LANES)),
            const((1, D)),
            const((1, D)),
            const((1, D)),
            const((3, T, LANES)),
            const((S // T, 8, LANES)),
        ],
        out_specs=pl.BlockSpec((1, T, D), lambda b, t: (b, t, 0)),
        scratch_shapes=[
            pltpu.VMEM((ROW_BLOCK, PAD_WIDTH), bf16),
            pltpu.VMEM((T, PAD_WIDTH), bf16),
            pltpu.VMEM((ATT_KV_HEADS, N_CHUNKS, LANES, STACK_ROWS), bf16),
            pltpu.VMEM((T, D), f32),
            pltpu.VMEM((4, HALO + T, LANES), bf16),
            pltpu.VMEM((HALO + T, LANES), f32),
            pltpu.VMEM((2, ATT_KV_HEADS, HEAD_DIM, HALO + T), bf16),
            pltpu.VMEM((T, PAD_WIDTH), bf16),
            pltpu.VMEM((T, PAD_WIDTH), bf16),
            pltpu.VMEM((T, PAD_WIDTH), bf16),
            pltpu.VMEM((T, D), f32),
            pltpu.VMEM((T, D), f32),
            pltpu.VMEM((T, D), f32),
            pltpu.VMEM((D, IN_WIDTH), bf16),
            pltpu.VMEM((D, OUT_W_WIDTH), bf16),
            pltpu.VMEM((LOAD_SLOTS, LOAD_ROWS, IN_WIDTH), f32),
            pltpu.VMEM((LOAD_SLOTS, 3, LOAD_ROWS, D), f32),
            pltpu.SemaphoreType.DMA((LOAD_SLOTS, 4)),
        ],
    )
    return pl.pallas_call(
        _block_kernel,
        grid_spec=grid_spec,
        out_shape=jax.ShapeDtypeStruct((B, S, D), x.dtype),
        compiler_params=pltpu.CompilerParams(
            dimension_semantics=("arbitrary", "arbitrary"),
            vmem_limit_bytes=VMEM_LIMIT_BYTES,
        ),
        name="hybrid_block",
    )(
        x, x, norm_g.astype(f32), w_in.astype(f32), w_att_out.astype(f32), w_sg_out.astype(f32),
        w_o.astype(f32), b_merge.astype(f32), sink_rows,
        sg_w[0].astype(f32), sgb, sg_ln_g.astype(f32), sg_ln_b.astype(f32),
        final_g.reshape(1, D).astype(f32), row_tab, tile_tab,
    )
```

```python
import numpy as np
import jax
import jax.numpy as jnp
from jax.experimental import pallas as pl
from jax.experimental.pallas import tpu as pltpu

D_MODEL = 1024
CHUNK = 64
EPS = 1e-6
ATT_HEADS = 16
ATT_KV_HEADS = 2
HEAD_DIM = 64
ATT_GROUP = ATT_HEADS // ATT_KV_HEADS
WINDOW_CHUNKS = 2
HALO = WINDOW_CHUNKS * CHUNK
KEY_BLOCK = HALO + CHUNK
ROPE_DIM = HEAD_DIM // 4
ROPE_HALF = ROPE_DIM // 2
ROPE_THETA = 500000.0
SG_BLOCK = 128
SG_GROUPS = 8
LANES = 128
PAIRS_PER_KV = ATT_GROUP // 2
STACK_ROWS = PAIRS_PER_KV * CHUNK
MASK_VALUE = -1e30

OFF_Q, OFF_K, OFF_V, OFF_ZA = 0, 1024, 1152, 1280
OFF_U, OFF_VS, OFF_ZS, OFF_GM = 2304, 3328, 4352, 5376
IN_WIDTH = 7424
N_SIDE = 4
OFF_GS = OFF_GM + 1024
SIDE_U, SIDE_GS, SIDE_ZS, SIDE_GA = range(N_SIDE)
SIDE_OFF = (OFF_U, OFF_GS, OFF_ZS, OFF_GM)
OFF_WAO, OFF_WSO, OFF_WO = 0, 1024, 2048
PAD_WIDTH = D_MODEL + LANES
OUT_W_WIDTH = 3 * D_MODEL + LANES
LOG2E = np.float32(1.4426950408889634)
LOAD_ROWS = 64
N_LOADS = D_MODEL // LOAD_ROWS
LOAD_SLOTS = 3
assert N_LOADS >= LOAD_SLOTS

SEQ_TILE = 512
ROW_BLOCK = 256
N_ROW_BLOCKS = SEQ_TILE // ROW_BLOCK
N_CHUNKS = SEQ_TILE // CHUNK
VMEM_LIMIT_BYTES = 58 * 1024 * 1024

VALUE_LAG = 2
assert 1 <= VALUE_LAG <= 2 * ATT_KV_HEADS
assert N_CHUNKS == N_SIDE * N_ROW_BLOCKS


def _dot(a, b):
    return jnp.dot(a, b, preferred_element_type=jnp.float32)


def _sigmoid(x):
    return 0.5 * jnp.tanh(0.5 * x) + 0.5


def _silu(x):
    hx = 0.5 * x
    return hx * (1.0 + jnp.tanh(hx))


def _gelu_exact(x):
    return 0.5 * x * (1.0 + jax.lax.erf(x * np.float32(1.0 / np.sqrt(2.0))))


def _load_weights(w_in_hbm, w_out_hbm, wa_ref, wout_ref, stage_in, stage_out, sems):
    n_out = len(w_out_hbm)
    ahead = LOAD_SLOTS - 1

    def copies(i, slot):
        rows = pl.ds(pl.multiple_of(i * LOAD_ROWS, LOAD_ROWS), LOAD_ROWS)
        cps = [pltpu.make_async_copy(w_in_hbm.at[0, rows, :], stage_in.at[slot], sems.at[slot, 0])]
        for k in range(n_out):
            cps.append(pltpu.make_async_copy(w_out_hbm[k].at[0, rows, :], stage_out.at[slot, k],
                                             sems.at[slot, 1 + k]))
        return cps

    for i in range(ahead):
        for cp in copies(i, i):
            cp.start()

    def body(i, carry):
        slot = i % LOAD_SLOTS

        @pl.when(i + ahead < N_LOADS)
        def _():
            for cp in copies(i + ahead, (i + ahead) % LOAD_SLOTS):
                cp.start()

        for cp in copies(i, slot):
            cp.wait()
        rows = pl.ds(pl.multiple_of(i * LOAD_ROWS, LOAD_ROWS), LOAD_ROWS)
        wa_ref[rows, :] = stage_in[slot].astype(jnp.bfloat16)
        for k in range(n_out):
            wout_ref[rows, k * D_MODEL:(k + 1) * D_MODEL] = stage_out[slot, k].astype(jnp.bfloat16)
        return carry

    jax.lax.fori_loop(0, N_LOADS, body, 0)


def _pre_norm(x, ng_ref):
    ms = jnp.mean(x * x, axis=-1, keepdims=True)
    return (x * jax.lax.rsqrt(ms + EPS) * ng_ref[...]).astype(jnp.bfloat16)


def _block_kernel(x_ref, xnext_ref, ng_ref, w_in_hbm, wao_hbm, wso_hbm, wo_hbm, bm_ref, sink_ref,
                  sgw_ref, sgb_ref,
                  lng_ref, lnb_ref, fg_ref,
                  rrow_ref, rtile_ref,
                  out_ref,
                  hnext_scr, h_scr, qt_scr, za_scr, k_scr, v_scr, vt_scr, ag_scr, sg_scr, vn_scr, uz_scr, ga_scr, gs_scr,
                  wa_ref, wout_ref, stage_in, stage_out, load_sems):
    t = pl.program_id(1)
    T = SEQ_TILE
    RB = ROW_BLOCK
    bf16 = jnp.bfloat16

    @pl.when((pl.program_id(0) == 0) & (t == 0))
    def _():
        _load_weights(w_in_hbm, (wao_hbm, wso_hbm, wo_hbm), wa_ref, wout_ref,
                      stage_in, stage_out, load_sems)
        hnext_scr[:, 0:D_MODEL] = _pre_norm(x_ref[0, 0:RB, :], ng_ref)
        k_scr[:, T:T + HALO, :] = jnp.zeros((4, HALO, LANES), bf16)
        v_scr[T:T + HALO, :] = jnp.zeros((HALO, LANES), jnp.float32)

    carry = t != 0
    k_scr[:, 0:HALO, :] = jnp.where(carry, k_scr[:, T:T + HALO, :], jnp.zeros((4, HALO, LANES), bf16))
    v_scr[0:HALO, :] = jnp.where(carry, v_scr[T:T + HALO, :], 0.0)

    lane = jax.lax.broadcasted_iota(jnp.int32, (RB, LANES), 1)
    first_half = (lane % HEAD_DIM) < ROPE_HALF
    low_half = lane < HEAD_DIM
    scale = np.float32(HEAD_DIM ** -0.5) * LOG2E

    for rb in range(N_ROW_BLOCKS):
        rows = slice(rb * RB, (rb + 1) * RB)
        h = hnext_scr[:, 0:D_MODEL] if rb == 0 else _pre_norm(x_ref[0, rows, :], ng_ref)
        h_scr[rows, 0:D_MODEL] = h

        cb, sb, sb_sgn = rrow_ref[0, rows, :], rrow_ref[1, rows, :], rrow_ref[2, rows, :]
        ca, sa, sa_sgn = rtile_ref[t, 0:1, :], rtile_ref[t, 1:2, :], rtile_ref[t, 2:3, :]
        cos_t = ca * cb - sa * sb
        sin_t = sa_sgn * cb + ca * sb_sgn

        def rope(v):
            nxt = pltpu.roll(v, LANES - ROPE_HALF, 1)
            prv = pltpu.roll(v, ROPE_HALF, 1)
            return v * cos_t + jnp.where(first_half, nxt, prv) * sin_t

        kv2 = _dot(h, wa_ref[:, OFF_K:OFF_V + LANES])
        k2 = rope(kv2[:, 0:LANES])
        zeros = jnp.zeros_like(k2)
        krows = slice(HALO + rb * RB, HALO + (rb + 1) * RB)
        h0_lo = jnp.where(low_half, k2, zeros)
        h1_hi = jnp.where(low_half, zeros, k2)
        h0_hi = pltpu.roll(h0_lo, HEAD_DIM, 1)
        h1_lo = pltpu.roll(h1_hi, HEAD_DIM, 1)
        k_scr[0, krows, :] = h0_lo.astype(bf16)
        k_scr[1, krows, :] = h0_hi.astype(bf16)
        k_scr[2, krows, :] = h1_lo.astype(bf16)
        k_scr[3, krows, :] = h1_hi.astype(bf16)
        v_scr[krows, :] = kv2[:, LANES:2 * LANES]

        q = _dot(h, wa_ref[:, OFF_Q:OFF_Q + 1024])
        q_tiles = [rope(q[:, j * LANES:(j + 1) * LANES]) * scale for j in range(1024 // LANES)]
        for hk in range(ATT_KV_HEADS):
            for cc in range(RB // CHUNK):
                stacked = jnp.concatenate(
                    [q_tiles[hk * PAIRS_PER_KV + j][cc * CHUNK:(cc + 1) * CHUNK] for j in range(PAIRS_PER_KV)],
                    axis=0)
                qt_scr[hk, rb * (RB // CHUNK) + cc] = stacked.T.astype(bf16)
        za_scr[rows, :] = _silu(_dot(h, wa_ref[:, OFF_ZA:OFF_ZA + 1024]))

        vs = _gelu_exact(_dot(h, wa_ref[:, OFF_VS:OFF_VS + 1024]))
        mu = jnp.mean(vs, axis=-1, keepdims=True)
        vc = vs - mu
        var = jnp.mean(vc * vc, axis=-1, keepdims=True)
        vn_scr[rows, 0:D_MODEL] = (vc * jax.lax.rsqrt(var + EPS) * lng_ref[...]
                                   + lnb_ref[...]).astype(bf16)

    for a in range(2):
        vt = v_scr[a * CHUNK:HALO + T, :].T
        for hk in range(ATT_KV_HEADS):
            vt_scr[a, hk, :, 0:HALO + T - a * CHUNK] = vt[hk * HEAD_DIM:(hk + 1) * HEAD_DIM].astype(bf16)

    key_off = jax.lax.broadcasted_iota(jnp.int32, (KEY_BLOCK, 1), 0)
    units = [(hk, par) for hk in range(ATT_KV_HEADS) for par in range(2)]
    side_block = D_MODEL // len(units)

    n_units = len(units)
    softmaxed = {}
    outs_t = {}

    def value_matmul(u):
        pc, n = divmod(u, n_units)
        hk = units[n][0]
        vkeys = slice((pc // 2) * LANES, (pc // 2) * LANES + KEY_BLOCK)
        probs, recip = softmaxed.pop(u)
        outs_t[u] = _dot(vt_scr[pc % 2, hk, :, vkeys], probs) * recip
        if n % 2 == 1:
            prow = slice(pc * CHUNK, (pc + 1) * CHUNK)
            o = jnp.concatenate([outs_t.pop(u - 1), outs_t.pop(u)], axis=0).T
            for j in range(PAIRS_PER_KV):
                c0 = (hk * PAIRS_PER_KV + j) * LANES
                gated = o[j * CHUNK:(j + 1) * CHUNK] * za_scr[prow, c0:c0 + LANES]
                ag_scr[prow, c0:c0 + LANES] = gated.astype(bf16)

    for c in range(N_CHUNKS):
        r0 = c * CHUNK
        keys = slice(r0, r0 + KEY_BLOCK)
        valid = (key_off + (t * T + r0 - HALO)) >= 0
        side, rb = divmod(c, N_ROW_BLOCKS)
        srows = slice(rb * RB, (rb + 1) * RB)
        raws = []
        for n, (hk, par) in enumerate(units):
            u = c * n_units + n
            w0 = SIDE_OFF[side] + n * side_block
            raws.append(_dot(h_scr[srows, 0:D_MODEL], wa_ref[:, w0:w0 + side_block]))
            if u >= VALUE_LAG:
                value_matmul(u - VALUE_LAG)
            s = _dot(k_scr[2 * hk + par, keys, :], qt_scr[hk, c])
            if r0 < HALO:
                s = jnp.where(valid, s, MASK_VALUE)
            sink = sink_ref[2 * hk + par, 0:1, :] * LOG2E
            m = jnp.maximum(jnp.max(s, axis=0, keepdims=True), sink)
            p = jnp.exp2(s - m)
            denom = jnp.sum(p, axis=0, keepdims=True) + jnp.exp2(sink - m)
            softmaxed[u] = (p.astype(bf16), 1.0 / denom)

        raw = jnp.concatenate(raws, axis=1)
        if side == SIDE_U:
            uz_scr[srows, :] = _gelu_exact(raw)
        elif side == SIDE_GS:
            gs_scr[srows, :] = _sigmoid(raw + bm_ref[:, 1024:2048])
        elif side == SIDE_ZS:
            uz_scr[srows, :] = uz_scr[srows, :] * _silu(raw)
        else:
            ga_scr[srows, :] = _sigmoid(raw + bm_ref[:, 0:1024])

    att_first = []
    for n in range(n_units):
        w0 = OFF_WAO + n * side_block
        att_first.append(_dot(ag_scr[0:RB, 0:D_MODEL], wout_ref[:, w0:w0 + side_block]))
        if n < VALUE_LAG:
            value_matmul(N_CHUNKS * n_units - VALUE_LAG + n)
    att_first = jnp.concatenate(att_first, axis=1)

    pos_i = jax.lax.broadcasted_iota(jnp.int32, (SG_BLOCK, SG_BLOCK), 0) // CHUNK
    pos_j = jax.lax.broadcasted_iota(jnp.int32, (SG_BLOCK, SG_BLOCK), 1) // CHUNK
    causal = pos_j <= pos_i
    nblk = T // SG_BLOCK
    att_rest = []
    assert SG_GROUPS == 2 * (D_MODEL // side_block) and N_ROW_BLOCKS == 2
    for g in range(SG_GROUPS):
        if g % 2 == 0:
            w0 = OFF_WAO + (g // 2) * side_block
            att_rest.append(_dot(ag_scr[RB:T, 0:D_MODEL], wout_ref[:, w0:w0 + side_block]))
        wg = jnp.where(causal, sgw_ref[g], 0.0).astype(bf16)
        c0 = g * LANES
        rhs = jnp.concatenate(
            [vn_scr[b * SG_BLOCK:(b + 1) * SG_BLOCK, c0:c0 + LANES] for b in range(nblk)], axis=1)
        mixed = _dot(wg, rhs)
        for b in range(nblk):
            brows = slice(b * SG_BLOCK, (b + 1) * SG_BLOCK)
            blk = mixed[:, b * LANES:(b + 1) * LANES] + sgb_ref[g]
            sg_scr[brows, c0:c0 + LANES] = (uz_scr[brows, c0:c0 + LANES] * blk).astype(bf16)

    for rb in range(N_ROW_BLOCKS):
        rows = slice(rb * RB, (rb + 1) * RB)
        att = att_first if rb == 0 else jnp.concatenate(att_rest, axis=1)
        y = ga_scr[rows, :] * att
        y = y + gs_scr[rows, :] * _dot(sg_scr[rows, 0:D_MODEL], wout_ref[:, OFF_WSO:OFF_WSO + D_MODEL])
        xo = x_ref[0, rows, :] + _dot(y.astype(bf16), wout_ref[:, OFF_WO:OFF_WO + D_MODEL])
        ms2 = jnp.mean(xo * xo, axis=-1, keepdims=True)
        out_ref[0, rows, :] = xo * jax.lax.rsqrt(ms2 + EPS) * fg_ref[...]

    hnext_scr[:, 0:D_MODEL] = _pre_norm(xnext_ref[0], ng_ref)


def _rope_tables(seq, tile):
    lane = np.arange(LANES) % HEAD_DIM
    inv_freq = ROPE_THETA ** (-(np.arange(ROPE_HALF, dtype=np.float64) * 2.0) / ROPE_DIM)
    freq = np.where(lane < ROPE_DIM, inv_freq[lane % ROPE_HALF], 0.0)
    sign = np.where(lane < ROPE_HALF, -1.0, 1.0)

    def tables(pos):
        ang = pos.astype(np.float64)[:, None] * freq[None, :]
        return np.stack([np.cos(ang), np.sin(ang), np.sin(ang) * sign], axis=0)

    row_tab = tables(np.arange(tile))
    tile_tab = np.zeros((seq // tile, 8, LANES))
    tile_tab[:, 0:3, :] = np.transpose(tables(np.arange(seq // tile) * tile), (1, 0, 2))
    return jnp.asarray(row_tab, jnp.float32), jnp.asarray(tile_tab, jnp.float32)


@jax.jit
def kernel(x, norm_g, w_in, b_merge, att_sinks, sg_w, sg_b, sg_ln_g, sg_ln_b,
           w_att_out, w_sg_out, w_o, final_g):
    B, S, D = x.shape
    T = SEQ_TILE
    assert D == D_MODEL and S % T == 0 and w_in.shape == (1, D, IN_WIDTH)
    bf16 = jnp.bfloat16
    f32 = jnp.float32

    row_tab, tile_tab = _rope_tables(S, T)
    sinks = att_sinks[0].astype(f32).reshape(ATT_KV_HEADS, PAIRS_PER_KV, 2)
    sink_rows = jnp.repeat(jnp.transpose(sinks, (0, 2, 1)).reshape(4, PAIRS_PER_KV), CHUNK, axis=1)
    sink_rows = jnp.broadcast_to(sink_rows[:, None, :], (4, 8, STACK_ROWS))
    sgb = jnp.broadcast_to(sg_b[0].astype(f32)[:, :, None], (SG_GROUPS, SG_BLOCK, LANES))

    hbm = pl.BlockSpec(memory_space=pl.ANY)

    n_tiles = S // T

    def next_first_rows(b, t):
        flat = jnp.minimum(b * n_tiles + t + 1, B * n_tiles - 1)
        return (flat // n_tiles, (flat % n_tiles) * N_ROW_BLOCKS, 0)

    def const(shape):
        zeros = (0,) * len(shape)
        return pl.BlockSpec(shape, lambda b, t: zeros, pipeline_mode=pl.Buffered(1))

    grid_spec = pltpu.PrefetchScalarGridSpec(
        num_scalar_prefetch=0,
        grid=(B, S // T),
        in_specs=[
            pl.BlockSpec((1, T, D), lambda b, t: (b, t, 0)),
            pl.BlockSpec((1, ROW_BLOCK, D), next_first_rows),
            const((1, D)),
            hbm, hbm, hbm, hbm,
            const((1, 2 * D)),
            const((4, 8, STACK_ROWS)),
            const((SG_GROUPS, SG_BLOCK, SG_BLOCK)),
            const((SG_GROUPS, SG_BLOCK, LANES)),
            const((1, D)),
            const((1, D)),
            const((1, D)),
            const((3, T, LANES)),
            const((S // T, 8, LANES)),
        ],
        out_specs=pl.BlockSpec((1, T, D), lambda b, t: (b, t, 0)),
        scratch_shapes=[
            pltpu.VMEM((ROW_BLOCK, PAD_WIDTH), bf16),
            pltpu.VMEM((T, PAD_WIDTH), bf16),
            pltpu.VMEM((ATT_KV_HEADS, N_CHUNKS, LANES, STACK_ROWS), bf16),
            pltpu.VMEM((T, D), f32),
            pltpu.VMEM((4, HALO + T, LANES), bf16),
            pltpu.VMEM((HALO + T, LANES), f32),
            pltpu.VMEM((2, ATT_KV_HEADS, HEAD_DIM, HALO + T), bf16),
            pltpu.VMEM((T, PAD_WIDTH), bf16),
            pltpu.VMEM((T, PAD_WIDTH), bf16),
            pltpu.VMEM((T, PAD_WIDTH), bf16),
            pltpu.VMEM((T, D), f32),
            pltpu.VMEM((T, D), f32),
            pltpu.VMEM((T, D), f32),
            pltpu.VMEM((D, IN_WIDTH), bf16),
            pltpu.VMEM((D, OUT_W_WIDTH), bf16),
            pltpu.VMEM((LOAD_SLOTS, LOAD_ROWS, IN_WIDTH), f32),
            pltpu.VMEM((LOAD_SLOTS, 3, LOAD_ROWS, D), f32),
            pltpu.SemaphoreType.DMA((LOAD_SLOTS, 4)),
        ],
    )
    return pl.pallas_call(
        _block_kernel,
        grid_spec=grid_spec,
        out_shape=jax.ShapeDtypeStruct((B, S, D), x.dtype),
        compiler_params=pltpu.CompilerParams(
            dimension_semantics=("arbitrary", "arbitrary"),
            vmem_limit_bytes=VMEM_LIMIT_BYTES,
        ),
        name="hybrid_block",
    )(
        x, x, norm_g.astype(f32), w_in.astype(f32), w_att_out.astype(f32), w_sg_out.astype(f32),
        w_o.astype(f32), b_merge.astype(f32), sink_rows,
        sg_w[0].astype(f32), sgb, sg_ln_g.astype(f32), sg_ln_b.astype(f32),
        final_g.reshape(1, D).astype(f32), row_tab, tile_tab,
    )
```

```python
import numpy as np
import jax
import jax.numpy as jnp
from jax.experimental import pallas as pl
from jax.experimental.pallas import tpu as pltpu

D_MODEL = 1024
CHUNK = 64
EPS = 1e-6
ATT_HEADS = 16
ATT_KV_HEADS = 2
HEAD_DIM = 64
ATT_GROUP = ATT_HEADS // ATT_KV_HEADS
WINDOW_CHUNKS = 2
HALO = WINDOW_CHUNKS * CHUNK
KEY_BLOCK = HALO + CHUNK
ROPE_DIM = HEAD_DIM // 4
ROPE_HALF = ROPE_DIM // 2
ROPE_THETA = 500000.0
SG_BLOCK = 128
SG_GROUPS = 8
LANES = 128
PAIRS_PER_KV = ATT_GROUP // 2
STACK_ROWS = PAIRS_PER_KV * CHUNK
MASK_VALUE = -1e30

OFF_Q, OFF_K, OFF_V, OFF_ZA = 0, 1024, 1152, 1280
OFF_U, OFF_VS, OFF_ZS, OFF_GM = 2304, 3328, 4352, 5376
IN_WIDTH = 7424
N_SIDE = 4
OFF_GS = OFF_GM + 1024
SIDE_U, SIDE_GS, SIDE_ZS, SIDE_GA = range(N_SIDE)
SIDE_OFF = (OFF_U, OFF_GS, OFF_ZS, OFF_GM)
OFF_WAO, OFF_WSO, OFF_WO = 0, 1024, 2048
PAD_WIDTH = D_MODEL + LANES
OUT_W_WIDTH = 3 * D_MODEL + LANES
LOG2E = np.float32(1.4426950408889634)
LOAD_ROWS = 64
N_LOADS = D_MODEL // LOAD_ROWS
LOAD_SLOTS = 3
assert N_LOADS >= LOAD_SLOTS

SEQ_TILE = 512
ROW_BLOCK = 256
N_ROW_BLOCKS = SEQ_TILE // ROW_BLOCK
N_CHUNKS = SEQ_TILE // CHUNK
VMEM_LIMIT_BYTES = 58 * 1024 * 1024

VALUE_LAG = 4
assert 1 <= VALUE_LAG <= 2 * ATT_KV_HEADS
assert N_CHUNKS == N_SIDE * N_ROW_BLOCKS


def _dot(a, b):
    return jnp.dot(a, b, preferred_element_type=jnp.float32)


def _sigmoid(x):
    return 0.5 * jnp.tanh(0.5 * x) + 0.5


def _silu(x):
    hx = 0.5 * x
    return hx * (1.0 + jnp.tanh(hx))


def _gelu_exact(x):
    return 0.5 * x * (1.0 + jax.lax.erf(x * np.float32(1.0 / np.sqrt(2.0))))


def _load_weights(w_in_hbm, w_out_hbm, wa_ref, wout_ref, stage_in, stage_out, sems):
    n_out = len(w_out_hbm)
    ahead = LOAD_SLOTS - 1

    def copies(i, slot):
        rows = pl.ds(pl.multiple_of(i * LOAD_ROWS, LOAD_ROWS), LOAD_ROWS)
        cps = [pltpu.make_async_copy(w_in_hbm.at[0, rows, :], stage_in.at[slot], sems.at[slot, 0])]
        for k in range(n_out):
            cps.append(pltpu.make_async_copy(w_out_hbm[k].at[0, rows, :], stage_out.at[slot, k],
                                             sems.at[slot, 1 + k]))
        return cps

    for i in range(ahead):
        for cp in copies(i, i):
            cp.start()

    def body(i, carry):
        slot = i % LOAD_SLOTS

        @pl.when(i + ahead < N_LOADS)
        def _():
            for cp in copies(i + ahead, (i + ahead) % LOAD_SLOTS):
                cp.start()

        for cp in copies(i, slot):
            cp.wait()
        rows = pl.ds(pl.multiple_of(i * LOAD_ROWS, LOAD_ROWS), LOAD_ROWS)
        wa_ref[rows, :] = stage_in[slot].astype(jnp.bfloat16)
        for k in range(n_out):
            wout_ref[rows, k * D_MODEL:(k + 1) * D_MODEL] = stage_out[slot, k].astype(jnp.bfloat16)
        return carry

    jax.lax.fori_loop(0, N_LOADS, body, 0)


def _pre_norm(x, ng_ref):
    ms = jnp.mean(x * x, axis=-1, keepdims=True)
    return (x * jax.lax.rsqrt(ms + EPS) * ng_ref[...]).astype(jnp.bfloat16)


def _block_kernel(x_ref, xnext_ref, ng_ref, w_in_hbm, wao_hbm, wso_hbm, wo_hbm, bm_ref, sink_ref,
                  sgw_ref, sgb_ref,
                  lng_ref, lnb_ref, fg_ref,
                  rrow_ref, rtile_ref,
                  out_ref,
                  hnext_scr, h_scr, qt_scr, za_scr, k_scr, v_scr, vt_scr, ag_scr, sg_scr, vn_scr, uz_scr, ga_scr, gs_scr,
                  wa_ref, wout_ref, stage_in, stage_out, load_sems):
    t = pl.program_id(1)
    T = SEQ_TILE
    RB = ROW_BLOCK
    bf16 = jnp.bfloat16

    @pl.when((pl.program_id(0) == 0) & (t == 0))
    def _():
        _load_weights(w_in_hbm, (wao_hbm, wso_hbm, wo_hbm), wa_ref, wout_ref,
                      stage_in, stage_out, load_sems)
        hnext_scr[:, 0:D_MODEL] = _pre_norm(x_ref[0, 0:RB, :], ng_ref)
        k_scr[:, T:T + HALO, :] = jnp.zeros((4, HALO, LANES), bf16)
        v_scr[T:T + HALO, :] = jnp.zeros((HALO, LANES), jnp.float32)

    carry = t != 0
    k_scr[:, 0:HALO, :] = jnp.where(carry, k_scr[:, T:T + HALO, :], jnp.zeros((4, HALO, LANES), bf16))
    v_scr[0:HALO, :] = jnp.where(carry, v_scr[T:T + HALO, :], 0.0)

    lane = jax.lax.broadcasted_iota(jnp.int32, (RB, LANES), 1)
    first_half = (lane % HEAD_DIM) < ROPE_HALF
    low_half = lane < HEAD_DIM
    scale = np.float32(HEAD_DIM ** -0.5) * LOG2E

    for rb in range(N_ROW_BLOCKS):
        rows = slice(rb * RB, (rb + 1) * RB)
        h = hnext_scr[:, 0:D_MODEL] if rb == 0 else _pre_norm(x_ref[0, rows, :], ng_ref)
        h_scr[rows, 0:D_MODEL] = h

        cb, sb, sb_sgn = rrow_ref[0, rows, :], rrow_ref[1, rows, :], rrow_ref[2, rows, :]
        ca, sa, sa_sgn = rtile_ref[t, 0:1, :], rtile_ref[t, 1:2, :], rtile_ref[t, 2:3, :]
        cos_t = ca * cb - sa * sb
        sin_t = sa_sgn * cb + ca * sb_sgn

        def rope(v):
            nxt = pltpu.roll(v, LANES - ROPE_HALF, 1)
            prv = pltpu.roll(v, ROPE_HALF, 1)
            return v * cos_t + jnp.where(first_half, nxt, prv) * sin_t

        kv2 = _dot(h, wa_ref[:, OFF_K:OFF_V + LANES])
        k2 = rope(kv2[:, 0:LANES])
        zeros = jnp.zeros_like(k2)
        krows = slice(HALO + rb * RB, HALO + (rb + 1) * RB)
        h0_lo = jnp.where(low_half, k2, zeros)
        h1_hi = jnp.where(low_half, zeros, k2)
        h0_hi = pltpu.roll(h0_lo, HEAD_DIM, 1)
        h1_lo = pltpu.roll(h1_hi, HEAD_DIM, 1)
        k_scr[0, krows, :] = h0_lo.astype(bf16)
        k_scr[1, krows, :] = h0_hi.astype(bf16)
        k_scr[2, krows, :] = h1_lo.astype(bf16)
        k_scr[3, krows, :] = h1_hi.astype(bf16)
        v_scr[krows, :] = kv2[:, LANES:2 * LANES]

        q = _dot(h, wa_ref[:, OFF_Q:OFF_Q + 1024])
        q_tiles = [rope(q[:, j * LANES:(j + 1) * LANES]) * scale for j in range(1024 // LANES)]
        for hk in range(ATT_KV_HEADS):
            for cc in range(RB // CHUNK):
                stacked = jnp.concatenate(
                    [q_tiles[hk * PAIRS_PER_KV + j][cc * CHUNK:(cc + 1) * CHUNK] for j in range(PAIRS_PER_KV)],
                    axis=0)
                qt_scr[hk, rb * (RB // CHUNK) + cc] = stacked.T.astype(bf16)
        za_scr[rows, :] = _silu(_dot(h, wa_ref[:, OFF_ZA:OFF_ZA + 1024]))

        vs = _gelu_exact(_dot(h, wa_ref[:, OFF_VS:OFF_VS + 1024]))
        mu = jnp.mean(vs, axis=-1, keepdims=True)
        vc = vs - mu
        var = jnp.mean(vc * vc, axis=-1, keepdims=True)
        vn_scr[rows, 0:D_MODEL] = (vc * jax.lax.rsqrt(var + EPS) * lng_ref[...]
                                   + lnb_ref[...]).astype(bf16)

    for a in range(2):
        vt = v_scr[a * CHUNK:HALO + T, :].T
        for hk in range(ATT_KV_HEADS):
            vt_scr[a, hk, :, 0:HALO + T - a * CHUNK] = vt[hk * HEAD_DIM:(hk + 1) * HEAD_DIM].astype(bf16)

    key_off = jax.lax.broadcasted_iota(jnp.int32, (KEY_BLOCK, 1), 0)
    units = [(hk, par) for hk in range(ATT_KV_HEADS) for par in range(2)]
    side_block = D_MODEL // len(units)

    n_units = len(units)
    softmaxed = {}
    outs_t = {}

    def value_matmul(u):
        pc, n = divmod(u, n_units)
        hk = units[n][0]
        vkeys = slice((pc // 2) * LANES, (pc // 2) * LANES + KEY_BLOCK)
        probs, recip = softmaxed.pop(u)
        outs_t[u] = _dot(vt_scr[pc % 2, hk, :, vkeys], probs) * recip
        if n % 2 == 1:
            prow = slice(pc * CHUNK, (pc + 1) * CHUNK)
            o = jnp.concatenate([outs_t.pop(u - 1), outs_t.pop(u)], axis=0).T
            for j in range(PAIRS_PER_KV):
                c0 = (hk * PAIRS_PER_KV + j) * LANES
                gated = o[j * CHUNK:(j + 1) * CHUNK] * za_scr[prow, c0:c0 + LANES]
                ag_scr[prow, c0:c0 + LANES] = gated.astype(bf16)

    for c in range(N_CHUNKS):
        r0 = c * CHUNK
        keys = slice(r0, r0 + KEY_BLOCK)
        valid = (key_off + (t * T + r0 - HALO)) >= 0
        side, rb = divmod(c, N_ROW_BLOCKS)
        srows = slice(rb * RB, (rb + 1) * RB)
        raws = []
        for n, (hk, par) in enumerate(units):
            u = c * n_units + n
            w0 = SIDE_OFF[side] + n * side_block
            raws.append(_dot(h_scr[srows, 0:D_MODEL], wa_ref[:, w0:w0 + side_block]))
            if u >= VALUE_LAG:
                value_matmul(u - VALUE_LAG)
            s = _dot(k_scr[2 * hk + par, keys, :], qt_scr[hk, c])
            if r0 < HALO:
                s = jnp.where(valid, s, MASK_VALUE)
            sink = sink_ref[2 * hk + par, 0:1, :] * LOG2E
            m = jnp.maximum(jnp.max(s, axis=0, keepdims=True), sink)
            p = jnp.exp2(s - m)
            denom = jnp.sum(p, axis=0, keepdims=True) + jnp.exp2(sink - m)
            softmaxed[u] = (p.astype(bf16), 1.0 / denom)

        raw = jnp.concatenate(raws, axis=1)
        if side == SIDE_U:
            uz_scr[srows, :] = _gelu_exact(raw)
        elif side == SIDE_GS:
            gs_scr[srows, :] = _sigmoid(raw + bm_ref[:, 1024:2048])
        elif side == SIDE_ZS:
            uz_scr[srows, :] = uz_scr[srows, :] * _silu(raw)
        else:
            ga_scr[srows, :] = _sigmoid(raw + bm_ref[:, 0:1024])

    att_first = []
    for n in range(n_units):
        w0 = OFF_WAO + n * side_block
        att_first.append(_dot(ag_scr[0:RB, 0:D_MODEL], wout_ref[:, w0:w0 + side_block]))
        if n < VALUE_LAG:
            value_matmul(N_CHUNKS * n_units - VALUE_LAG + n)
    att_first = jnp.concatenate(att_first, axis=1)

    pos_i = jax.lax.broadcasted_iota(jnp.int32, (SG_BLOCK, SG_BLOCK), 0) // CHUNK
    pos_j = jax.lax.broadcasted_iota(jnp.int32, (SG_BLOCK, SG_BLOCK), 1) // CHUNK
    causal = pos_j <= pos_i
    nblk = T // SG_BLOCK
    att_rest = []
    assert SG_GROUPS == 2 * (D_MODEL // side_block) and N_ROW_BLOCKS == 2
    for g in range(SG_GROUPS):
        if g % 2 == 0:
            w0 = OFF_WAO + (g // 2) * side_block
            att_rest.append(_dot(ag_scr[RB:T, 0:D_MODEL], wout_ref[:, w0:w0 + side_block]))
        wg = jnp.where(causal, sgw_ref[g], 0.0).astype(bf16)
        c0 = g * LANES
        rhs = jnp.concatenate(
            [vn_scr[b * SG_BLOCK:(b + 1) * SG_BLOCK, c0:c0 + LANES] for b in range(nblk)], axis=1)
        mixed = _dot(wg, rhs)
        for b in range(nblk):
            brows = slice(b * SG_BLOCK, (b + 1) * SG_BLOCK)
            blk = mixed[:, b * LANES:(b + 1) * LANES] + sgb_ref[g]
            sg_scr[brows, c0:c0 + LANES] = (uz_scr[brows, c0:c0 + LANES] * blk).astype(bf16)

    for rb in range(N_ROW_BLOCKS):
        rows = slice(rb * RB, (rb + 1) * RB)
        att = att_first if rb == 0 else jnp.concatenate(att_rest, axis=1)
        y = ga_scr[rows, :] * att
        y = y + gs_scr[rows, :] * _dot(sg_scr[rows, 0:D_MODEL], wout_ref[:, OFF_WSO:OFF_WSO + D_MODEL])
        xo = x_ref[0, rows, :] + _dot(y.astype(bf16), wout_ref[:, OFF_WO:OFF_WO + D_MODEL])
        ms2 = jnp.mean(xo * xo, axis=-1, keepdims=True)
        out_ref[0, rows, :] = xo * jax.lax.rsqrt(ms2 + EPS) * fg_ref[...]

    hnext_scr[:, 0:D_MODEL] = _pre_norm(xnext_ref[0], ng_ref)


def _rope_tables(seq, tile):
    lane = np.arange(LANES) % HEAD_DIM
    inv_freq = ROPE_THETA ** (-(np.arange(ROPE_HALF, dtype=np.float64) * 2.0) / ROPE_DIM)
    freq = np.where(lane < ROPE_DIM, inv_freq[lane % ROPE_HALF], 0.0)
    sign = np.where(lane < ROPE_HALF, -1.0, 1.0)

    def tables(pos):
        ang = pos.astype(np.float64)[:, None] * freq[None, :]
        return np.stack([np.cos(ang), np.sin(ang), np.sin(ang) * sign], axis=0)

    row_tab = tables(np.arange(tile))
    tile_tab = np.zeros((seq // tile, 8, LANES))
    tile_tab[:, 0:3, :] = np.transpose(tables(np.arange(seq // tile) * tile), (1, 0, 2))
    return jnp.asarray(row_tab, jnp.float32), jnp.asarray(tile_tab, jnp.float32)


@jax.jit
def kernel(x, norm_g, w_in, b_merge, att_sinks, sg_w, sg_b, sg_ln_g, sg_ln_b,
           w_att_out, w_sg_out, w_o, final_g):
    B, S, D = x.shape
    T = SEQ_TILE
    assert D == D_MODEL and S % T == 0 and w_in.shape == (1, D, IN_WIDTH)
    bf16 = jnp.bfloat16
    f32 = jnp.float32

    row_tab, tile_tab = _rope_tables(S, T)
    sinks = att_sinks[0].astype(f32).reshape(ATT_KV_HEADS, PAIRS_PER_KV, 2)
    sink_rows = jnp.repeat(jnp.transpose(sinks, (0, 2, 1)).reshape(4, PAIRS_PER_KV), CHUNK, axis=1)
    sink_rows = jnp.broadcast_to(sink_rows[:, None, :], (4, 8, STACK_ROWS))
    sgb = jnp.broadcast_to(sg_b[0].astype(f32)[:, :, None], (SG_GROUPS, SG_BLOCK, LANES))

    hbm = pl.BlockSpec(memory_space=pl.ANY)

    n_tiles = S // T

    def next_first_rows(b, t):
        flat = jnp.minimum(b * n_tiles + t + 1, B * n_tiles - 1)
        return (flat // n_tiles, (flat % n_tiles) * N_ROW_BLOCKS, 0)

    def const(shape):
        zeros = (0,) * len(shape)
        return pl.BlockSpec(shape, lambda b, t: zeros, pipeline_mode=pl.Buffered(1))

    grid_spec = pltpu.PrefetchScalarGridSpec(
        num_scalar_prefetch=0,
        grid=(B, S // T),
        in_specs=[
            pl.BlockSpec((1, T, D), lambda b, t: (b, t, 0)),
            pl.BlockSpec((1, ROW_BLOCK, D), next_first_rows),
            const((1, D)),
            hbm, hbm, hbm, hbm,
            const((1, 2 * D)),
            const((4, 8, STACK_ROWS)),
            const((SG_GROUPS, SG_BLOCK, SG_BLOCK)),
            const((SG_GROUPS, SG_BLOCK, LANES)),
            const((1, D)),
            const((1, D)),
            const((1, D)),
            const((3, T, LANES)),
            const((S // T, 8, LANES)),
        ],
        out_specs=pl.BlockSpec((1, T, D), lambda b, t: (b, t, 0)),
        scratch_shapes=[
            pltpu.VMEM((ROW_BLOCK, PAD_WIDTH), bf16),
            pltpu.VMEM((T, PAD_WIDTH), bf16),
            pltpu.VMEM((ATT_KV_HEADS, N_CHUNKS, LANES, STACK_ROWS), bf16),
            pltpu.VMEM((T, D), f32),
            pltpu.VMEM((4, HALO + T, LANES), bf16),
            pltpu.VMEM((HALO + T, LANES), f32),
            pltpu.VMEM((2, ATT_KV_HEADS, HEAD_DIM, HALO + T), bf16),
            pltpu.VMEM((T, PAD_WIDTH), bf16),
            pltpu.VMEM((T, PAD_WIDTH), bf16),
            pltpu.VMEM((T, PAD_WIDTH), bf16),
            pltpu.VMEM((T, D), f32),
            pltpu.VMEM((T, D), f32),
            pltpu.VMEM((T, D), f32),
            pltpu.VMEM((D, IN_WIDTH), bf16),
            pltpu.VMEM((D, OUT_W_WIDTH), bf16),
            pltpu.VMEM((LOAD_SLOTS, LOAD_ROWS, IN_WIDTH), f32),
            pltpu.VMEM((LOAD_SLOTS, 3, LOAD_ROWS, D), f32),
            pltpu.SemaphoreType.DMA((LOAD_SLOTS, 4)),
        ],
    )
    return pl.pallas_call(
        _block_kernel,
        grid_spec=grid_spec,
        out_shape=jax.ShapeDtypeStruct((B, S, D), x.dtype),
        compiler_params=pltpu.CompilerParams(
            dimension_semantics=("arbitrary", "arbitrary"),
            vmem_limit_bytes=VMEM_LIMIT_BYTES,
        ),
        name="hybrid_block",
    )(
        x, x, norm_g.astype(f32), w_in.astype(f32), w_att_out.astype(f32), w_sg_out.astype(f32),
        w_o.astype(f32), b_merge.astype(f32), sink_rows,
        sg_w[0].astype(f32), sgb, sg_ln_g.astype(f32), sg_ln_b.astype(f32),
        final_g.reshape(1, D).astype(f32), row_tab, tile_tab,
    )
```
